```python
import math
import jax, jax.numpy as jnp
from jax import lax
import numpy as np

D_MODEL = 1024
BATCH = 4
SEQ = 4096
DEPTH = 2
DEC_BATCH = 128
DEC_SEQ = 4
PAST_LEN = 2048
PAGE_SIZE = 128

N_DN_LAYERS = (DEPTH + 1) // 2
N_ATT_LAYERS = DEPTH // 2
DN_HEADS = 4
DN_DK = 128
DN_DV = 128
DN_WIDTH = DN_HEADS * DN_DV
QKV_WIDTH = 2 * DN_HEADS * DN_DK + DN_WIDTH
DN_CONV = 4
DN_CHUNK = 64
SC_WIDTH = D_MODEL // 2
SC_CONV = 3
ATT_HEADS = 8
ATT_HEAD_DIM = D_MODEL // ATT_HEADS
ATT_WIDTH = ATT_HEADS * ATT_HEAD_DIM
MOBA_BLOCK = 256
MOBA_TOPK = 3
MOBA_Q_CHUNK = 32
DN_IN = QKV_WIDTH + DN_WIDTH + 2 * DN_HEADS + 4 * SC_WIDTH
ATT_IN = 4 * ATT_WIDTH
EPS = 1e-6

kernel_name = 'hybrid_deltanet_shortconv_moba_adaln_step'


def rms_norm(x, g):
    xf = x.astype(jnp.float32)
    y = xf * lax.rsqrt(jnp.mean(xf * xf, axis=-1, keepdims=True) + EPS)
    return (y * g.astype(jnp.float32)).astype(x.dtype)


def l2_norm(x):
    xf = x.astype(jnp.float32)
    return xf * lax.rsqrt(jnp.sum(xf * xf, axis=-1, keepdims=True) + EPS)


def causal_conv(x, w, buf):
    width = w.shape[0]
    t = x.shape[1]
    xp = jnp.concatenate([buf.astype(x.dtype), x], axis=1)
    y = sum(xp[:, j:j + t] * w[j] for j in range(width))
    return y, xp[:, t:]


def modulate(x, c, norm_g, ada_w, ada_b):
    mod = (c @ ada_w + ada_b)[:, None, :]
    shift, scale, gate = jnp.split(mod, 3, axis=-1)
    return rms_norm(x, norm_g) * (1 + scale) + shift, gate


def gated_delta_rule(q, k, v, g, beta, s0):
    n, t, h, _ = q.shape
    dv = v.shape[-1]
    c = DN_CHUNK
    pad = (-t) % c

    def blocks(a):
        a = jnp.pad(a.astype(jnp.float32), [(0, 0), (0, pad)] + [(0, 0)] * (a.ndim - 2))
        a = jnp.moveaxis(a, 2, 1)
        return a.reshape((n, h, -1, c) + a.shape[3:])

    q, k, v, g, beta = (blocks(a) for a in (q, k, v, g, beta))
    gc = jnp.cumsum(g, axis=-1)
    causal = jnp.tril(jnp.ones((c, c), dtype=bool))
    strict = jnp.tril(jnp.ones((c, c), dtype=bool), -1)
    decay = jnp.exp(jnp.where(causal, gc[..., :, None] - gc[..., None, :], -jnp.inf))
    kb = k * beta[..., None]
    m = jnp.where(strict, jnp.einsum('nhbid,nhbjd->nhbij', kb, k) * decay, 0.0)
    eye = jnp.eye(c, dtype=jnp.float32)
    rhs = jnp.concatenate([v * beta[..., None], kb * jnp.exp(gc)[..., None]], axis=-1)
    sol = lax.linalg.triangular_solve(m + eye, rhs, left_side=True, lower=True, unit_diagonal=True)
    u, w = sol[..., :dv], sol[..., dv:]
    a_intra = jnp.einsum('nhbid,nhbjd->nhbij', q, k) * decay

    def step(s, xs):
        q_i, k_i, u_i, w_i, gc_i, a_i = xs
        v_new = u_i - jnp.einsum('nhck,nhkv->nhcv', w_i, s)
        o_i = (jnp.einsum('nhck,nhkv->nhcv', q_i * jnp.exp(gc_i)[..., None], s)
               + jnp.einsum('nhij,nhjv->nhiv', a_i, v_new))
        g_last = gc_i[..., -1:]
        s = (s * jnp.exp(g_last)[..., None]
             + jnp.einsum('nhck,nhcv->nhkv', k_i * jnp.exp(g_last - gc_i)[..., None], v_new))
        return s, o_i

    xs = tuple(jnp.moveaxis(a, 2, 0) for a in (q, k, u, w, gc, a_intra))
    s_final, o = lax.scan(step, s0.astype(jnp.float32), xs)
    o = jnp.moveaxis(o, 0, 2).reshape(n, h, -1, dv)
    return jnp.moveaxis(o, 1, 2)[:, :t], s_final


def dn_shortconv_mixer(h, w_in, conv_w, a_log, dt_bias, norm_g, sc_w, w_out, s_delta, s_qkv, s_sc):
    n, t, _ = h.shape
    qk_w = DN_HEADS * DN_DK
    cuts = [int(v) for v in np.cumsum([QKV_WIDTH, DN_WIDTH, DN_HEADS, DN_HEADS, SC_WIDTH, SC_WIDTH, SC_WIDTH])]
    qkv_raw, z_a, a_in, b_in, b_gate, c_gate, x_b, z_b = jnp.split(h @ w_in, cuts, axis=-1)
    qkv, s_qkv_new = causal_conv(qkv_raw, conv_w, s_qkv)
    q, k, v = jnp.split(jax.nn.silu(qkv), [qk_w, 2 * qk_w], axis=-1)
    q = l2_norm(q.reshape(n, t, DN_HEADS, DN_DK)) * (DN_DK ** -0.5)
    k = l2_norm(k.reshape(n, t, DN_HEADS, DN_DK))
    v = v.reshape(n, t, DN_HEADS, DN_DV)
    g = -jnp.exp(a_log.astype(jnp.float32)) * jax.nn.softplus(a_in.astype(jnp.float32) + dt_bias.astype(jnp.float32))
    beta = jax.nn.sigmoid(b_in.astype(jnp.float32))
    o, s_delta_new = gated_delta_rule(q, k, v, g, beta, s_delta)
    y_a = rms_norm(o, norm_g).astype(h.dtype).reshape(n, t, DN_WIDTH) * jax.nn.silu(z_a)
    cv, s_sc_new = causal_conv(c_gate * x_b, sc_w, s_sc)
    y_b = b_gate * cv * jax.nn.silu(z_b)
    out = jnp.concatenate([y_a, y_b], axis=-1) @ w_out
    return out, s_delta_new, s_qkv_new, s_sc_new


def moba_attend(q, k_all, v_all, q_pos0):
    t, h, d = q.shape
    length = k_all.shape[0]
    nb = -(-length // MOBA_BLOCK)
    pad = nb * MOBA_BLOCK - length

    def to_blocks(a):
        a = jnp.pad(a, ((0, pad), (0, 0), (0, 0)))
        return a.reshape(nb, MOBA_BLOCK, h, d).transpose(2, 0, 1, 3)

    kb, vb = to_blocks(k_all), to_blocks(v_all)
    kmean = jnp.mean(kb.astype(jnp.float32), axis=2)
    n_sel = min(MOBA_TOPK, nb)
    qc = math.gcd(MOBA_Q_CHUNK, t)
    head_idx = jnp.arange(h)[:, None, None]
    offs = jnp.arange(MOBA_BLOCK, dtype=jnp.int32)
    scale = d ** -0.5

    def chunk(args):
        qi, pos = args
        own = pos // MOBA_BLOCK
        gate = jnp.einsum('qhd,hbd->hqb', qi.astype(jnp.float32), kmean)
        cand = jnp.arange(nb, dtype=jnp.int32)[None, :] < own[:, None]
        _, top = lax.top_k(jnp.where(cand[None], gate, -jnp.inf), n_sel)
        blocks = jnp.concatenate([top.astype(jnp.int32), jnp.broadcast_to(own[None, :, None], (h, qc, 1))], axis=-1)
        valid = jnp.concatenate([jnp.arange(n_sel, dtype=jnp.int32)[None, :] < own[:, None],
                                 jnp.ones((qc, 1), dtype=bool)], axis=-1)
        kg = kb[head_idx, blocks]
        vg = vb[head_idx, blocks]
        s = jnp.einsum('qhd,hqskd->hqsk', qi, kg).astype(jnp.float32) * scale
        kpos = blocks[..., None] * MOBA_BLOCK + offs
        mask = valid[None, :, :, None] & (kpos <= pos[None, :, None, None])
        p = jax.nn.softmax(jnp.where(mask, s, -jnp.inf).reshape(h, qc, -1), axis=-1)
        return jnp.einsum('hqn,hqnd->qhd', p, vg.reshape(h, qc, -1, d).astype(jnp.float32))

    qpos = q_pos0 + jnp.arange(t, dtype=jnp.int32)
    o = lax.map(chunk, (q.reshape(t // qc, qc, h, d), qpos.reshape(t // qc, qc)))
    return o.reshape(t, h, d)


def moba_mixer(h, w_in, qn_g, kn_g, w_out, cache_k=None, cache_v=None, page_table=None, layer_idx=0):
    n, t, _ = h.shape
    q, k, v, z = jnp.split(h @ w_in, 4, axis=-1)
    heads = (n, t, ATT_HEADS, ATT_HEAD_DIM)
    q = rms_norm(q.reshape(heads), qn_g)
    k = rms_norm(k.reshape(heads), kn_g)
    v = v.reshape(heads)
    if page_table is None:
        o = lax.map(lambda a: moba_attend(a[0], a[1], a[2], 0), (q, k, v))
    else:
        past = page_table.shape[1] * cache_k.shape[2]

        def one(a):
            qi, ki, vi, pages = a
            kp = cache_k[layer_idx, pages].reshape(past, ATT_HEADS, ATT_HEAD_DIM).astype(ki.dtype)
            vp = cache_v[layer_idx, pages].reshape(past, ATT_HEADS, ATT_HEAD_DIM).astype(vi.dtype)
            return moba_attend(qi, jnp.concatenate([kp, ki], axis=0), jnp.concatenate([vp, vi], axis=0), past)

        o = lax.map(one, (q, k, v, page_table))
    o = o.astype(h.dtype).reshape(n, t, ATT_WIDTH) * jax.nn.silu(z)
    return o @ w_out, k, v


def setup_inputs(seed: int = 0) -> dict:
    key = jax.random.key(seed)
    ks = jax.random.split(key, 24)
    f32 = jnp.float32

    def nrm(i, shape, scale):
        return jax.random.normal(ks[i], shape, f32) * scale

    n_pages = PAST_LEN // PAGE_SIZE
    n_used = DEC_BATCH * n_pages
    n_pool = n_used + max(1, n_used // 4)
    page_table = jax.random.permutation(ks[7], n_pool)[:n_used].reshape(DEC_BATCH, n_pages).astype(jnp.int32)
    dt = jnp.exp(jax.random.uniform(ks[16], (N_DN_LAYERS, DN_HEADS), f32, math.log(1e-3), math.log(1e-1)))
    return {
        'x_prompt': nrm(0, (BATCH, SEQ, D_MODEL), 1.0),
        'x_sample': nrm(1, (DEC_BATCH, DEC_SEQ, D_MODEL), 1.0),
        'state_delta': nrm(2, (N_DN_LAYERS, DEC_BATCH, DN_HEADS, DN_DK, DN_DV), 0.1),
        'state_qkv_conv': nrm(3, (N_DN_LAYERS, DEC_BATCH, DN_CONV - 1, QKV_WIDTH), 1.0),
        'state_short_conv': nrm(4, (N_DN_LAYERS, DEC_BATCH, SC_CONV - 1, SC_WIDTH), 1.0),
        'cache_k': nrm(5, (N_ATT_LAYERS, n_pool, PAGE_SIZE, ATT_HEADS, ATT_HEAD_DIM), 1.0),
        'cache_v': nrm(6, (N_ATT_LAYERS, n_pool, PAGE_SIZE, ATT_HEADS, ATT_HEAD_DIM), 1.0),
        'page_table': page_table,
        'c_prompt': nrm(8, (BATCH, D_MODEL), 1.0),
        'c_sample': nrm(9, (DEC_BATCH, D_MODEL), 1.0),
        'norm_g': 1.0 + nrm(10, (DEPTH, D_MODEL), 0.02),
        'ada_w': nrm(11, (DEPTH, D_MODEL, 3 * D_MODEL), 0.5 * D_MODEL ** -0.5),
        'ada_b': nrm(12, (DEPTH, 3 * D_MODEL), 0.01),
        'dn_w_in': nrm(13, (N_DN_LAYERS, D_MODEL, DN_IN), D_MODEL ** -0.5),
        'dn_conv_w': nrm(14, (N_DN_LAYERS, DN_CONV, QKV_WIDTH), DN_CONV ** -0.5),
        'dn_a_log': jnp.log(jax.random.uniform(ks[15], (N_DN_LAYERS, DN_HEADS), f32, 1.0, 16.0)),
        'dn_dt_bias': dt + jnp.log(-jnp.expm1(-dt)),
        'dn_norm_g': 1.0 + nrm(17, (N_DN_LAYERS, DN_DV), 0.02),
        'sc_conv_w': nrm(18, (N_DN_LAYERS, SC_CONV, SC_WIDTH), SC_CONV ** -0.5),
        'dn_w_out': nrm(19, (N_DN_LAYERS, DN_WIDTH + SC_WIDTH, D_MODEL), (DN_WIDTH + SC_WIDTH) ** -0.5),
        'att_w_in': nrm(20, (N_ATT_LAYERS, D_MODEL, ATT_IN), D_MODEL ** -0.5),
        'att_qn_g': 1.0 + nrm(21, (N_ATT_LAYERS, ATT_HEAD_DIM), 0.02),
        'att_kn_g': 1.0 + nrm(22, (N_ATT_LAYERS, ATT_HEAD_DIM), 0.02),
        'att_w_out': nrm(23, (N_ATT_LAYERS, ATT_WIDTH, D_MODEL), ATT_WIDTH ** -0.5),
    }


def reference(x_prompt, x_sample, state_delta, state_qkv_conv, state_short_conv, cache_k, cache_v,
              page_table, c_prompt, c_sample, norm_g, ada_w, ada_b, dn_w_in, dn_conv_w, dn_a_log,
              dn_dt_bias, dn_norm_g, sc_conv_w, dn_w_out, att_w_in, att_qn_g, att_kn_g, att_w_out):
    xp, xs = x_prompt, x_sample
    bp = x_prompt.shape[0]
    p_delta, p_qkv, p_sc, p_k, p_v = [], [], [], [], []
    s_delta, s_qkv, s_sc, s_k, s_v = [], [], [], [], []
    for layer in range(DEPTH):
        i = layer // 2
        hp, gp = modulate(xp, c_prompt, norm_g[layer], ada_w[layer], ada_b[layer])
        hs, gs = modulate(xs, c_sample, norm_g[layer], ada_w[layer], ada_b[layer])
        if layer % 2 == 0:
            args = (dn_w_in[i], dn_conv_w[i], dn_a_log[i], dn_dt_bias[i], dn_norm_g[i], sc_conv_w[i], dn_w_out[i])
            op, d_p, q_p, c_p = dn_shortconv_mixer(
                hp, *args,
                jnp.zeros((bp, DN_HEADS, DN_DK, DN_DV), jnp.float32),
                jnp.zeros((bp, DN_CONV - 1, QKV_WIDTH), xp.dtype),
                jnp.zeros((bp, SC_CONV - 1, SC_WIDTH), xp.dtype))
            os_, d_s, q_s, c_s = dn_shortconv_mixer(
                hs, *args, state_delta[i], state_qkv_conv[i], state_short_conv[i])
            p_delta.append(d_p); p_qkv.append(q_p); p_sc.append(c_p)
            s_delta.append(d_s); s_qkv.append(q_s); s_sc.append(c_s)
        else:
            args = (att_w_in[i], att_qn_g[i], att_kn_g[i], att_w_out[i])
            op, k_p, v_p = moba_mixer(hp, *args)
            os_, k_s, v_s = moba_mixer(hs, *args, cache_k=cache_k, cache_v=cache_v,
                                       page_table=page_table, layer_idx=i)
            p_k.append(k_p); p_v.append(v_p)
            s_k.append(k_s); s_v.append(v_s)
        xp = xp + gp * op
        xs = xs + gs * os_
    return (xp, xs,
            jnp.stack(p_delta), jnp.stack(p_qkv), jnp.stack(p_sc), jnp.stack(p_k), jnp.stack(p_v),
            jnp.stack(s_delta), jnp.stack(s_qkv), jnp.stack(s_sc), jnp.stack(s_k), jnp.stack(s_v))
```

```python
import functools

import jax
import jax.numpy as jnp
from jax import lax
from jax.experimental import pallas as pl
from jax.experimental.pallas import tpu as pltpu

F32 = jnp.float32
BF16 = jnp.bfloat16
HI = lax.Precision.HIGHEST

EPS = 1e-6
LANES = 128
SUBLANES = 8
HEAD_DIM = 128
DN_CHUNK_PROMPT = 64
DN_CHUNK_SAMPLE = 16
DN_SOLVE_BLOCK = 16
MOBA_BLOCK = 256
MOBA_TOPK = 3
NEG = -1e30
VMEM_LIMIT = 48 * 1024 * 1024


def _cparams(sem):
    return pltpu.CompilerParams(dimension_semantics=sem, vmem_limit_bytes=VMEM_LIMIT)


def _silu(x):
    return x * jax.nn.sigmoid(x)


def _softplus(x):
    return jnp.maximum(x, 0.0) + jnp.log1p(jnp.exp(-jnp.abs(x)))


def _rms(x, g):
    return x * lax.rsqrt(jnp.mean(x * x, axis=-1, keepdims=True) + EPS) * g


def _modulate(x, norm_g, mod):
    d = x.shape[-1]
    return _rms(x, norm_g) * (1.0 + mod[:, d:2 * d]) + mod[:, :d]


def _dot_nt(a, b, precision=None):
    return lax.dot_general(a, b, (((1,), (1,)), ((), ())), precision=precision,
                           preferred_element_type=F32)


def _dot_tn(a, b, precision=None):
    return lax.dot_general(a, b, (((0,), (0,)), ((), ())), precision=precision,
                           preferred_element_type=F32)


def _adaln_kernel(c_ref, w_ref, b_ref, o_ref):
    o_ref[0] = jnp.dot(c_ref[...], w_ref[0], preferred_element_type=F32) + b_ref[0]


def _adaln(c_all, ada_w, ada_b):
    n_layers, d, d3 = ada_w.shape
    n = c_all.shape[0]
    tn = d
    return pl.pallas_call(
        _adaln_kernel,
        grid=(n_layers, d3 // tn),
        in_specs=[pl.BlockSpec((n, d), lambda l, j: (0, 0)),
                  pl.BlockSpec((1, d, tn), lambda l, j: (l, 0, j)),
                  pl.BlockSpec((1, 1, tn), lambda l, j: (l, 0, j))],
        out_specs=pl.BlockSpec((1, n, tn), lambda l, j: (l, 0, j)),
        out_shape=jax.ShapeDtypeStruct((n_layers, n, d3), F32),
        compiler_params=_cparams(("parallel", "parallel")),
        name="adaln_mod",
    )(c_all, ada_w, ada_b.reshape(n_layers, 1, d3))


def _dn_activations(qkv, ab, alog, dtb, n_heads):
    w = n_heads * HEAD_DIM
    act = _silu(qkv)
    qs, ks = [], []
    for h in range(n_heads):
        qh = act[:, h * HEAD_DIM:(h + 1) * HEAD_DIM]
        kh = act[:, w + h * HEAD_DIM:w + (h + 1) * HEAD_DIM]
        qs.append(qh * lax.rsqrt(jnp.sum(qh * qh, axis=-1, keepdims=True) + EPS) * (HEAD_DIM ** -0.5))
        ks.append(kh * lax.rsqrt(jnp.sum(kh * kh, axis=-1, keepdims=True) + EPS))
    q = jnp.concatenate(qs, axis=-1)
    k = jnp.concatenate(ks, axis=-1)
    v = act[:, 2 * w:3 * w]
    lane = lax.broadcasted_iota(jnp.int32, ab.shape, 1)
    g = -jnp.exp(alog) * _softplus(ab + dtb)
    beta = jax.nn.sigmoid(ab)
    gb = jnp.where(lane < n_heads, g, jnp.where(lane < 2 * n_heads, beta, 0.0))
    return q, k, v, gb


def _shift_rows(cur, prev, s):
    ext = jnp.concatenate([prev, cur], axis=0)
    return ext[SUBLANES - s:SUBLANES - s + cur.shape[0]]


def _conv_rows(cur, prev, w):
    width = w.shape[0]
    y = cur * w[width - 1:width]
    for s in range(1, width):
        y = y + _shift_rows(cur, prev, s) * w[width - 1 - s:width - s]
    return y


def _dn_in_prompt_kernel(x_ref, mod_ref, ng_ref, w_ref, cw_ref, alog_ref, dtb_ref, scw_ref,
                         q_ref, k_ref, v_ref, gb_ref, sza_ref, yb_ref, qkvt_ref, sct_ref,
                         prev_qkv, prev_sc, *, n_heads, sc_w):
    i = pl.program_id(1)
    w = n_heads * HEAD_DIM
    o_z, o_ab, o_sc = 3 * w, 4 * w, 4 * w + LANES

    @pl.when(i == 0)
    def _():
        prev_qkv[...] = jnp.zeros_like(prev_qkv)
        prev_sc[...] = jnp.zeros_like(prev_sc)

    h = _modulate(x_ref[0], ng_ref[...], mod_ref[0]).astype(BF16)
    raw = jnp.dot(h, w_ref[:, 0:o_z], preferred_element_type=F32)
    qkv = _conv_rows(raw, prev_qkv[...], cw_ref[...])
    ab = jnp.dot(h, w_ref[:, o_ab:o_sc], preferred_element_type=F32)
    q, k, v, gb = _dn_activations(qkv, ab, alog_ref[...], dtb_ref[...], n_heads)
    q_ref[0], k_ref[0], v_ref[0], gb_ref[0] = q, k, v, gb
    prev_qkv[...] = raw[-SUBLANES:]
    qkvt_ref[0] = raw[-SUBLANES:]

    sza_ref[0] = _silu(jnp.dot(h, w_ref[:, o_z:o_ab], preferred_element_type=F32))

    sc = jnp.dot(h, w_ref[:, o_sc:o_sc + 4 * sc_w], preferred_element_type=F32)
    cx = sc[:, sc_w:2 * sc_w] * sc[:, 2 * sc_w:3 * sc_w]
    cv = _conv_rows(cx, prev_sc[...], scw_ref[...])
    yb_ref[0] = sc[:, 0:sc_w] * cv * _silu(sc[:, 3 * sc_w:4 * sc_w])
    prev_sc[...] = cx[-SUBLANES:]
    sct_ref[0] = cx[-SUBLANES:]


def _dn_in_prompt(x, mod, norm_g, w0, conv_w, alog, dtb, sc_conv_w, n_heads, sc_w, tm=256):
    n, t, d = x.shape
    w = n_heads * HEAD_DIM
    wtot = w0.shape[1]
    row = lambda c: pl.BlockSpec((1, tm, c), lambda b, i: (b, i, 0))
    full = lambda a: pl.BlockSpec(a.shape, lambda b, i: (0,) * a.ndim)
    tail = lambda c: pl.BlockSpec((1, SUBLANES, c), lambda b, i: (b, 0, 0))
    outs = [jax.ShapeDtypeStruct((n, t, w), F32)] * 3 + [
        jax.ShapeDtypeStruct((n, t, LANES), F32), jax.ShapeDtypeStruct((n, t, w), F32),
        jax.ShapeDtypeStruct((n, t, sc_w), F32),
        jax.ShapeDtypeStruct((n, SUBLANES, 3 * w), F32), jax.ShapeDtypeStruct((n, SUBLANES, sc_w), F32)]
    return pl.pallas_call(
        functools.partial(_dn_in_prompt_kernel, n_heads=n_heads, sc_w=sc_w),
        grid=(n, t // tm),
        in_specs=[row(d), pl.BlockSpec((1, 1, 3 * d), lambda b, i: (b, 0, 0)), full(norm_g),
                  pl.BlockSpec((d, wtot), lambda b, i: (0, 0)), full(conv_w), full(alog), full(dtb),
                  full(sc_conv_w)],
        out_specs=[row(w), row(w), row(w), row(LANES), row(w), row(sc_w), tail(3 * w), tail(sc_w)],
        out_shape=outs,
        scratch_shapes=[pltpu.VMEM((SUBLANES, 3 * w), F32), pltpu.VMEM((SUBLANES, sc_w), F32)],
        compiler_params=_cparams(("parallel", "arbitrary")),
        name="dn_in_prompt",
    )(x, mod, norm_g, w0, conv_w, alog, dtb, sc_conv_w)


def _dn_in_sample_kernel(x_ref, mod_ref, ng_ref, w_ref, sq_ref, ssc_ref, cw_ref, alog_ref, dtb_ref,
                         scw_ref, q_ref, k_ref, v_ref, gb_ref, sza_ref, yb_ref, nq_ref, nsc_ref,
                         *, n_heads, sc_w):
    t_len = x_ref.shape[0]
    w = n_heads * HEAD_DIM
    o_z, o_ab, o_sc = 3 * w, 4 * w, 4 * w + LANES
    cw, scw = cw_ref[...], scw_ref[...]
    nq, nsc = cw.shape[0] - 1, scw.shape[0] - 1
    xq = [sq_ref[j] for j in range(nq)]
    xsc = [ssc_ref[j] for j in range(nsc)]
    hs, sc_all = [], []
    for t in range(t_len):
        h = _modulate(x_ref[t], ng_ref[...], mod_ref[...]).astype(BF16)
        hs.append(h)
        xq.append(jnp.dot(h, w_ref[:, 0:o_z], preferred_element_type=F32))
        sc = jnp.dot(h, w_ref[:, o_sc:o_sc + 4 * sc_w], preferred_element_type=F32)
        sc_all.append(sc)
        xsc.append(sc[:, sc_w:2 * sc_w] * sc[:, 2 * sc_w:3 * sc_w])
    for t in range(t_len):
        qkv = sum(xq[t + j] * cw[j:j + 1] for j in range(nq + 1))
        ab = jnp.dot(hs[t], w_ref[:, o_ab:o_sc], preferred_element_type=F32)
        q, k, v, gb = _dn_activations(qkv, ab, alog_ref[...], dtb_ref[...], n_heads)
        q_ref[t], k_ref[t], v_ref[t], gb_ref[t] = q, k, v, gb
        sza_ref[t] = _silu(jnp.dot(hs[t], w_ref[:, o_z:o_ab], preferred_element_type=F32))
        cv = sum(xsc[t + j] * scw[j:j + 1] for j in range(nsc + 1))
        sc = sc_all[t]
        yb_ref[t] = sc[:, 0:sc_w] * cv * _silu(sc[:, 3 * sc_w:4 * sc_w])
    for j in range(nq):
        nq_ref[j] = xq[t_len + j]
    for j in range(nsc):
        nsc_ref[j] = xsc[t_len + j]


def _dn_in_sample(x_tm, mod, norm_g, w0, s_qkv_tm, s_sc_tm, conv_w, alog, dtb, sc_conv_w, n_heads, sc_w):
    t, n, d = x_tm.shape
    w = n_heads * HEAD_DIM
    gs = min(n, 128)
    wtot = w0.shape[1]
    tm3 = lambda r, c: pl.BlockSpec((r, gs, c), lambda g: (0, g, 0))
    full = lambda a: pl.BlockSpec(a.shape, lambda g: (0,) * a.ndim)
    nq, nsc = conv_w.shape[0] - 1, sc_conv_w.shape[0] - 1
    outs = [jax.ShapeDtypeStruct((t, n, w), F32)] * 3 + [
        jax.ShapeDtypeStruct((t, n, LANES), F32), jax.ShapeDtypeStruct((t, n, w), F32),
        jax.ShapeDtypeStruct((t, n, sc_w), F32),
        jax.ShapeDtypeStruct((nq, n, 3 * w), F32), jax.ShapeDtypeStruct((nsc, n, sc_w), F32)]
    return pl.pallas_call(
        functools.partial(_dn_in_sample_kernel, n_heads=n_heads, sc_w=sc_w),
        grid=(n // gs,),
        in_specs=[tm3(t, d), pl.BlockSpec((gs, 3 * d), lambda g: (g, 0)), full(norm_g),
                  pl.BlockSpec((d, wtot), lambda g: (0, 0)), tm3(nq, 3 * w), tm3(nsc, sc_w),
                  full(conv_w), full(alog), full(dtb), full(sc_conv_w)],
        out_specs=[tm3(t, w), tm3(t, w), tm3(t, w), tm3(t, LANES), tm3(t, w), tm3(t, sc_w),
                   tm3(nq, 3 * w), tm3(nsc, sc_w)],
        out_shape=outs,
        compiler_params=_cparams(("parallel",)),
        name="dn_in_sample",
    )(x_tm, mod, norm_g, w0, s_qkv_tm, s_sc_tm, conv_w, alog, dtb, sc_conv_w)


def _bmm(a, b):
    return jnp.einsum("bij,bjk->bik", a, b, precision=HI, preferred_element_type=F32)


def _bmm_nt(a, b):
    return jnp.einsum("bik,bjk->bij", a, b, precision=HI, preferred_element_type=F32)


def _neumann_inverse(m, order, eye):
    p = eye - m
    mk, k = m, 1
    while 2 * k < order:
        mk = _bmm(mk, mk)
        p = p + _bmm(p, mk)
        k *= 2
    return p


def _unit_lower_inverse(m, c):
    row = lax.broadcasted_iota(jnp.int32, (c, c), 0)
    col = lax.broadcasted_iota(jnp.int32, (c, c), 1)
    eye = (row == col).astype(F32)
    if c <= DN_SOLVE_BLOCK:
        return _neumann_inverse(m, c, eye)
    on_diag = (row // DN_SOLVE_BLOCK) == (col // DN_SOLVE_BLOCK)
    m_diag = jnp.where(on_diag, m, 0.0)
    d_inv = _neumann_inverse(m_diag, DN_SOLVE_BLOCK, eye)
    n_off = _bmm(d_inv, m - m_diag)
    return _bmm(_neumann_inverse(n_off, c // DN_SOLVE_BLOCK, eye), d_inv)


def _delta_prep_kernel(q_ref, k_ref, v_ref, gb_ref, u_ref, w_ref, qg_ref, kd_ref, el_ref, a_ref, *, c):
    g_dim, r_dim, wq = q_ref.shape
    n_heads = wq // HEAD_DIM
    rows = g_dim * r_dim
    nb = rows // c
    row = lax.broadcasted_iota(jnp.int32, (c, c), 0)
    col = lax.broadcasted_iota(jnp.int32, (c, c), 1)
    gb2 = gb_ref[...].reshape(rows, LANES)
    gb3 = gb2.reshape(nb, c, LANES)
    tril = jnp.broadcast_to((row >= col).astype(F32), (nb, c, c))
    gc3 = _bmm(tril, gb3)
    gc2 = gc3.reshape(rows, LANES)
    pick_row = lax.broadcasted_iota(jnp.int32, (LANES, LANES), 0)
    pick_lane = lax.broadcasted_iota(jnp.int32, (c, LANES), 1)
    for h in range(n_heads):
        sl = slice(h * HEAD_DIM, (h + 1) * HEAD_DIM)
        gch = jnp.dot(gc2, (pick_row == h).astype(F32), precision=HI,
                      preferred_element_type=F32).reshape(nb, c, LANES)
        beta = jnp.dot(gb2, (pick_row == n_heads + h).astype(F32), precision=HI,
                       preferred_element_type=F32).reshape(nb, c, LANES)
        gc_last = jnp.broadcast_to(gch[:, c - 1:c, :], (nb, c, LANES))
        gc_cols = _bmm_nt(jnp.broadcast_to((pick_lane == h).astype(F32), (nb, c, LANES)), gc3)
        decay = jnp.where(row >= col, jnp.exp(gch[:, :, :c] - gc_cols), 0.0)
        qh = q_ref[:, :, sl].reshape(nb, c, HEAD_DIM)
        kh = k_ref[:, :, sl].reshape(nb, c, HEAD_DIM)
        vh = v_ref[:, :, sl].reshape(nb, c, HEAD_DIM)
        kb = kh * beta
        m = jnp.where(row > col, _bmm_nt(kb, kh) * decay, 0.0)
        a = _bmm_nt(qh, kh) * decay
        t_inv = _unit_lower_inverse(m, c)
        egc = jnp.exp(gch)
        sol = _bmm(t_inv, jnp.concatenate([vh * beta, kb * egc], axis=-1))
        u_ref[:, :, sl] = sol[:, :, :HEAD_DIM].reshape(g_dim, r_dim, HEAD_DIM)
        w_ref[:, :, sl] = sol[:, :, HEAD_DIM:].reshape(g_dim, r_dim, HEAD_DIM)
        qg_ref[:, :, sl] = (qh * egc).reshape(g_dim, r_dim, HEAD_DIM)
        kd_ref[:, :, sl] = (kh * jnp.exp(gc_last - gch)).reshape(g_dim, r_dim, HEAD_DIM)
        el_ref[:, :, sl] = jnp.exp(gc_last).reshape(g_dim, r_dim, HEAD_DIM)
        a_ref[:, :, h * c:(h + 1) * c] = a.reshape(g_dim, r_dim, c)


def _delta_prep(q, k, v, gb, c, g_dim, r_dim):
    n, t, wq = q.shape
    n_heads = wq // HEAD_DIM
    blk = lambda cc: pl.BlockSpec((g_dim, r_dim, cc), lambda b, i: (b, i, 0))
    outs = [jax.ShapeDtypeStruct((n, t, wq), F32)] * 5 + [jax.ShapeDtypeStruct((n, t, n_heads * c), F32)]
    return pl.pallas_call(
        functools.partial(_delta_prep_kernel, c=c),
        grid=(n // g_dim, t // r_dim),
        in_specs=[blk(wq), blk(wq), blk(wq), blk(LANES)],
        out_specs=[blk(wq)] * 5 + [blk(n_heads * c)],
        out_shape=outs,
        compiler_params=_cparams(("parallel", "parallel")),
        name="delta_prep",
    )(q, k, v, gb)


def _delta_scan_kernel(u_ref, w_ref, qg_ref, kd_ref, el_ref, a_ref, s0_ref, o_ref, s_ref, *, c):
    @pl.when(pl.program_id(1) == 0)
    def _():
        s_ref[...] = s0_ref[...]

    g_dim, _, wq = u_ref.shape
    for g in range(g_dim):
        for h in range(wq // HEAD_DIM):
            sl = slice(h * HEAD_DIM, (h + 1) * HEAD_DIM)
            s = s_ref[g, h]
            v_new = u_ref[g, :, sl] - jnp.dot(w_ref[g, :, sl], s, precision=HI, preferred_element_type=F32)
            o_ref[g, :, sl] = (jnp.dot(qg_ref[g, :, sl], s, precision=HI, preferred_element_type=F32)
                               + jnp.dot(a_ref[g, :, h * c:(h + 1) * c], v_new, precision=HI,
                                         preferred_element_type=F32))
            s_ref[g, h] = s * el_ref[g, 0:1, sl] + _dot_tn(kd_ref[g, :, sl], v_new, precision=HI)


def _delta_scan(u, w, qg, kd, el, a, s0, c, g_dim):
    n, t, wq = u.shape
    n_heads = wq // HEAD_DIM
    blk = lambda cc: pl.BlockSpec((g_dim, c, cc), lambda b, i: (b, i, 0))
    st = pl.BlockSpec((g_dim, n_heads, HEAD_DIM, HEAD_DIM), lambda b, i: (b, 0, 0, 0))
    return pl.pallas_call(
        functools.partial(_delta_scan_kernel, c=c),
        grid=(n // g_dim, t // c),
        in_specs=[blk(wq)] * 5 + [blk(n_heads * c), st],
        out_specs=[blk(wq), st],
        out_shape=[jax.ShapeDtypeStruct((n, t, wq), F32),
                   jax.ShapeDtypeStruct((n, n_heads, HEAD_DIM, HEAD_DIM), F32)],
        compiler_params=_cparams(("parallel", "arbitrary")),
        name="delta_scan",
    )(u, w, qg, kd, el, a, s0)


def _gated_delta_rule(q, k, v, gb, s0, c, prep_block, scan_seqs):
    u, w, qg, kd, el, a = _delta_prep(q, k, v, gb, c, *prep_block)
    return _delta_scan(u, w, qg, kd, el, a, s0, c, scan_seqs)


def _dn_out_kernel(o_ref, sza_ref, yb_ref, x_ref, gate_ref, ng_ref, w_ref, y_ref):
    o = o_ref[0]
    wa = o.shape[-1]
    ya = jnp.concatenate([_rms(o[:, h * HEAD_DIM:(h + 1) * HEAD_DIM], ng_ref[...])
                          for h in range(wa // HEAD_DIM)], axis=-1) * sza_ref[0]
    out = (jnp.dot(ya.astype(BF16), w_ref[0:wa, :], preferred_element_type=F32)
           + jnp.dot(yb_ref[0].astype(BF16), w_ref[wa:, :], preferred_element_type=F32))
    y_ref[0] = x_ref[0] + gate_ref[0] * out


def _dn_out(o, sza, yb, x, gate, dn_norm_g, w_out, tm):
    n, t, d = x.shape
    wa, wb = o.shape[-1], yb.shape[-1]
    row = lambda c: pl.BlockSpec((1, tm, c), lambda b, i: (b, i, 0))
    gate_rows = gate.shape[1]
    gspec = (pl.BlockSpec((1, 1, d), lambda b, i: (b, 0, 0)) if gate_rows == 1
             else pl.BlockSpec((1, tm, d), lambda b, i: (b, i, 0)))
    return pl.pallas_call(
        _dn_out_kernel,
        grid=(n, t // tm),
        in_specs=[row(wa), row(wa), row(wb), row(d), gspec,
                  pl.BlockSpec(dn_norm_g.shape, lambda b, i: (0, 0)),
                  pl.BlockSpec(w_out.shape, lambda b, i: (0, 0))],
        out_specs=row(d),
        out_shape=jax.ShapeDtypeStruct((n, t, d), F32),
        compiler_params=_cparams(("parallel", "parallel")),
        name="dn_out",
    )(o, sza, yb, x, gate, dn_norm_g, w_out)


def _attn_in_kernel(x_ref, mod_ref, ng_ref, w_ref, qn_ref, kn_ref, q_ref, k_ref, v_ref, kb_ref, vb_ref,
                    sz_ref, *maybe_kmean_ref):
    d = x_ref.shape[-1]
    h = _modulate(x_ref[0], ng_ref[...], mod_ref[0]).astype(BF16)
    heads = range(d // HEAD_DIM)
    qr = jnp.dot(h, w_ref[:, 0:d], preferred_element_type=F32)
    q_ref[0] = jnp.concatenate([_rms(qr[:, i * HEAD_DIM:(i + 1) * HEAD_DIM], qn_ref[...]) for i in heads], axis=-1)
    kr = jnp.dot(h, w_ref[:, d:2 * d], preferred_element_type=F32)
    k = jnp.concatenate([_rms(kr[:, i * HEAD_DIM:(i + 1) * HEAD_DIM], kn_ref[...]) for i in heads], axis=-1)
    k_ref[0] = k
    kb_ref[0] = k.astype(BF16)
    v = jnp.dot(h, w_ref[:, 2 * d:3 * d], preferred_element_type=F32)
    v_ref[0] = v
    vb_ref[0] = v.astype(BF16)
    sz_ref[0] = _silu(jnp.dot(h, w_ref[:, 3 * d:4 * d], preferred_element_type=F32))
    if maybe_kmean_ref:
        maybe_kmean_ref[0][0, 0] = jnp.mean(k, axis=0, keepdims=True)


def _attn_in(x, mod, norm_g, w_in, qn_g, kn_g, tm, with_kmean):
    n, t, d = x.shape
    row = lambda: pl.BlockSpec((1, tm, d), lambda b, i: (b, i, 0))
    mod_rows = mod.shape[1]
    mspec = (pl.BlockSpec((1, 1, 3 * d), lambda b, i: (b, 0, 0)) if mod_rows == 1
             else pl.BlockSpec((1, tm, 3 * d), lambda b, i: (b, i, 0)))
    full = lambda a: pl.BlockSpec(a.shape, lambda b, i: (0,) * a.ndim)
    out_specs = [row()] * 6
    outs = [jax.ShapeDtypeStruct((n, t, d), F32)] * 3 + [jax.ShapeDtypeStruct((n, t, d), BF16)] * 2 + [
        jax.ShapeDtypeStruct((n, t, d), F32)]
    if with_kmean:
        out_specs.append(pl.BlockSpec((1, 1, 1, d), lambda b, i: (b, i, 0, 0)))
        outs.append(jax.ShapeDtypeStruct((n, t // tm, 1, d), F32))
    return pl.pallas_call(
        _attn_in_kernel,
        grid=(n, t // tm),
        in_specs=[row(), mspec, full(norm_g), full(w_in), full(qn_g), full(kn_g)],
        out_specs=out_specs,
        out_shape=outs,
        compiler_params=_cparams(("parallel", "parallel")),
        name="attn_in",
    )(x, mod, norm_g, w_in, qn_g, kn_g)


def _topk_select(gate, cand, n_blocks):
    blk = lax.broadcasted_iota(jnp.int32, gate.shape, 1)
    gm = jnp.where(cand, gate, -jnp.inf)
    rank = jnp.zeros(gate.shape, jnp.int32)
    for b in range(n_blocks):
        other = gm[:, b:b + 1]
        rank = rank + ((other > gm) | ((other == gm) & (b < blk))).astype(jnp.int32)
    return cand & (rank < MOBA_TOPK)


def _moba_prompt_kernel(q_ref, k_ref, v_ref, kmean_ref, o_ref, m_scr, l_scr, acc_scr, sel_scr):
    i = pl.program_id(2)
    blk = q_ref.shape[1]
    n_blocks = kmean_ref.shape[1]
    scale = HEAD_DIM ** -0.5
    qf = q_ref[0]
    qb = qf.astype(BF16)

    gate = _dot_nt(qf, kmean_ref[0], precision=HI)
    cand = lax.broadcasted_iota(jnp.int32, gate.shape, 1) < i
    sel = _topk_select(gate, cand, n_blocks).astype(F32)
    for b in range(n_blocks):
        sel_scr[b] = jnp.broadcast_to(sel[:, b:b + 1], (blk, LANES))

    start = pl.multiple_of(i * blk, blk)
    s = _dot_nt(qb, k_ref[0, pl.ds(start, blk), :]) * scale
    causal = lax.broadcasted_iota(jnp.int32, s.shape, 0) >= lax.broadcasted_iota(jnp.int32, s.shape, 1)
    m0 = jnp.max(jnp.where(causal, s, NEG), axis=1, keepdims=True)
    p = jnp.where(causal, jnp.exp(s - m0), 0.0)
    m_scr[...] = jnp.broadcast_to(m0, m_scr.shape)
    l_scr[...] = jnp.broadcast_to(jnp.sum(p, axis=1, keepdims=True), l_scr.shape)
    acc_scr[...] = jnp.dot(p.astype(BF16), v_ref[0, pl.ds(start, blk), :], preferred_element_type=F32)

    def past_block(j, carry):
        st = pl.multiple_of(j * blk, blk)
        sj = _dot_nt(qb, k_ref[0, pl.ds(st, blk), :]) * scale
        keep = jnp.concatenate([sel_scr[j]] * (blk // LANES), axis=1) > 0.5
        m_prev = m_scr[...]
        m_new = jnp.maximum(m_prev, jnp.max(jnp.where(keep, sj, NEG), axis=1, keepdims=True))
        alpha = jnp.exp(m_prev - m_new)
        pj = jnp.where(keep, jnp.exp(sj - m_new[:, 0:1]), 0.0)
        l_scr[...] = alpha * l_scr[...] + jnp.sum(pj, axis=1, keepdims=True)
        acc_scr[...] = alpha * acc_scr[...] + jnp.dot(pj.astype(BF16), v_ref[0, pl.ds(st, blk), :],
                                                      preferred_element_type=F32)
        m_scr[...] = m_new
        return carry

    lax.fori_loop(0, i, past_block, 0)
    o_ref[0] = acc_scr[...] / l_scr[...]


def _moba_prompt(q, kb, vb, kmean):
    n, t, d = q.shape
    n_heads = d // HEAD_DIM
    n_blocks = t // MOBA_BLOCK
    qspec = pl.BlockSpec((1, MOBA_BLOCK, HEAD_DIM), lambda b, h, i: (b, i, h))
    kvspec = pl.BlockSpec((1, t, HEAD_DIM), lambda b, h, i: (b, 0, h))
    return pl.pallas_call(
        _moba_prompt_kernel,
        grid=(n, n_heads, n_blocks),
        in_specs=[qspec, kvspec, kvspec, pl.BlockSpec((1, n_blocks, HEAD_DIM), lambda b, h, i: (b, 0, h))],
        out_specs=qspec,
        out_shape=jax.ShapeDtypeStruct((n, t, d), F32),
        scratch_shapes=[pltpu.VMEM((MOBA_BLOCK, LANES), F32)] * 3
                       + [pltpu.VMEM((n_blocks, MOBA_BLOCK, LANES), F32)],
        compiler_params=_cparams(("parallel", "parallel", "arbitrary")),
        name="moba_prompt",
    )(q, kb, vb, kmean)


def _moba_decode_kernel(pt_ref, q_ref, kn_ref, vn_ref, *refs, pages_per_block):
    del pt_ref
    k_refs = refs[:pages_per_block]
    v_refs = refs[pages_per_block:2 * pages_per_block]
    o_ref, kmean_scr, m_scr, l_scr, acc_scr = refs[2 * pages_per_block:]
    b = pl.program_id(1)
    n_blocks = pl.num_programs(1)
    t_len, d = q_ref.shape[1], q_ref.shape[2]
    n_heads = d // HEAD_DIM
    rows = t_len * n_heads
    scale = HEAD_DIM ** -0.5

    ridx = lax.broadcasted_iota(jnp.int32, (rows, d), 0)
    lidx = lax.broadcasted_iota(jnp.int32, (rows, d), 1)
    head_mask = (ridx % n_heads) == (lidx // HEAD_DIM)
    q4 = q_ref[0]
    qbd = jnp.where(head_mask, jnp.broadcast_to(q4[:, None, :], (t_len, n_heads, d)).reshape(rows, d), 0.0)

    pages_k = [r[0, 0] for r in k_refs]
    s = jnp.concatenate([_dot_nt(qbd, kp) for kp in pages_k], axis=1) * scale
    m_b = jnp.max(s, axis=1, keepdims=True)
    p = jnp.exp(s - m_b)
    page = pages_k[0].shape[0]
    acc = sum(jnp.dot(p[:, j * page:(j + 1) * page], v_refs[j][0, 0], preferred_element_type=F32)
              for j in range(pages_per_block))
    ksum = sum(jnp.sum(kp, axis=0, keepdims=True) for kp in pages_k)
    kmean_scr[pl.ds(b, 1), :] = ksum / (page * pages_per_block)
    m_scr[b] = jnp.broadcast_to(m_b, (rows, LANES))
    l_scr[b] = jnp.broadcast_to(jnp.sum(p, axis=1, keepdims=True), (rows, LANES))
    acc_scr[b] = acc

    @pl.when(b == n_blocks - 1)
    def _():
        nb = kmean_scr.shape[0]
        gate = _dot_nt(qbd, kmean_scr[...], precision=HI)
        sel = _topk_select(gate, jnp.full(gate.shape, True), nb)
        tok = lax.broadcasted_iota(jnp.int32, (rows, 1), 0) // n_heads
        kn, vn = kn_ref[0], vn_ref[0]
        s_own = [jnp.sum(qbd * kn[j:j + 1], axis=1, keepdims=True) * scale for j in range(t_len)]
        m_all = functools.reduce(jnp.maximum, [jnp.where(tok >= j, s_own[j], NEG) for j in range(t_len)])
        for j in range(nb):
            m_all = jnp.maximum(m_all, jnp.where(sel[:, j:j + 1], m_scr[j][:, 0:1], NEG))
        den = jnp.zeros((rows, 1), F32)
        num = jnp.zeros((rows, d), F32)
        for j in range(t_len):
            pj = jnp.where(tok >= j, jnp.exp(s_own[j] - m_all), 0.0)
            den = den + pj
            num = num + pj * vn[j:j + 1]
        for j in range(nb):
            wj = jnp.where(sel[:, j:j + 1], jnp.exp(m_scr[j][:, 0:1] - m_all), 0.0)
            den = den + wj * l_scr[j][:, 0:1]
            num = num + wj * acc_scr[j]
        out = jnp.where(head_mask, num / den, 0.0)
        o_ref[0] = jnp.sum(out.reshape(t_len, n_heads, d), axis=1)


def _moba_decode(q, k_new, v_new, cache_k, cache_v, page_table, layer):
    n, t_len, d = q.shape
    n_pages = page_table.shape[1]
    page = cache_k.shape[2]
    assert MOBA_BLOCK % page == 0 and (n_pages * page) % MOBA_BLOCK == 0 and t_len <= MOBA_BLOCK
    ppb = MOBA_BLOCK // page
    nb = n_pages // ppb
    ck = cache_k.reshape(cache_k.shape[0], cache_k.shape[1], page, d)
    cv = cache_v.reshape(cache_v.shape[0], cache_v.shape[1], page, d)
    tok = pl.BlockSpec((1, t_len, d), lambda s, b, pt: (s, 0, 0))

    def page_spec(j):
        return pl.BlockSpec((1, 1, page, d), lambda s, b, pt: (layer, pt[s, b * ppb + j], 0, 0))

    rows = t_len * (d // HEAD_DIM)
    grid_spec = pltpu.PrefetchScalarGridSpec(
        num_scalar_prefetch=1,
        grid=(n, nb),
        in_specs=[tok, tok, tok] + [page_spec(j) for j in range(ppb)] * 2,
        out_specs=tok,
        scratch_shapes=[pltpu.VMEM((nb, d), F32), pltpu.VMEM((nb, rows, LANES), F32),
                        pltpu.VMEM((nb, rows, LANES), F32), pltpu.VMEM((nb, rows, d), F32)])
    return pl.pallas_call(
        functools.partial(_moba_decode_kernel, pages_per_block=ppb),
        grid_spec=grid_spec,
        out_shape=jax.ShapeDtypeStruct((n, t_len, d), F32),
        compiler_params=_cparams(("parallel", "arbitrary")),
        name="moba_decode",
    )(page_table, q, k_new, v_new, *([ck] * ppb), *([cv] * ppb))


def _attn_out_kernel(o_ref, sz_ref, x_ref, gate_ref, w_ref, y_ref):
    out = jnp.dot((o_ref[0] * sz_ref[0]).astype(BF16), w_ref[...], preferred_element_type=F32)
    y_ref[0] = x_ref[0] + gate_ref[0] * out


def _attn_out(o, sz, x, gate, w_out, tm):
    n, t, d = x.shape
    row = lambda: pl.BlockSpec((1, tm, d), lambda b, i: (b, i, 0))
    gspec = (pl.BlockSpec((1, 1, d), lambda b, i: (b, 0, 0)) if gate.shape[1] == 1 else row())
    return pl.pallas_call(
        _attn_out_kernel,
        grid=(n, t // tm),
        in_specs=[row(), row(), row(), gspec, pl.BlockSpec(w_out.shape, lambda b, i: (0, 0))],
        out_specs=row(),
        out_shape=jax.ShapeDtypeStruct((n, t, d), F32),
        compiler_params=_cparams(("parallel", "parallel")),
        name="attn_out",
    )(o, sz, x, gate, w_out)


def _pad_lanes(a):
    return jnp.pad(a, ((0, 0), (0, LANES - a.shape[-1])))


def kernel(x_prompt, x_sample, state_delta, state_qkv_conv, state_short_conv, cache_k, cache_v, page_table, c_prompt, c_sample, norm_g, ada_w, ada_b, dn_w_in, dn_conv_w, dn_a_log, dn_dt_bias, dn_norm_g, sc_conv_w, dn_w_out, att_w_in, att_qn_g, att_kn_g, att_w_out):
    bp, seq, d = x_prompt.shape
    bs, t_s, _ = x_sample.shape
    n_heads = dn_a_log.shape[-1]
    w = n_heads * HEAD_DIM
    sc_w = sc_conv_w.shape[-1]
    assert dn_w_in.shape[-1] == 4 * w + 2 * n_heads + 4 * sc_w and state_delta.shape[-2:] == (HEAD_DIM, HEAD_DIM)

    mod = _adaln(jnp.concatenate([c_prompt, c_sample], axis=0), ada_w, ada_b)
    mod_p, mod_s = mod[:, :bp], mod[:, bp:]

    wi = dn_w_in[0]
    w0 = jnp.concatenate([wi[:, :4 * w], _pad_lanes(wi[:, 4 * w:4 * w + 2 * n_heads]),
                          wi[:, 4 * w + 2 * n_heads:]], axis=1).astype(BF16)
    alog, dtb = _pad_lanes(dn_a_log[0][None]), _pad_lanes(dn_dt_bias[0][None])
    ng0 = norm_g[0][None]
    dng = dn_norm_g[0][None]
    w_out0 = dn_w_out[0].astype(BF16)
    nq, nsc = dn_conv_w.shape[1] - 1, sc_conv_w.shape[1] - 1

    q, k, v, gb, sza, yb, qkv_tail, sc_tail = _dn_in_prompt(
        x_prompt, mod_p[0][:, None], ng0, w0, dn_conv_w[0], alog, dtb, sc_conv_w[0], n_heads, sc_w)
    o_p, p_delta = _gated_delta_rule(q, k, v, gb, jnp.zeros((bp, n_heads, HEAD_DIM, HEAD_DIM), F32),
                                     DN_CHUNK_PROMPT, (1, 256), bp)
    xp1 = _dn_out(o_p, sza, yb, x_prompt, mod_p[0][:, None, 2 * d:], dng, w_out0, 256)
    p_qkv, p_sc = qkv_tail[:, SUBLANES - nq:], sc_tail[:, SUBLANES - nsc:]

    tm_of = lambda a: jnp.swapaxes(a, 0, 1)
    qs, ks, vs, gbs, szas, ybs, s_qkv_tm, s_sc_tm = _dn_in_sample(
        tm_of(x_sample), mod_s[0], ng0, w0, tm_of(state_qkv_conv[0]), tm_of(state_short_conv[0]),
        dn_conv_w[0], alog, dtb, sc_conv_w[0], n_heads, sc_w)
    cs = DN_CHUNK_SAMPLE
    t_pad = -(-t_s // cs) * cs
    pad_t = lambda a: jnp.pad(tm_of(a), ((0, 0), (0, t_pad - t_s), (0, 0)))
    o_s, s_delta = _gated_delta_rule(pad_t(qs), pad_t(ks), pad_t(vs), pad_t(gbs), state_delta[0],
                                     cs, (8, t_pad), 8)
    rows_s = bs * t_s
    flat = lambda a: a.reshape(1, rows_s, a.shape[-1])
    gate_rows = lambda m: flat(jnp.repeat(m, t_s, axis=0))
    xs1 = _dn_out(flat(o_s[:, :t_s]), flat(tm_of(szas)), flat(tm_of(ybs)), flat(x_sample),
                  gate_rows(mod_s[0][:, 2 * d:]), dng, w_out0, rows_s)
    s_qkv, s_sc = tm_of(s_qkv_tm), tm_of(s_sc_tm)

    wa = att_w_in[0].astype(BF16)
    ng1, qn, kn = norm_g[1][None], att_qn_g[0][None], att_kn_g[0][None]
    w_out1 = att_w_out[0].astype(BF16)
    n_att_heads = d // HEAD_DIM

    q1, k1, v1, kb1, vb1, sz1, kmean = _attn_in(xp1, mod_p[1][:, None], ng1, wa, qn, kn, MOBA_BLOCK, True)
    o1 = _moba_prompt(q1, kb1, vb1, kmean.reshape(bp, seq // MOBA_BLOCK, d))
    y_prompt = _attn_out(o1, sz1, xp1, mod_p[1][:, None, 2 * d:], w_out1, 256)

    q1s, k1s, v1s, _, _, sz1s = _attn_in(xs1, gate_rows(mod_s[1]), ng1, wa, qn, kn, rows_s, False)
    seqs = lambda a: a.reshape(bs, t_s, d)
    o1s = _moba_decode(seqs(q1s), seqs(k1s), seqs(v1s), cache_k, cache_v, page_table, 0)
    y_sample = _attn_out(flat(o1s), sz1s, xs1, gate_rows(mod_s[1][:, 2 * d:]), w_out1, rows_s)

    heads = lambda a, n: a.reshape(1, n, -1, n_att_heads, HEAD_DIM)
    return (y_prompt, y_sample.reshape(bs, t_s, d),
            p_delta[None], p_qkv[None], p_sc[None], heads(k1, bp), heads(v1, bp),
            s_delta[None], s_qkv[None], s_sc[None], heads(k1s, bs), heads(v1s, bs))
```

```python
import functools

import jax
import jax.numpy as jnp
from jax import lax
from jax.experimental import pallas as pl
from jax.experimental.pallas import tpu as pltpu

F32 = jnp.float32
BF16 = jnp.bfloat16
HI = lax.Precision.HIGHEST

EPS = 1e-6
LANES = 128
SUBLANES = 8
HEAD_DIM = 128
DN_CHUNK_PROMPT = 64
DN_CHUNK_SAMPLE = 16
DN_SOLVE_BLOCK = 16
MOBA_BLOCK = 256
MOBA_TOPK = 3
DECODE_BLOCKS_PER_STEP = 4
MOBA_HEADS_PER_STEP = 4
DN_PREP_ROWS = 512
NEG = -1e30
DN_GRAM_PASSES = 1
DN_SOLVE_PASSES = 3
DN_SCAN_PASSES = 1
VMEM_LIMIT = 48 * 1024 * 1024


def _cparams(sem):
    return pltpu.CompilerParams(dimension_semantics=sem, vmem_limit_bytes=VMEM_LIMIT)


def _silu(x):
    return x * jax.nn.sigmoid(x)


def _softplus(x):
    return jnp.maximum(x, 0.0) + jnp.log1p(jnp.exp(-jnp.abs(x)))


def _rms(x, g):
    return x * lax.rsqrt(jnp.mean(x * x, axis=-1, keepdims=True) + EPS) * g


def _modulate(x, norm_g, mod):
    d = x.shape[-1]
    return _rms(x, norm_g) * (1.0 + mod[:, d:2 * d]) + mod[:, :d]


def _dot_nt(a, b, precision=None):
    return lax.dot_general(a, b, (((1,), (1,)), ((), ())), precision=precision,
                           preferred_element_type=F32)


def _dot_tn(a, b, precision=None):
    return lax.dot_general(a, b, (((0,), (0,)), ((), ())), precision=precision,
                           preferred_element_type=F32)


def _adaln_kernel(c_ref, w_ref, b_ref, o_ref):
    o_ref[0] = jnp.dot(c_ref[...], w_ref[0], preferred_element_type=F32) + b_ref[0]


def _adaln(c_all, ada_w, ada_b):
    n_layers, d, d3 = ada_w.shape
    n = c_all.shape[0]
    tn = d
    return pl.pallas_call(
        _adaln_kernel,
        grid=(n_layers, d3 // tn),
        in_specs=[pl.BlockSpec((n, d), lambda l, j: (0, 0)),
                  pl.BlockSpec((1, d, tn), lambda l, j: (l, 0, j)),
                  pl.BlockSpec((1, 1, tn), lambda l, j: (l, 0, j))],
        out_specs=pl.BlockSpec((1, n, tn), lambda l, j: (l, 0, j)),
        out_shape=jax.ShapeDtypeStruct((n_layers, n, d3), F32),
        compiler_params=_cparams(("parallel", "parallel")),
        name="adaln_mod",
    )(c_all, ada_w, ada_b.reshape(n_layers, 1, d3))


def _dn_activations(qkv, ab, alog, dtb, n_heads):
    w = n_heads * HEAD_DIM
    act = _silu(qkv)
    qs, ks = [], []
    for h in range(n_heads):
        qh = act[:, h * HEAD_DIM:(h + 1) * HEAD_DIM]
        kh = act[:, w + h * HEAD_DIM:w + (h + 1) * HEAD_DIM]
        qs.append(qh * lax.rsqrt(jnp.sum(qh * qh, axis=-1, keepdims=True) + EPS) * (HEAD_DIM ** -0.5))
        ks.append(kh * lax.rsqrt(jnp.sum(kh * kh, axis=-1, keepdims=True) + EPS))
    q = jnp.concatenate(qs, axis=-1)
    k = jnp.concatenate(ks, axis=-1)
    v = act[:, 2 * w:3 * w]
    lane = lax.broadcasted_iota(jnp.int32, ab.shape, 1)
    g = -jnp.exp(alog) * _softplus(ab + dtb)
    beta = jax.nn.sigmoid(ab)
    gb = jnp.where(lane < n_heads, g, jnp.where(lane < 2 * n_heads, beta, 0.0))
    return q, k, v, gb


def _shift_rows(cur, prev, s):
    ext = jnp.concatenate([prev, cur], axis=0)
    return ext[SUBLANES - s:SUBLANES - s + cur.shape[0]]


def _conv_rows(cur, prev, w):
    width = w.shape[0]
    y = cur * w[width - 1:width]
    for s in range(1, width):
        y = y + _shift_rows(cur, prev, s) * w[width - 1 - s:width - s]
    return y


def _dn_in_prompt_kernel(x_ref, mod_ref, ng_ref, w_ref, cw_ref, alog_ref, dtb_ref, scw_ref,
                         q_ref, k_ref, v_ref, gb_ref, sza_ref, yb_ref, qkvt_ref, sct_ref,
                         prev_qkv, prev_sc, *, n_heads, sc_w):
    i = pl.program_id(1)
    w = n_heads * HEAD_DIM
    o_z, o_ab, o_sc = 3 * w, 4 * w, 4 * w + LANES

    @pl.when(i == 0)
    def _():
        prev_qkv[...] = jnp.zeros_like(prev_qkv)
        prev_sc[...] = jnp.zeros_like(prev_sc)

    h = _modulate(x_ref[0], ng_ref[...], mod_ref[0]).astype(BF16)
    raw = jnp.dot(h, w_ref[:, 0:o_z], preferred_element_type=F32)
    qkv = _conv_rows(raw, prev_qkv[...], cw_ref[...])
    ab = jnp.dot(h, w_ref[:, o_ab:o_sc], preferred_element_type=F32)
    q, k, v, gb = _dn_activations(qkv, ab, alog_ref[...], dtb_ref[...], n_heads)
    q_ref[0], k_ref[0], v_ref[0], gb_ref[0] = q, k, v, gb
    prev_qkv[...] = raw[-SUBLANES:]
    qkvt_ref[0] = raw[-SUBLANES:]

    sza_ref[0] = _silu(jnp.dot(h, w_ref[:, o_z:o_ab], preferred_element_type=F32))

    sc = jnp.dot(h, w_ref[:, o_sc:o_sc + 4 * sc_w], preferred_element_type=F32)
    cx = sc[:, sc_w:2 * sc_w] * sc[:, 2 * sc_w:3 * sc_w]
    cv = _conv_rows(cx, prev_sc[...], scw_ref[...])
    yb_ref[0] = sc[:, 0:sc_w] * cv * _silu(sc[:, 3 * sc_w:4 * sc_w])
    prev_sc[...] = cx[-SUBLANES:]
    sct_ref[0] = cx[-SUBLANES:]


def _dn_in_prompt(x, mod, norm_g, w0, conv_w, alog, dtb, sc_conv_w, n_heads, sc_w, tm=256):
    n, t, d = x.shape
    w = n_heads * HEAD_DIM
    wtot = w0.shape[1]
    row = lambda c: pl.BlockSpec((1, tm, c), lambda b, i: (b, i, 0))
    full = lambda a: pl.BlockSpec(a.shape, lambda b, i: (0,) * a.ndim)
    tail = lambda c: pl.BlockSpec((1, SUBLANES, c), lambda b, i: (b, 0, 0))
    outs = [jax.ShapeDtypeStruct((n, t, w), F32)] * 3 + [
        jax.ShapeDtypeStruct((n, t, LANES), F32), jax.ShapeDtypeStruct((n, t, w), F32),
        jax.ShapeDtypeStruct((n, t, sc_w), F32),
        jax.ShapeDtypeStruct((n, SUBLANES, 3 * w), F32), jax.ShapeDtypeStruct((n, SUBLANES, sc_w), F32)]
    return pl.pallas_call(
        functools.partial(_dn_in_prompt_kernel, n_heads=n_heads, sc_w=sc_w),
        grid=(n, t // tm),
        in_specs=[row(d), pl.BlockSpec((1, 1, 3 * d), lambda b, i: (b, 0, 0)), full(norm_g),
                  pl.BlockSpec((d, wtot), lambda b, i: (0, 0)), full(conv_w), full(alog), full(dtb),
                  full(sc_conv_w)],
        out_specs=[row(w), row(w), row(w), row(LANES), row(w), row(sc_w), tail(3 * w), tail(sc_w)],
        out_shape=outs,
        scratch_shapes=[pltpu.VMEM((SUBLANES, 3 * w), F32), pltpu.VMEM((SUBLANES, sc_w), F32)],
        compiler_params=_cparams(("parallel", "arbitrary")),
        name="dn_in_prompt",
    )(x, mod, norm_g, w0, conv_w, alog, dtb, sc_conv_w)


def _dn_in_sample_kernel(x_ref, mod_ref, ng_ref, w_ref, sq_ref, ssc_ref, cw_ref, alog_ref, dtb_ref,
                         scw_ref, q_ref, k_ref, v_ref, gb_ref, sza_ref, yb_ref, nq_ref, nsc_ref,
                         *, n_heads, sc_w):
    t_len = x_ref.shape[0]
    w = n_heads * HEAD_DIM
    o_z, o_ab, o_sc = 3 * w, 4 * w, 4 * w + LANES
    cw, scw = cw_ref[...], scw_ref[...]
    nq, nsc = cw.shape[0] - 1, scw.shape[0] - 1
    xq = [sq_ref[j] for j in range(nq)]
    xsc = [ssc_ref[j] for j in range(nsc)]
    hs, sc_all = [], []
    for t in range(t_len):
        h = _modulate(x_ref[t], ng_ref[...], mod_ref[...]).astype(BF16)
        hs.append(h)
        xq.append(jnp.dot(h, w_ref[:, 0:o_z], preferred_element_type=F32))
        sc = jnp.dot(h, w_ref[:, o_sc:o_sc + 4 * sc_w], preferred_element_type=F32)
        sc_all.append(sc)
        xsc.append(sc[:, sc_w:2 * sc_w] * sc[:, 2 * sc_w:3 * sc_w])
    for t in range(t_len):
        qkv = sum(xq[t + j] * cw[j:j + 1] for j in range(nq + 1))
        ab = jnp.dot(hs[t], w_ref[:, o_ab:o_sc], preferred_element_type=F32)
        q, k, v, gb = _dn_activations(qkv, ab, alog_ref[...], dtb_ref[...], n_heads)
        q_ref[t], k_ref[t], v_ref[t], gb_ref[t] = q, k, v, gb
        sza_ref[t] = _silu(jnp.dot(hs[t], w_ref[:, o_z:o_ab], preferred_element_type=F32))
        cv = sum(xsc[t + j] * scw[j:j + 1] for j in range(nsc + 1))
        sc = sc_all[t]
        yb_ref[t] = sc[:, 0:sc_w] * cv * _silu(sc[:, 3 * sc_w:4 * sc_w])
    for j in range(nq):
        nq_ref[j] = xq[t_len + j]
    for j in range(nsc):
        nsc_ref[j] = xsc[t_len + j]


def _dn_in_sample(x_tm, mod, norm_g, w0, s_qkv_tm, s_sc_tm, conv_w, alog, dtb, sc_conv_w, n_heads, sc_w):
    t, n, d = x_tm.shape
    w = n_heads * HEAD_DIM
    gs = min(n, 128)
    wtot = w0.shape[1]
    tm3 = lambda r, c: pl.BlockSpec((r, gs, c), lambda g: (0, g, 0))
    full = lambda a: pl.BlockSpec(a.shape, lambda g: (0,) * a.ndim)
    nq, nsc = conv_w.shape[0] - 1, sc_conv_w.shape[0] - 1
    outs = [jax.ShapeDtypeStruct((t, n, w), F32)] * 3 + [
        jax.ShapeDtypeStruct((t, n, LANES), F32), jax.ShapeDtypeStruct((t, n, w), F32),
        jax.ShapeDtypeStruct((t, n, sc_w), F32),
        jax.ShapeDtypeStruct((nq, n, 3 * w), F32), jax.ShapeDtypeStruct((nsc, n, sc_w), F32)]
    return pl.pallas_call(
        functools.partial(_dn_in_sample_kernel, n_heads=n_heads, sc_w=sc_w),
        grid=(n // gs,),
        in_specs=[tm3(t, d), pl.BlockSpec((gs, 3 * d), lambda g: (g, 0)), full(norm_g),
                  pl.BlockSpec((d, wtot), lambda g: (0, 0)), tm3(nq, 3 * w), tm3(nsc, sc_w),
                  full(conv_w), full(alog), full(dtb), full(sc_conv_w)],
        out_specs=[tm3(t, w), tm3(t, w), tm3(t, w), tm3(t, LANES), tm3(t, w), tm3(t, sc_w),
                   tm3(nq, 3 * w), tm3(nsc, sc_w)],
        out_shape=outs,
        compiler_params=_cparams(("parallel",)),
        name="dn_in_sample",
    )(x_tm, mod, norm_g, w0, s_qkv_tm, s_sc_tm, conv_w, alog, dtb, sc_conv_w)


def _bf16_parts(x, n):
    parts, r = [], x
    for i in range(n):
        p = r.astype(BF16)
        parts.append(p)
        if i + 1 < n:
            r = r - p.astype(F32)
    return parts


def _mm(dot, a, b, passes):
    if passes == 1:
        return dot(a.astype(BF16), b.astype(BF16))
    a_hi, a_lo = _bf16_parts(a, 2)
    b_hi, b_lo = _bf16_parts(b, 2)
    return dot(a_hi, b_hi) + (dot(a_hi, b_lo) + dot(a_lo, b_hi))


def _mm_exact01(dot, sel, x):
    hi, mid, lo = (p.astype(sel.dtype) for p in _bf16_parts(x, 3))
    return dot(sel, hi) + (dot(sel, mid) + dot(sel, lo))


def _bdot(a, b):
    return jnp.einsum("bij,bjk->bik", a, b, preferred_element_type=F32)


def _bdot_nt(a, b):
    return jnp.einsum("bik,bjk->bij", a, b, preferred_element_type=F32)


def _dot(a, b):
    return jnp.dot(a, b, preferred_element_type=F32)


def _neumann_inverse(m, order, eye):
    p = eye - m
    mk, k = m, 1
    while 2 * k < order:
        mk = _mm(_bdot, mk, mk, DN_SOLVE_PASSES)
        p = p + _mm(_bdot, p, mk, DN_SOLVE_PASSES)
        k *= 2
    return p


def _unit_lower_inverse(m, c):
    row = lax.broadcasted_iota(jnp.int32, (c, c), 0)
    col = lax.broadcasted_iota(jnp.int32, (c, c), 1)
    eye = (row == col).astype(F32)
    if c <= DN_SOLVE_BLOCK:
        return _neumann_inverse(m, c, eye)
    on_diag = (row // DN_SOLVE_BLOCK) == (col // DN_SOLVE_BLOCK)
    m_diag = jnp.where(on_diag, m, 0.0)
    d_inv = _neumann_inverse(m_diag, DN_SOLVE_BLOCK, eye)
    n_off = _mm(_bdot, d_inv, m - m_diag, DN_SOLVE_PASSES)
    return _mm(_bdot, _neumann_inverse(n_off, c // DN_SOLVE_BLOCK, eye), d_inv, DN_SOLVE_PASSES)


def _delta_prep_kernel(q_ref, k_ref, v_ref, gb_ref, u_ref, w_ref, qg_ref, kd_ref, el_ref, a_ref, *, c):
    g_dim, r_dim, wq = q_ref.shape
    n_heads = wq // HEAD_DIM
    rows = g_dim * r_dim
    nb = rows // c
    row = lax.broadcasted_iota(jnp.int32, (c, c), 0)
    col = lax.broadcasted_iota(jnp.int32, (c, c), 1)
    gb2 = gb_ref[...].reshape(rows, LANES)
    tril = jnp.broadcast_to((row >= col).astype(BF16), (nb, c, c))
    gc3 = _mm_exact01(_bdot, tril, gb2.reshape(nb, c, LANES))
    gc2 = gc3.reshape(rows, LANES)
    pick_lane = lax.broadcasted_iota(jnp.int32, (c, LANES), 1)
    for h in range(n_heads):
        sl = slice(h * HEAD_DIM, (h + 1) * HEAD_DIM)
        gch = jnp.broadcast_to(gc2[:, h:h + 1], (rows, LANES)).reshape(nb, c, LANES)
        beta = jnp.broadcast_to(gb2[:, n_heads + h:n_heads + h + 1], (rows, LANES)).reshape(nb, c, LANES)
        gc_last = jnp.broadcast_to(gch[:, c - 1:c, :], (nb, c, LANES))
        pick = jnp.broadcast_to((pick_lane == h).astype(F32), (nb, c, LANES))
        gc_cols = _mm_exact01(_bdot_nt, pick, gc3)
        decay = jnp.where(row >= col, jnp.exp(gch[:, :, :c] - gc_cols), 0.0)
        qh = q_ref[:, :, sl].reshape(nb, c, HEAD_DIM)
        kh = k_ref[:, :, sl].reshape(nb, c, HEAD_DIM)
        vh = v_ref[:, :, sl].reshape(nb, c, HEAD_DIM)
        kb = kh * beta
        m = jnp.where(row > col, _mm(_bdot_nt, kb, kh, DN_GRAM_PASSES) * decay, 0.0)
        a = _mm(_bdot_nt, qh, kh, DN_GRAM_PASSES) * decay
        t_inv = _unit_lower_inverse(m, c)
        egc = jnp.exp(gch)
        sol = _mm(_bdot, t_inv, jnp.concatenate([vh * beta, kb * egc], axis=-1), DN_SOLVE_PASSES)
        u_ref[:, :, sl] = sol[:, :, :HEAD_DIM].reshape(g_dim, r_dim, HEAD_DIM)
        w_ref[:, :, sl] = sol[:, :, HEAD_DIM:].reshape(g_dim, r_dim, HEAD_DIM)
        qg_ref[:, :, sl] = (qh * egc).reshape(g_dim, r_dim, HEAD_DIM)
        kd_ref[:, :, sl] = (kh * jnp.exp(gc_last - gch)).reshape(g_dim, r_dim, HEAD_DIM)
        el_ref[:, :, sl] = jnp.exp(gc_last).reshape(g_dim, r_dim, HEAD_DIM)
        a_ref[:, :, h * c:(h + 1) * c] = a.reshape(g_dim, r_dim, c)


def _delta_prep(q, k, v, gb, c, g_dim, r_dim):
    n, t, wq = q.shape
    n_heads = wq // HEAD_DIM
    blk = lambda cc: pl.BlockSpec((g_dim, r_dim, cc), lambda b, i: (b, i, 0))
    outs = [jax.ShapeDtypeStruct((n, t, wq), F32)] * 5 + [jax.ShapeDtypeStruct((n, t, n_heads * c), F32)]
    return pl.pallas_call(
        functools.partial(_delta_prep_kernel, c=c),
        grid=(n // g_dim, t // r_dim),
        in_specs=[blk(wq), blk(wq), blk(wq), blk(LANES)],
        out_specs=[blk(wq)] * 5 + [blk(n_heads * c)],
        out_shape=outs,
        compiler_params=_cparams(("parallel", "parallel")),
        name="delta_prep",
    )(q, k, v, gb)


def _delta_scan_kernel(u_ref, w_ref, qg_ref, kd_ref, el_ref, a_ref, s0_ref, o_ref, s_ref, *, c):
    @pl.when(pl.program_id(1) == 0)
    def _():
        s_ref[...] = s0_ref[...]

    g_dim, _, wq = u_ref.shape
    for g in range(g_dim):
        for h in range(wq // HEAD_DIM):
            sl = slice(h * HEAD_DIM, (h + 1) * HEAD_DIM)
            s = s_ref[g, h]
            v_new = u_ref[g, :, sl] - _mm(_dot, w_ref[g, :, sl], s, DN_SCAN_PASSES)
            o_ref[g, :, sl] = (_mm(_dot, qg_ref[g, :, sl], s, DN_SCAN_PASSES)
                               + _mm(_dot, a_ref[g, :, h * c:(h + 1) * c], v_new, DN_SCAN_PASSES))
            s_ref[g, h] = s * el_ref[g, 0:1, sl] + _mm(_dot_tn, kd_ref[g, :, sl], v_new, DN_SCAN_PASSES)


def _delta_scan(u, w, qg, kd, el, a, s0, c, g_dim):
    n, t, wq = u.shape
    n_heads = wq // HEAD_DIM
    blk = lambda cc: pl.BlockSpec((g_dim, c, cc), lambda b, i: (b, i, 0))
    st = pl.BlockSpec((g_dim, n_heads, HEAD_DIM, HEAD_DIM), lambda b, i: (b, 0, 0, 0))
    return pl.pallas_call(
        functools.partial(_delta_scan_kernel, c=c),
        grid=(n // g_dim, t // c),
        in_specs=[blk(wq)] * 5 + [blk(n_heads * c), st],
        out_specs=[blk(wq), st],
        out_shape=[jax.ShapeDtypeStruct((n, t, wq), F32),
                   jax.ShapeDtypeStruct((n, n_heads, HEAD_DIM, HEAD_DIM), F32)],
        compiler_params=_cparams(("parallel", "arbitrary")),
        name="delta_scan",
    )(u, w, qg, kd, el, a, s0)


def _gated_delta_rule(q, k, v, gb, s0, c, prep_block, scan_seqs):
    u, w, qg, kd, el, a = _delta_prep(q, k, v, gb, c, *prep_block)
    return _delta_scan(u, w, qg, kd, el, a, s0, c, scan_seqs)


def _dn_out_kernel(o_ref, sza_ref, yb_ref, x_ref, gate_ref, ng_ref, w_ref, y_ref):
    o = o_ref[0]
    wa = o.shape[-1]
    ya = jnp.concatenate([_rms(o[:, h * HEAD_DIM:(h + 1) * HEAD_DIM], ng_ref[...])
                          for h in range(wa // HEAD_DIM)], axis=-1) * sza_ref[0]
    out = (jnp.dot(ya.astype(BF16), w_ref[0:wa, :], preferred_element_type=F32)
           + jnp.dot(yb_ref[0].astype(BF16), w_ref[wa:, :], preferred_element_type=F32))
    y_ref[0] = x_ref[0] + gate_ref[0] * out


def _dn_out(o, sza, yb, x, gate, dn_norm_g, w_out, tm):
    n, t, d = x.shape
    wa, wb = o.shape[-1], yb.shape[-1]
    row = lambda c: pl.BlockSpec((1, tm, c), lambda b, i: (b, i, 0))
    gate_rows = gate.shape[1]
    gspec = (pl.BlockSpec((1, 1, d), lambda b, i: (b, 0, 0)) if gate_rows == 1
             else pl.BlockSpec((1, tm, d), lambda b, i: (b, i, 0)))
    return pl.pallas_call(
        _dn_out_kernel,
        grid=(n, t // tm),
        in_specs=[row(wa), row(wa), row(wb), row(d), gspec,
                  pl.BlockSpec(dn_norm_g.shape, lambda b, i: (0, 0)),
                  pl.BlockSpec(w_out.shape, lambda b, i: (0, 0))],
        out_specs=row(d),
        out_shape=jax.ShapeDtypeStruct((n, t, d), F32),
        compiler_params=_cparams(("parallel", "parallel")),
        name="dn_out",
    )(o, sza, yb, x, gate, dn_norm_g, w_out)


def _attn_in_kernel(x_ref, mod_ref, ng_ref, w_ref, qn_ref, kn_ref, q_ref, k_ref, v_ref, sz_ref, *prompt_refs):
    d = x_ref.shape[-1]
    h = _modulate(x_ref[0], ng_ref[...], mod_ref[0]).astype(BF16)
    heads = range(d // HEAD_DIM)
    qr = jnp.dot(h, w_ref[:, 0:d], preferred_element_type=F32)
    q_ref[0] = jnp.concatenate([_rms(qr[:, i * HEAD_DIM:(i + 1) * HEAD_DIM], qn_ref[...]) for i in heads], axis=-1)
    kr = jnp.dot(h, w_ref[:, d:2 * d], preferred_element_type=F32)
    k = jnp.concatenate([_rms(kr[:, i * HEAD_DIM:(i + 1) * HEAD_DIM], kn_ref[...]) for i in heads], axis=-1)
    k_ref[0] = k
    v = jnp.dot(h, w_ref[:, 2 * d:3 * d], preferred_element_type=F32)
    v_ref[0] = v
    sz_ref[0] = _silu(jnp.dot(h, w_ref[:, 3 * d:4 * d], preferred_element_type=F32))
    if prompt_refs:
        kb_ref, vt_ref, kmean_ref = prompt_refs
        kb_ref[0] = k.astype(BF16)
        vt_ref[0] = v.T.astype(BF16)
        kmean_ref[0, 0] = jnp.mean(k, axis=0, keepdims=True)


def _attn_in(x, mod, norm_g, w_in, qn_g, kn_g, tm, for_prompt):
    n, t, d = x.shape
    row = lambda: pl.BlockSpec((1, tm, d), lambda b, i: (b, i, 0))
    mod_rows = mod.shape[1]
    mspec = (pl.BlockSpec((1, 1, 3 * d), lambda b, i: (b, 0, 0)) if mod_rows == 1
             else pl.BlockSpec((1, tm, 3 * d), lambda b, i: (b, i, 0)))
    full = lambda a: pl.BlockSpec(a.shape, lambda b, i: (0,) * a.ndim)
    out_specs = [row()] * 4
    outs = [jax.ShapeDtypeStruct((n, t, d), F32)] * 4
    if for_prompt:
        assert tm == MOBA_BLOCK
        out_specs += [row(), pl.BlockSpec((1, d, tm), lambda b, i: (b, 0, i)),
                      pl.BlockSpec((1, 1, 1, d), lambda b, i: (b, i, 0, 0))]
        outs += [jax.ShapeDtypeStruct((n, t, d), BF16), jax.ShapeDtypeStruct((n, d, t), BF16),
                 jax.ShapeDtypeStruct((n, t // tm, 1, d), F32)]
    return pl.pallas_call(
        _attn_in_kernel,
        grid=(n, t // tm),
        in_specs=[row(), mspec, full(norm_g), full(w_in), full(qn_g), full(kn_g)],
        out_specs=out_specs,
        out_shape=outs,
        compiler_params=_cparams(("parallel", "parallel")),
        name="attn_in",
    )(x, mod, norm_g, w_in, qn_g, kn_g)


def _topk_select(gate, cand, n_blocks, axis):
    blk = lax.broadcasted_iota(jnp.int32, gate.shape, axis)
    gm = jnp.where(cand, gate, -jnp.inf)
    rank = jnp.zeros(gate.shape, jnp.int32)
    for b in range(n_blocks):
        other = gm[b:b + 1, :] if axis == 0 else gm[:, b:b + 1]
        rank = rank + ((other > gm) | ((other == gm) & (b < blk))).astype(jnp.int32)
    return cand & (rank < MOBA_TOPK)


def _moba_prompt_kernel(q_ref, k_ref, vt_ref, kmean_ref, o_ref, acc_scr, sel_scr):
    i = pl.program_id(2)
    blk = q_ref.shape[1]
    n_blocks = kmean_ref.shape[1]
    heads = range(q_ref.shape[2] // HEAD_DIM)
    hs = lambda h: slice(h * HEAD_DIM, (h + 1) * HEAD_DIM)
    start = pl.multiple_of(i * blk, blk)
    causal = (lax.broadcasted_iota(jnp.int32, (blk, blk), 0) <= lax.broadcasted_iota(jnp.int32, (blk, blk), 1))

    qbs, m0, l0 = [], [], []
    for h in heads:
        qf = q_ref[0, :, hs(h)]
        qb = (qf * (HEAD_DIM ** -0.5)).astype(BF16)
        qbs.append(qb)
        gate = _dot_nt(kmean_ref[0, :, hs(h)], qf, precision=HI)
        cand = lax.broadcasted_iota(jnp.int32, gate.shape, 0) < i
        sel_scr[h] = _topk_select(gate, cand, n_blocks, 0).astype(F32)
        s = jnp.where(causal, _dot_nt(k_ref[0, pl.ds(start, blk), hs(h)], qb), NEG)
        m = jnp.max(s, axis=0, keepdims=True)
        p = jnp.exp(s - m)
        m0.append(m)
        l0.append(jnp.sum(p, axis=0, keepdims=True))
        acc_scr[h] = jnp.dot(vt_ref[0, hs(h), pl.ds(start, blk)], p.astype(BF16), preferred_element_type=F32)

    def past_block(j, carry):
        ms, ls = carry
        st = pl.multiple_of(j * blk, blk)
        new_m, new_l = [], []
        for h in heads:
            sj = _dot_nt(k_ref[0, pl.ds(st, blk), hs(h)], qbs[h])
            sj = jnp.where(sel_scr[h, pl.ds(j, 1), :] > 0.5, sj, NEG)
            m_new = jnp.maximum(ms[h], jnp.max(sj, axis=0, keepdims=True))
            alpha = jnp.exp(ms[h] - m_new)
            pj = jnp.exp(sj - m_new)
            acc_scr[h] = alpha * acc_scr[h] + jnp.dot(vt_ref[0, hs(h), pl.ds(st, blk)], pj.astype(BF16),
                                                      preferred_element_type=F32)
            new_m.append(m_new)
            new_l.append(alpha * ls[h] + jnp.sum(pj, axis=0, keepdims=True))
        return tuple(new_m), tuple(new_l)

    _, l_fin = lax.fori_loop(0, i, past_block, (tuple(m0), tuple(l0)))
    for h in heads:
        o_ref[0, :, hs(h)] = (acc_scr[h] / l_fin[h]).T


def _moba_prompt(q, kb, vt, kmean, heads_per_step):
    n, t, d = q.shape
    n_blocks = t // MOBA_BLOCK
    wg = heads_per_step * HEAD_DIM
    qspec = pl.BlockSpec((1, MOBA_BLOCK, wg), lambda b, h, i: (b, i, h))
    return pl.pallas_call(
        _moba_prompt_kernel,
        grid=(n, d // wg, n_blocks),
        in_specs=[qspec, pl.BlockSpec((1, t, wg), lambda b, h, i: (b, 0, h)),
                  pl.BlockSpec((1, wg, t), lambda b, h, i: (b, h, 0)),
                  pl.BlockSpec((1, n_blocks, wg), lambda b, h, i: (b, 0, h))],
        out_specs=qspec,
        out_shape=jax.ShapeDtypeStruct((n, t, d), F32),
        scratch_shapes=[pltpu.VMEM((heads_per_step, HEAD_DIM, MOBA_BLOCK), F32),
                        pltpu.VMEM((heads_per_step, n_blocks, MOBA_BLOCK), F32)],
        compiler_params=_cparams(("parallel", "parallel", "arbitrary")),
        name="moba_prompt",
    )(q, kb, vt, kmean)


def _moba_decode_kernel(pt_ref, q_ref, kn_ref, vn_ref, *refs, pages_per_block, blocks_per_step):
    del pt_ref
    n_pages = pages_per_block * blocks_per_step
    k_refs, v_refs = refs[:n_pages], refs[n_pages:2 * n_pages]
    o_ref, kmean_scr, m_scr, l_scr, acc_scr = refs[2 * n_pages:]
    step = pl.program_id(1)
    t_len, n_heads = q_ref.shape[1], q_ref.shape[2]
    rows = t_len * n_heads
    page = k_refs[0].shape[2]
    cols = page * n_heads
    qall = q_ref[0].reshape(rows, HEAD_DIM)
    qs = qall * (HEAD_DIM ** -0.5)
    same_head = (lax.broadcasted_iota(jnp.int32, (rows, cols), 1) % n_heads
                 == lax.broadcasted_iota(jnp.int32, (rows, cols), 0) % n_heads)

    for bb in range(blocks_per_step):
        pk = [k_refs[bb * pages_per_block + j][0, 0] for j in range(pages_per_block)]
        pv = [v_refs[bb * pages_per_block + j][0, 0] for j in range(pages_per_block)]
        s = [jnp.where(same_head, _dot_nt(qs, kp.reshape(cols, HEAD_DIM)), NEG) for kp in pk]
        m_b = functools.reduce(jnp.maximum, [jnp.max(x, axis=1, keepdims=True) for x in s])
        p = [jnp.exp(x - m_b) for x in s]
        l_b = sum(jnp.sum(x, axis=1, keepdims=True) for x in p)
        acc = sum(jnp.dot(x, vp.reshape(cols, HEAD_DIM), preferred_element_type=F32) for x, vp in zip(p, pv))
        idx = step * blocks_per_step + bb
        kmean_scr[idx] = sum(jnp.sum(kp, axis=0) for kp in pk) / (page * pages_per_block)
        m_scr[idx] = jnp.broadcast_to(m_b, (rows, LANES))
        l_scr[idx] = jnp.broadcast_to(l_b, (rows, LANES))
        acc_scr[idx] = acc

    @pl.when(step == pl.num_programs(1) - 1)
    def _():
        nb = kmean_scr.shape[0]
        per_row = lambda a: jnp.broadcast_to(a[None], (t_len, n_heads, HEAD_DIM)).reshape(rows, HEAD_DIM)
        gate = jnp.concatenate([jnp.sum(qall * per_row(kmean_scr[j]), axis=1, keepdims=True)
                                for j in range(nb)], axis=1)
        sel = _topk_select(gate, jnp.full(gate.shape, True), nb, 1)
        tok = lax.broadcasted_iota(jnp.int32, (rows, 1), 0) // n_heads
        s_own = [jnp.sum(qs * per_row(kn_ref[0, j]), axis=1, keepdims=True) for j in range(t_len)]
        m_all = functools.reduce(jnp.maximum, [jnp.where(tok >= j, s_own[j], NEG) for j in range(t_len)])
        for j in range(nb):
            m_all = jnp.maximum(m_all, jnp.where(sel[:, j:j + 1], m_scr[j][:, 0:1], NEG))
        den = jnp.zeros((rows, 1), F32)
        num = jnp.zeros((rows, HEAD_DIM), F32)
        for j in range(t_len):
            pj = jnp.where(tok >= j, jnp.exp(s_own[j] - m_all), 0.0)
            den = den + pj
            num = num + pj * per_row(vn_ref[0, j])
        for j in range(nb):
            wj = jnp.where(sel[:, j:j + 1], jnp.exp(m_scr[j][:, 0:1] - m_all), 0.0)
            den = den + wj * l_scr[j][:, 0:1]
            num = num + wj * acc_scr[j]
        o_ref[0] = (num / den).reshape(t_len, n_heads, HEAD_DIM)


def _moba_decode(q, k_new, v_new, cache_k, cache_v, page_table, layer, blocks_per_step):
    n, t_len, n_heads, _ = q.shape
    n_pages = page_table.shape[1]
    page = cache_k.shape[2]
    assert MOBA_BLOCK % page == 0 and (n_pages * page) % (MOBA_BLOCK * blocks_per_step) == 0
    assert t_len <= MOBA_BLOCK
    ppb = MOBA_BLOCK // page
    nb = n_pages // ppb
    pps = ppb * blocks_per_step
    tok = pl.BlockSpec((1, t_len, n_heads, HEAD_DIM), lambda s, b, pt: (s, 0, 0, 0))

    def page_spec(j):
        return pl.BlockSpec((1, 1, page, n_heads, HEAD_DIM),
                            lambda s, b, pt: (layer, pt[s, b * pps + j], 0, 0, 0))

    rows = t_len * n_heads
    grid_spec = pltpu.PrefetchScalarGridSpec(
        num_scalar_prefetch=1,
        grid=(n, nb // blocks_per_step),
        in_specs=[tok, tok, tok] + [page_spec(j) for j in range(pps)] * 2,
        out_specs=tok,
        scratch_shapes=[pltpu.VMEM((nb, n_heads, HEAD_DIM), F32), pltpu.VMEM((nb, rows, LANES), F32),
                        pltpu.VMEM((nb, rows, LANES), F32), pltpu.VMEM((nb, rows, HEAD_DIM), F32)])
    return pl.pallas_call(
        functools.partial(_moba_decode_kernel, pages_per_block=ppb, blocks_per_step=blocks_per_step),
        grid_spec=grid_spec,
        out_shape=jax.ShapeDtypeStruct((n, t_len, n_heads, HEAD_DIM), F32),
        compiler_params=_cparams(("parallel", "arbitrary")),
        name="moba_decode",
    )(page_table, q, k_new, v_new, *([cache_k] * pps), *([cache_v] * pps))


def _attn_out_kernel(o_ref, sz_ref, x_ref, gate_ref, w_ref, y_ref):
    out = jnp.dot((o_ref[0] * sz_ref[0]).astype(BF16), w_ref[...], preferred_element_type=F32)
    y_ref[0] = x_ref[0] + gate_ref[0] * out


def _attn_out(o, sz, x, gate, w_out, tm):
    n, t, d = x.shape
    row = lambda: pl.BlockSpec((1, tm, d), lambda b, i: (b, i, 0))
    gspec = (pl.BlockSpec((1, 1, d), lambda b, i: (b, 0, 0)) if gate.shape[1] == 1 else row())
    return pl.pallas_call(
        _attn_out_kernel,
        grid=(n, t // tm),
        in_specs=[row(), row(), row(), gspec, pl.BlockSpec(w_out.shape, lambda b, i: (0, 0))],
        out_specs=row(),
        out_shape=jax.ShapeDtypeStruct((n, t, d), F32),
        compiler_params=_cparams(("parallel", "parallel")),
        name="attn_out",
    )(o, sz, x, gate, w_out)


def _pad_lanes(a):
    return jnp.pad(a, ((0, 0), (0, LANES - a.shape[-1])))


def kernel(x_prompt, x_sample, state_delta, state_qkv_conv, state_short_conv, cache_k, cache_v, page_table, c_prompt, c_sample, norm_g, ada_w, ada_b, dn_w_in, dn_conv_w, dn_a_log, dn_dt_bias, dn_norm_g, sc_conv_w, dn_w_out, att_w_in, att_qn_g, att_kn_g, att_w_out):
    bp, seq, d = x_prompt.shape
    bs, t_s, _ = x_sample.shape
    n_heads = dn_a_log.shape[-1]
    w = n_heads * HEAD_DIM
    sc_w = sc_conv_w.shape[-1]
    assert dn_w_in.shape[-1] == 4 * w + 2 * n_heads + 4 * sc_w and state_delta.shape[-2:] == (HEAD_DIM, HEAD_DIM)

    mod = _adaln(jnp.concatenate([c_prompt, c_sample], axis=0), ada_w, ada_b)
    mod_p, mod_s = mod[:, :bp], mod[:, bp:]

    wi = dn_w_in[0]
    w0 = jnp.concatenate([wi[:, :4 * w], _pad_lanes(wi[:, 4 * w:4 * w + 2 * n_heads]),
                          wi[:, 4 * w + 2 * n_heads:]], axis=1).astype(BF16)
    alog, dtb = _pad_lanes(dn_a_log[0][None]), _pad_lanes(dn_dt_bias[0][None])
    ng0 = norm_g[0][None]
    dng = dn_norm_g[0][None]
    w_out0 = dn_w_out[0].astype(BF16)
    nq, nsc = dn_conv_w.shape[1] - 1, sc_conv_w.shape[1] - 1

    q, k, v, gb, sza, yb, qkv_tail, sc_tail = _dn_in_prompt(
        x_prompt, mod_p[0][:, None], ng0, w0, dn_conv_w[0], alog, dtb, sc_conv_w[0], n_heads, sc_w)
    o_p, p_delta = _gated_delta_rule(q, k, v, gb, jnp.zeros((bp, n_heads, HEAD_DIM, HEAD_DIM), F32),
                                     DN_CHUNK_PROMPT, (1, DN_PREP_ROWS), bp)
    xp1 = _dn_out(o_p, sza, yb, x_prompt, mod_p[0][:, None, 2 * d:], dng, w_out0, 256)
    p_qkv, p_sc = qkv_tail[:, SUBLANES - nq:], sc_tail[:, SUBLANES - nsc:]

    tm_of = lambda a: jnp.swapaxes(a, 0, 1)
    qs, ks, vs, gbs, szas, ybs, s_qkv_tm, s_sc_tm = _dn_in_sample(
        tm_of(x_sample), mod_s[0], ng0, w0, tm_of(state_qkv_conv[0]), tm_of(state_short_conv[0]),
        dn_conv_w[0], alog, dtb, sc_conv_w[0], n_heads, sc_w)
    cs = DN_CHUNK_SAMPLE
    t_pad = -(-t_s // cs) * cs
    pad_t = lambda a: jnp.pad(tm_of(a), ((0, 0), (0, t_pad - t_s), (0, 0)))
    o_s, s_delta = _gated_delta_rule(pad_t(qs), pad_t(ks), pad_t(vs), pad_t(gbs), state_delta[0],
                                     cs, (8, t_pad), 8)
    rows_s = bs * t_s
    flat = lambda a: a.reshape(1, rows_s, a.shape[-1])
    gate_rows = lambda m: flat(jnp.repeat(m, t_s, axis=0))
    xs1 = _dn_out(flat(o_s[:, :t_s]), flat(tm_of(szas)), flat(tm_of(ybs)), flat(x_sample),
                  gate_rows(mod_s[0][:, 2 * d:]), dng, w_out0, rows_s)
    s_qkv, s_sc = tm_of(s_qkv_tm), tm_of(s_sc_tm)

    wa = att_w_in[0].astype(BF16)
    ng1, qn, kn = norm_g[1][None], att_qn_g[0][None], att_kn_g[0][None]
    w_out1 = att_w_out[0].astype(BF16)
    n_att_heads = d // HEAD_DIM

    q1, k1, v1, sz1, kb1, vt1, kmean = _attn_in(xp1, mod_p[1][:, None], ng1, wa, qn, kn, MOBA_BLOCK, True)
    o1 = _moba_prompt(q1, kb1, vt1, kmean.reshape(bp, seq // MOBA_BLOCK, d), MOBA_HEADS_PER_STEP)
    y_prompt = _attn_out(o1, sz1, xp1, mod_p[1][:, None, 2 * d:], w_out1, 256)

    q1s, k1s, v1s, sz1s = _attn_in(xs1, gate_rows(mod_s[1]), ng1, wa, qn, kn, rows_s, False)
    seqs = lambda a: a.reshape(bs, t_s, n_att_heads, HEAD_DIM)
    o1s = _moba_decode(seqs(q1s), seqs(k1s), seqs(v1s), cache_k, cache_v, page_table, 0, DECODE_BLOCKS_PER_STEP)
    y_sample = _attn_out(flat(o1s.reshape(bs, t_s, d)), sz1s, xs1, gate_rows(mod_s[1][:, 2 * d:]), w_out1, rows_s)

    heads = lambda a, n: a.reshape(1, n, -1, n_att_heads, HEAD_DIM)
    return (y_prompt, y_sample.reshape(bs, t_s, d),
            p_delta[None], p_qkv[None], p_sc[None], heads(k1, bp), heads(v1, bp),
            s_delta[None], s_qkv[None], s_sc[None], heads(k1s, bs), heads(v1s, bs))
```

```python
import functools

import jax
import jax.numpy as jnp
from jax import lax
from jax.experimental import pallas as pl
from jax.experimental.pallas import tpu as pltpu

F32 = jnp.float32
BF16 = jnp.bfloat16
HI = lax.Precision.HIGHEST

EPS = 1e-6
LANES = 128
SUBLANES = 8
SUBLANES_BF16 = 16
HEAD_DIM = 128
DN_CHUNK_PROMPT = 64
DN_CHUNK_SAMPLE = 16
DN_SOLVE_BLOCK = 16
MOBA_BLOCK = 256
MOBA_TOPK = 3
DECODE_BLOCKS_PER_STEP = 4
MOBA_HEADS_PER_STEP = 4
MOBA_KV_GROUP = 4
DN_PREP_ROWS = 512
NEG = -1e30
DN_GRAM_PASSES = 1
DN_SOLVE_PASSES = 1
DN_SCAN_PASSES = 1
VMEM_LIMIT = 48 * 1024 * 1024


def _cparams(sem):
    return pltpu.CompilerParams(dimension_semantics=sem, vmem_limit_bytes=VMEM_LIMIT)


def _silu(x):
    return x * jax.nn.sigmoid(x)


def _softplus(x):
    return jnp.maximum(x, 0.0) + jnp.log1p(jnp.exp(-jnp.abs(x)))


def _rms(x, g):
    return x * lax.rsqrt(jnp.mean(x * x, axis=-1, keepdims=True) + EPS) * g


def _modulate(x, norm_g, mod):
    d = x.shape[-1]
    return _rms(x, norm_g) * (1.0 + mod[:, d:2 * d]) + mod[:, :d]


def _dot_nt(a, b, precision=None):
    return lax.dot_general(a, b, (((1,), (1,)), ((), ())), precision=precision,
                           preferred_element_type=F32)


def _dot_tn(a, b, precision=None):
    return lax.dot_general(a, b, (((0,), (0,)), ((), ())), precision=precision,
                           preferred_element_type=F32)


def _adaln_kernel(c_ref, w_ref, b_ref, o_ref):
    o_ref[0] = jnp.dot(c_ref[...], w_ref[0], preferred_element_type=F32) + b_ref[0]


def _adaln(c_all, ada_w, ada_b):
    n_layers, d, d3 = ada_w.shape
    n = c_all.shape[0]
    tn = d
    return pl.pallas_call(
        _adaln_kernel,
        grid=(n_layers, d3 // tn),
        in_specs=[pl.BlockSpec((n, d), lambda l, j: (0, 0)),
                  pl.BlockSpec((1, d, tn), lambda l, j: (l, 0, j)),
                  pl.BlockSpec((1, 1, tn), lambda l, j: (l, 0, j))],
        out_specs=pl.BlockSpec((1, n, tn), lambda l, j: (l, 0, j)),
        out_shape=jax.ShapeDtypeStruct((n_layers, n, d3), F32),
        compiler_params=_cparams(("parallel", "parallel")),
        name="adaln_mod",
    )(c_all, ada_w, ada_b.reshape(n_layers, 1, d3))


def _dn_activations(qkv, ab, alog, dtb, n_heads):
    w = n_heads * HEAD_DIM
    act = _silu(qkv)
    qs, ks = [], []
    for h in range(n_heads):
        qh = act[:, h * HEAD_DIM:(h + 1) * HEAD_DIM]
        kh = act[:, w + h * HEAD_DIM:w + (h + 1) * HEAD_DIM]
        qs.append(qh * lax.rsqrt(jnp.sum(qh * qh, axis=-1, keepdims=True) + EPS) * (HEAD_DIM ** -0.5))
        ks.append(kh * lax.rsqrt(jnp.sum(kh * kh, axis=-1, keepdims=True) + EPS))
    q = jnp.concatenate(qs, axis=-1)
    k = jnp.concatenate(ks, axis=-1)
    v = act[:, 2 * w:3 * w]
    lane = lax.broadcasted_iota(jnp.int32, ab.shape, 1)
    g = -jnp.exp(alog) * _softplus(ab + dtb)
    beta = jax.nn.sigmoid(ab)
    gb = jnp.where(lane < n_heads, g, jnp.where(lane < 2 * n_heads, beta, 0.0))
    return q, k, v, gb


def _shift_rows(cur, prev, s):
    ext = jnp.concatenate([prev, cur], axis=0)
    return ext[SUBLANES - s:SUBLANES - s + cur.shape[0]]


def _conv_rows(cur, prev, w):
    width = w.shape[0]
    y = cur * w[width - 1:width]
    for s in range(1, width):
        y = y + _shift_rows(cur, prev, s) * w[width - 1 - s:width - s]
    return y


def _dn_in_prompt_kernel(x_ref, mod_ref, ng_ref, w_ref, cw_ref, alog_ref, dtb_ref, scw_ref,
                         q_ref, k_ref, v_ref, gb_ref, sza_ref, yb_ref, qkvt_ref, sct_ref,
                         prev_qkv, prev_sc, *, n_heads, sc_w):
    i = pl.program_id(1)
    w = n_heads * HEAD_DIM
    o_z, o_ab, o_sc = 3 * w, 4 * w, 4 * w + LANES

    @pl.when(i == 0)
    def _():
        prev_qkv[...] = jnp.zeros_like(prev_qkv)
        prev_sc[...] = jnp.zeros_like(prev_sc)

    h = _modulate(x_ref[0], ng_ref[...], mod_ref[0]).astype(BF16)
    raw = jnp.dot(h, w_ref[:, 0:o_z], preferred_element_type=F32)
    qkv = _conv_rows(raw, prev_qkv[...], cw_ref[...])
    ab = jnp.dot(h, w_ref[:, o_ab:o_sc], preferred_element_type=F32)
    q, k, v, gb = _dn_activations(qkv, ab, alog_ref[...], dtb_ref[...], n_heads)
    q_ref[0], k_ref[0], v_ref[0], gb_ref[0] = q, k, v, gb
    prev_qkv[...] = raw[-SUBLANES:]
    qkvt_ref[0] = raw[-SUBLANES:]

    sza_ref[0] = _silu(jnp.dot(h, w_ref[:, o_z:o_ab], preferred_element_type=F32)).astype(BF16)

    sc = jnp.dot(h, w_ref[:, o_sc:o_sc + 4 * sc_w], preferred_element_type=F32)
    cx = sc[:, sc_w:2 * sc_w] * sc[:, 2 * sc_w:3 * sc_w]
    cv = _conv_rows(cx, prev_sc[...], scw_ref[...])
    yb_ref[0] = (sc[:, 0:sc_w] * cv * _silu(sc[:, 3 * sc_w:4 * sc_w])).astype(BF16)
    prev_sc[...] = cx[-SUBLANES:]
    sct_ref[0] = cx[-SUBLANES:]


def _dn_in_prompt(x, mod, norm_g, w0, conv_w, alog, dtb, sc_conv_w, n_heads, sc_w, tm=256):
    n, t, d = x.shape
    w = n_heads * HEAD_DIM
    wtot = w0.shape[1]
    row = lambda c: pl.BlockSpec((1, tm, c), lambda b, i: (b, i, 0))
    full = lambda a: pl.BlockSpec(a.shape, lambda b, i: (0,) * a.ndim)
    tail = lambda c: pl.BlockSpec((1, SUBLANES, c), lambda b, i: (b, 0, 0))
    outs = [jax.ShapeDtypeStruct((n, t, w), F32)] * 3 + [
        jax.ShapeDtypeStruct((n, t, LANES), F32), jax.ShapeDtypeStruct((n, t, w), BF16),
        jax.ShapeDtypeStruct((n, t, sc_w), BF16),
        jax.ShapeDtypeStruct((n, SUBLANES, 3 * w), F32), jax.ShapeDtypeStruct((n, SUBLANES, sc_w), F32)]
    return pl.pallas_call(
        functools.partial(_dn_in_prompt_kernel, n_heads=n_heads, sc_w=sc_w),
        grid=(n, t // tm),
        in_specs=[row(d), pl.BlockSpec((1, 1, 3 * d), lambda b, i: (b, 0, 0)), full(norm_g),
                  pl.BlockSpec((d, wtot), lambda b, i: (0, 0)), full(conv_w), full(alog), full(dtb),
                  full(sc_conv_w)],
        out_specs=[row(w), row(w), row(w), row(LANES), row(w), row(sc_w), tail(3 * w), tail(sc_w)],
        out_shape=outs,
        scratch_shapes=[pltpu.VMEM((SUBLANES, 3 * w), F32), pltpu.VMEM((SUBLANES, sc_w), F32)],
        compiler_params=_cparams(("parallel", "arbitrary")),
        name="dn_in_prompt",
    )(x, mod, norm_g, w0, conv_w, alog, dtb, sc_conv_w)


def _dn_in_sample_kernel(x_ref, mod_ref, ng_ref, w_ref, sq_ref, ssc_ref, cw_ref, alog_ref, dtb_ref,
                         scw_ref, q_ref, k_ref, v_ref, gb_ref, sza_ref, yb_ref, nq_ref, nsc_ref,
                         *, n_heads, sc_w):
    t_len = x_ref.shape[0]
    w = n_heads * HEAD_DIM
    o_z, o_ab, o_sc = 3 * w, 4 * w, 4 * w + LANES
    cw, scw = cw_ref[...], scw_ref[...]
    nq, nsc = cw.shape[0] - 1, scw.shape[0] - 1
    xq = [sq_ref[j] for j in range(nq)]
    xsc = [ssc_ref[j] for j in range(nsc)]
    hs, sc_all = [], []
    for t in range(t_len):
        h = _modulate(x_ref[t], ng_ref[...], mod_ref[...]).astype(BF16)
        hs.append(h)
        xq.append(jnp.dot(h, w_ref[:, 0:o_z], preferred_element_type=F32))
        sc = jnp.dot(h, w_ref[:, o_sc:o_sc + 4 * sc_w], preferred_element_type=F32)
        sc_all.append(sc)
        xsc.append(sc[:, sc_w:2 * sc_w] * sc[:, 2 * sc_w:3 * sc_w])
    for t in range(t_len):
        qkv = sum(xq[t + j] * cw[j:j + 1] for j in range(nq + 1))
        ab = jnp.dot(hs[t], w_ref[:, o_ab:o_sc], preferred_element_type=F32)
        q, k, v, gb = _dn_activations(qkv, ab, alog_ref[...], dtb_ref[...], n_heads)
        q_ref[t], k_ref[t], v_ref[t], gb_ref[t] = q, k, v, gb
        sza_ref[t] = _silu(jnp.dot(hs[t], w_ref[:, o_z:o_ab], preferred_element_type=F32)).astype(BF16)
        cv = sum(xsc[t + j] * scw[j:j + 1] for j in range(nsc + 1))
        sc = sc_all[t]
        yb_ref[t] = (sc[:, 0:sc_w] * cv * _silu(sc[:, 3 * sc_w:4 * sc_w])).astype(BF16)
    for j in range(nq):
        nq_ref[j] = xq[t_len + j]
    for j in range(nsc):
        nsc_ref[j] = xsc[t_len + j]


def _dn_in_sample(x_tm, mod, norm_g, w0, s_qkv_tm, s_sc_tm, conv_w, alog, dtb, sc_conv_w, n_heads, sc_w):
    t, n, d = x_tm.shape
    w = n_heads * HEAD_DIM
    gs = min(n, 128)
    wtot = w0.shape[1]
    tm3 = lambda r, c: pl.BlockSpec((r, gs, c), lambda g: (0, g, 0))
    full = lambda a: pl.BlockSpec(a.shape, lambda g: (0,) * a.ndim)
    nq, nsc = conv_w.shape[0] - 1, sc_conv_w.shape[0] - 1
    outs = [jax.ShapeDtypeStruct((t, n, w), F32)] * 3 + [
        jax.ShapeDtypeStruct((t, n, LANES), F32), jax.ShapeDtypeStruct((t, n, w), BF16),
        jax.ShapeDtypeStruct((t, n, sc_w), BF16),
        jax.ShapeDtypeStruct((nq, n, 3 * w), F32), jax.ShapeDtypeStruct((nsc, n, sc_w), F32)]
    return pl.pallas_call(
        functools.partial(_dn_in_sample_kernel, n_heads=n_heads, sc_w=sc_w),
        grid=(n // gs,),
        in_specs=[tm3(t, d), pl.BlockSpec((gs, 3 * d), lambda g: (g, 0)), full(norm_g),
                  pl.BlockSpec((d, wtot), lambda g: (0, 0)), tm3(nq, 3 * w), tm3(nsc, sc_w),
                  full(conv_w), full(alog), full(dtb), full(sc_conv_w)],
        out_specs=[tm3(t, w), tm3(t, w), tm3(t, w), tm3(t, LANES), tm3(t, w), tm3(t, sc_w),
                   tm3(nq, 3 * w), tm3(nsc, sc_w)],
        out_shape=outs,
        compiler_params=_cparams(("parallel",)),
        name="dn_in_sample",
    )(x_tm, mod, norm_g, w0, s_qkv_tm, s_sc_tm, conv_w, alog, dtb, sc_conv_w)


def _bf16_parts(x, n):
    parts, r = [], x
    for i in range(n):
        p = r.astype(BF16)
        parts.append(p)
        if i + 1 < n:
            r = r - p.astype(F32)
    return parts


def _mm(dot, a, b, passes):
    if passes == 1:
        return dot(a.astype(BF16), b.astype(BF16))
    a_hi, a_lo = _bf16_parts(a, 2)
    b_hi, b_lo = _bf16_parts(b, 2)
    return dot(a_hi, b_hi) + (dot(a_hi, b_lo) + dot(a_lo, b_hi))


def _mm_exact01(dot, sel, x):
    hi, mid, lo = (p.astype(sel.dtype) for p in _bf16_parts(x, 3))
    return dot(sel, hi) + (dot(sel, mid) + dot(sel, lo))


def _bdot(a, b):
    return jnp.einsum("bij,bjk->bik", a, b, preferred_element_type=F32)


def _bdot_nt(a, b):
    return jnp.einsum("bik,bjk->bij", a, b, preferred_element_type=F32)


def _dot(a, b):
    return jnp.dot(a, b, preferred_element_type=F32)


def _neumann_inverse(m, order, eye):
    p = eye - m
    mk, k = m, 1
    while 2 * k < order:
        mk = _mm(_bdot, mk, mk, DN_SOLVE_PASSES)
        p = p + _mm(_bdot, p, mk, DN_SOLVE_PASSES)
        k *= 2
    return p


def _unit_lower_inverse(m, c):
    row = lax.broadcasted_iota(jnp.int32, (c, c), 0)
    col = lax.broadcasted_iota(jnp.int32, (c, c), 1)
    eye = (row == col).astype(F32)
    if c <= DN_SOLVE_BLOCK:
        return _neumann_inverse(m, c, eye)
    on_diag = (row // DN_SOLVE_BLOCK) == (col // DN_SOLVE_BLOCK)
    m_diag = jnp.where(on_diag, m, 0.0)
    d_inv = _neumann_inverse(m_diag, DN_SOLVE_BLOCK, eye)
    n_off = _mm(_bdot, d_inv, m - m_diag, DN_SOLVE_PASSES)
    return _mm(_bdot, _neumann_inverse(n_off, c // DN_SOLVE_BLOCK, eye), d_inv, DN_SOLVE_PASSES)


def _delta_prep_kernel(q_ref, k_ref, v_ref, gb_ref, u_ref, w_ref, qg_ref, kd_ref, el_ref, a_ref, *, c):
    g_dim, r_dim, wq = q_ref.shape
    n_heads = wq // HEAD_DIM
    rows = g_dim * r_dim
    nb = rows // c
    row = lax.broadcasted_iota(jnp.int32, (c, c), 0)
    col = lax.broadcasted_iota(jnp.int32, (c, c), 1)
    gb2 = gb_ref[...].reshape(rows, LANES)
    tril = jnp.broadcast_to((row >= col).astype(BF16), (nb, c, c))
    gc3 = _mm_exact01(_bdot, tril, gb2.reshape(nb, c, LANES))
    gc2 = gc3.reshape(rows, LANES)
    gt2 = _mm_exact01(_bdot, jnp.ones((nb, c, c), BF16), gb2.reshape(nb, c, LANES)).reshape(rows, LANES)
    hs = lambda h: slice(h * HEAD_DIM, (h + 1) * HEAD_DIM)
    stack = lambda f: jnp.concatenate([f(h) for h in range(n_heads)], axis=0)
    gch = stack(lambda h: jnp.broadcast_to(gc2[:, h:h + 1], (rows, LANES)).reshape(nb, c, LANES))
    beta = stack(lambda h: jnp.broadcast_to(gb2[:, n_heads + h:n_heads + h + 1], (rows, LANES)).reshape(nb, c, LANES))
    diag_gc = jnp.where(row == col, gch[:, :, :c], 0.0)
    gc_cols = _mm_exact01(_bdot, jnp.ones(diag_gc.shape, BF16), diag_gc)
    gc_last = stack(lambda h: jnp.broadcast_to(gt2[:, h:h + 1], (rows, LANES)).reshape(nb, c, LANES))
    decay = jnp.where(row >= col, jnp.exp(gch[:, :, :c] - gc_cols), 0.0)
    qh = stack(lambda h: q_ref[:, :, hs(h)].reshape(nb, c, HEAD_DIM))
    kh = stack(lambda h: k_ref[:, :, hs(h)].reshape(nb, c, HEAD_DIM))
    vh = stack(lambda h: v_ref[:, :, hs(h)].reshape(nb, c, HEAD_DIM))
    kb = kh * beta
    m = jnp.where(row > col, _mm(_bdot_nt, kb, kh, DN_GRAM_PASSES) * decay, 0.0)
    a = _mm(_bdot_nt, qh, kh, DN_GRAM_PASSES) * decay
    t_inv = _unit_lower_inverse(m, c)
    egc = jnp.exp(gch)
    sol = _mm(_bdot, t_inv, jnp.concatenate([vh * beta, kb * egc], axis=-1), DN_SOLVE_PASSES)
    qg, kd, el = qh * egc, kh * jnp.exp(gc_last - gch), jnp.exp(gc_last)
    for h in range(n_heads):
        of_head = lambda x: x[h * nb:(h + 1) * nb].reshape(g_dim, r_dim, x.shape[-1])
        u_ref[:, :, hs(h)] = of_head(sol[:, :, :HEAD_DIM])
        el_ref[:, :, hs(h)] = of_head(el)
        w_ref[:, :, hs(h)] = of_head(sol[:, :, HEAD_DIM:]).astype(w_ref.dtype)
        qg_ref[:, :, hs(h)] = of_head(qg).astype(qg_ref.dtype)
        kd_ref[:, :, hs(h)] = of_head(kd).astype(kd_ref.dtype)
        a_ref[:, :, h * c:(h + 1) * c] = of_head(a).astype(a_ref.dtype)


def _delta_prep(q, k, v, gb, c, g_dim, r_dim):
    n, t, wq = q.shape
    n_heads = wq // HEAD_DIM
    blk = lambda cc: pl.BlockSpec((g_dim, r_dim, cc), lambda b, i: (b, i, 0))
    operand = BF16 if DN_SCAN_PASSES == 1 else F32
    outs = ([jax.ShapeDtypeStruct((n, t, wq), F32)] + [jax.ShapeDtypeStruct((n, t, wq), operand)] * 3
            + [jax.ShapeDtypeStruct((n, t, wq), F32), jax.ShapeDtypeStruct((n, t, n_heads * c), operand)])
    return pl.pallas_call(
        functools.partial(_delta_prep_kernel, c=c),
        grid=(n // g_dim, t // r_dim),
        in_specs=[blk(wq), blk(wq), blk(wq), blk(LANES)],
        out_specs=[blk(wq)] * 5 + [blk(n_heads * c)],
        out_shape=outs,
        compiler_params=_cparams(("parallel", "parallel")),
        name="delta_prep",
    )(q, k, v, gb)


def _delta_scan_kernel(u_ref, w_ref, qg_ref, kd_ref, el_ref, a_ref, s0_ref, o_ref, s_ref, *, c):
    @pl.when(pl.program_id(1) == 0)
    def _():
        s_ref[...] = s0_ref[...]

    g_dim, _, wq = u_ref.shape
    chains = [(g, h, slice(h * HEAD_DIM, (h + 1) * HEAD_DIM)) for g in range(g_dim) for h in range(wq // HEAD_DIM)]
    states = [s_ref[g, h] for g, h, _ in chains]
    v_new = [u_ref[g, :, sl] - _mm(_dot, w_ref[g, :, sl], s, DN_SCAN_PASSES)
             for (g, _, sl), s in zip(chains, states)]
    o_inter = [_mm(_dot, qg_ref[g, :, sl], s, DN_SCAN_PASSES) for (g, _, sl), s in zip(chains, states)]
    for (g, h, sl), s, vn, oi in zip(chains, states, v_new, o_inter):
        o_ref[g, :, sl] = oi + _mm(_dot, a_ref[g, :, h * c:(h + 1) * c], vn, DN_SCAN_PASSES)
    for (g, h, sl), s, vn in zip(chains, states, v_new):
        s_ref[g, h] = s * el_ref[g, 0:1, sl] + _mm(_dot_tn, kd_ref[g, :, sl], vn, DN_SCAN_PASSES)


def _delta_scan(u, w, qg, kd, el, a, s0, c, g_dim):
    n, t, wq = u.shape
    n_heads = wq // HEAD_DIM
    blk = lambda cc: pl.BlockSpec((g_dim, c, cc), lambda b, i: (b, i, 0))
    st = pl.BlockSpec((g_dim, n_heads, HEAD_DIM, HEAD_DIM), lambda b, i: (b, 0, 0, 0))
    return pl.pallas_call(
        functools.partial(_delta_scan_kernel, c=c),
        grid=(n // g_dim, t // c),
        in_specs=[blk(wq)] * 5 + [blk(n_heads * c), st],
        out_specs=[blk(wq), st],
        out_shape=[jax.ShapeDtypeStruct((n, t, wq), F32),
                   jax.ShapeDtypeStruct((n, n_heads, HEAD_DIM, HEAD_DIM), F32)],
        compiler_params=_cparams(("parallel", "arbitrary")),
        name="delta_scan",
    )(u, w, qg, kd, el, a, s0)


def _gated_delta_rule(q, k, v, gb, s0, c, prep_block, scan_seqs):
    u, w, qg, kd, el, a = _delta_prep(q, k, v, gb, c, *prep_block)
    return _delta_scan(u, w, qg, kd, el, a, s0, c, scan_seqs)


def _dn_out_kernel(o_ref, sza_ref, yb_ref, x_ref, gate_ref, ng_ref, w_ref, y_ref):
    o = o_ref[0]
    wa = o.shape[-1]
    ya = jnp.concatenate([_rms(o[:, h * HEAD_DIM:(h + 1) * HEAD_DIM], ng_ref[...])
                          for h in range(wa // HEAD_DIM)], axis=-1) * sza_ref[0]
    out = (jnp.dot(ya.astype(BF16), w_ref[0:wa, :], preferred_element_type=F32)
           + jnp.dot(yb_ref[0].astype(BF16), w_ref[wa:, :], preferred_element_type=F32))
    y_ref[0] = x_ref[0] + gate_ref[0] * out


def _dn_out(o, sza, yb, x, gate, dn_norm_g, w_out, tm):
    n, t, d = x.shape
    wa, wb = o.shape[-1], yb.shape[-1]
    row = lambda c: pl.BlockSpec((1, tm, c), lambda b, i: (b, i, 0))
    gate_rows = gate.shape[1]
    gspec = (pl.BlockSpec((1, 1, d), lambda b, i: (b, 0, 0)) if gate_rows == 1
             else pl.BlockSpec((1, tm, d), lambda b, i: (b, i, 0)))
    return pl.pallas_call(
        _dn_out_kernel,
        grid=(n, t // tm),
        in_specs=[row(wa), row(wa), row(wb), row(d), gspec,
                  pl.BlockSpec(dn_norm_g.shape, lambda b, i: (0, 0)),
                  pl.BlockSpec(w_out.shape, lambda b, i: (0, 0))],
        out_specs=row(d),
        out_shape=jax.ShapeDtypeStruct((n, t, d), F32),
        compiler_params=_cparams(("parallel", "parallel")),
        name="dn_out",
    )(o, sza, yb, x, gate, dn_norm_g, w_out)


def _attn_in_kernel(x_ref, mod_ref, ng_ref, w_ref, qn_ref, kn_ref, q_ref, k_ref, v_ref, sz_ref, *prompt_refs):
    d = x_ref.shape[-1]
    h = _modulate(x_ref[0], ng_ref[...], mod_ref[0]).astype(BF16)
    heads = range(d // HEAD_DIM)
    qr = jnp.dot(h, w_ref[:, 0:d], preferred_element_type=F32)
    q_ref[0] = jnp.concatenate([_rms(qr[:, i * HEAD_DIM:(i + 1) * HEAD_DIM], qn_ref[...]) for i in heads], axis=-1)
    kr = jnp.dot(h, w_ref[:, d:2 * d], preferred_element_type=F32)
    k = jnp.concatenate([_rms(kr[:, i * HEAD_DIM:(i + 1) * HEAD_DIM], kn_ref[...]) for i in heads], axis=-1)
    k_ref[0] = k
    v = jnp.dot(h, w_ref[:, 2 * d:3 * d], preferred_element_type=F32)
    v_ref[0] = v
    sz_ref[0] = _silu(jnp.dot(h, w_ref[:, 3 * d:4 * d], preferred_element_type=F32)).astype(BF16)
    if prompt_refs:
        kb_ref, vt_ref, kmean_ref = prompt_refs
        kb_ref[0] = k.astype(BF16)
        vt_ref[0] = v.T.astype(BF16)
        kmean_ref[0, 0] = jnp.mean(k, axis=0, keepdims=True)


def _attn_in(x, mod, norm_g, w_in, qn_g, kn_g, tm, for_prompt):
    n, t, d = x.shape
    row = lambda: pl.BlockSpec((1, tm, d), lambda b, i: (b, i, 0))
    mod_rows = mod.shape[1]
    mspec = (pl.BlockSpec((1, 1, 3 * d), lambda b, i: (b, 0, 0)) if mod_rows == 1
             else pl.BlockSpec((1, tm, 3 * d), lambda b, i: (b, i, 0)))
    full = lambda a: pl.BlockSpec(a.shape, lambda b, i: (0,) * a.ndim)
    out_specs = [row()] * 4
    outs = [jax.ShapeDtypeStruct((n, t, d), F32)] * 3 + [jax.ShapeDtypeStruct((n, t, d), BF16)]
    if for_prompt:
        assert tm == MOBA_BLOCK
        out_specs += [row(), pl.BlockSpec((1, d, tm), lambda b, i: (b, 0, i)),
                      pl.BlockSpec((1, 1, 1, d), lambda b, i: (b, i, 0, 0))]
        outs += [jax.ShapeDtypeStruct((n, t, d), BF16), jax.ShapeDtypeStruct((n, d, t), BF16),
                 jax.ShapeDtypeStruct((n, t // tm, 1, d), F32)]
    return pl.pallas_call(
        _attn_in_kernel,
        grid=(n, t // tm),
        in_specs=[row(), mspec, full(norm_g), full(w_in), full(qn_g), full(kn_g)],
        out_specs=out_specs,
        out_shape=outs,
        compiler_params=_cparams(("parallel", "parallel")),
        name="attn_in",
    )(x, mod, norm_g, w_in, qn_g, kn_g)


def _topk_select(gate, cand, n_blocks, axis):
    blk = lax.broadcasted_iota(jnp.int32, gate.shape, axis)
    gm = jnp.where(cand, gate, -jnp.inf)
    rank = jnp.zeros(gate.shape, jnp.int32)
    for b in range(n_blocks):
        other = gm[b:b + 1, :] if axis == 0 else gm[:, b:b + 1]
        rank = rank + ((other > gm) | ((other == gm) & (b < blk))).astype(jnp.int32)
    return cand & (rank < MOBA_TOPK)


def _moba_prompt_kernel(q_ref, k_ref, vt_ref, kmean_ref, o_ref, acc_scr, sel_scr, *, group):
    i = pl.program_id(2)
    blk = q_ref.shape[1]
    n_blocks = kmean_ref.shape[1]
    heads = range(q_ref.shape[2] // HEAD_DIM)
    hs = lambda h: slice(h * HEAD_DIM, (h + 1) * HEAD_DIM)
    start = pl.multiple_of(i * blk, blk)
    causal = (lax.broadcasted_iota(jnp.int32, (blk, blk), 0) <= lax.broadcasted_iota(jnp.int32, (blk, blk), 1))

    def values_and_ones(h, st, n_keys):
        return jnp.concatenate([vt_ref[0, hs(h), pl.ds(st, n_keys)], jnp.ones((SUBLANES_BF16, n_keys), BF16)], axis=0)

    qfs = [q_ref[0, :, hs(h)] for h in heads]
    qbs = [(qf * (HEAD_DIM ** -0.5)).astype(BF16) for qf in qfs]
    s_own = [_dot_nt(k_ref[0, pl.ds(start, blk), hs(h)], qbs[h]).astype(BF16) for h in heads]
    gates = [_dot_nt(kmean_ref[0, :, hs(h)], qfs[h], precision=HI) for h in heads]
    m0, p_own = [], []
    for h in heads:
        s = jnp.where(causal, s_own[h], NEG)
        m = jnp.max(s, axis=0, keepdims=True)
        m0.append(m.astype(F32))
        p_own.append(jnp.exp(s - m))
    for h in heads:
        acc_scr[h] = jnp.dot(values_and_ones(h, start, blk), p_own[h], preferred_element_type=F32)
        cand = lax.broadcasted_iota(jnp.int32, gates[h].shape, 0) < i
        sel_scr[h] = _topk_select(gates[h], cand, n_blocks, 0).astype(F32)

    def past_group(jg, ms):
        st = pl.multiple_of(jg * (group * blk), group * blk)
        sgs = [_dot_nt(k_ref[0, pl.ds(st, group * blk), hs(h)], qbs[h]).astype(BF16) for h in heads]
        new_m, pjs = [], []
        for h in heads:
            sj = [jnp.where(sel_scr[h, pl.ds(jg * group + g, 1), :] > 0.5, sgs[h][g * blk:(g + 1) * blk], NEG)
                  for g in range(group)]
            m_blk = functools.reduce(jnp.maximum, [jnp.max(x, axis=0, keepdims=True) for x in sj])
            m_new = jnp.maximum(ms[h], m_blk.astype(F32))
            m_b16 = m_new.astype(BF16)
            pjs.append(jnp.concatenate([jnp.exp(x - m_b16) for x in sj], axis=0))
            new_m.append(m_new)
        for h in heads:
            acc_scr[h] = (jnp.exp(ms[h] - new_m[h]) * acc_scr[h]
                          + jnp.dot(values_and_ones(h, st, group * blk), pjs[h], preferred_element_type=F32))
        return tuple(new_m)

    lax.fori_loop(0, lax.div(i + (group - 1), group), past_group, tuple(m0))
    for h in heads:
        acc = acc_scr[h]
        o_ref[0, :, hs(h)] = (acc[:HEAD_DIM] / acc[HEAD_DIM:HEAD_DIM + 1]).T


def _moba_prompt(q, kb, vt, kmean, heads_per_step):
    n, t, d = q.shape
    n_blocks = t // MOBA_BLOCK
    wg = heads_per_step * HEAD_DIM
    group = max(g for g in range(1, MOBA_KV_GROUP + 1) if n_blocks % g == 0)
    qspec = pl.BlockSpec((1, MOBA_BLOCK, wg), lambda b, h, i: (b, i, h))
    return pl.pallas_call(
        functools.partial(_moba_prompt_kernel, group=group),
        grid=(n, d // wg, n_blocks),
        in_specs=[qspec, pl.BlockSpec((1, t, wg), lambda b, h, i: (b, 0, h)),
                  pl.BlockSpec((1, wg, t), lambda b, h, i: (b, h, 0)),
                  pl.BlockSpec((1, n_blocks, wg), lambda b, h, i: (b, 0, h))],
        out_specs=qspec,
        out_shape=jax.ShapeDtypeStruct((n, t, d), F32),
        scratch_shapes=[pltpu.VMEM((heads_per_step, HEAD_DIM + SUBLANES_BF16, MOBA_BLOCK), F32),
                        pltpu.VMEM((heads_per_step, n_blocks, MOBA_BLOCK), F32)],
        compiler_params=_cparams(("parallel", "parallel", "arbitrary")),
        name="moba_prompt",
    )(q, kb, vt, kmean)


def _moba_decode_kernel(pt_ref, q_ref, kn_ref, vn_ref, *refs, pages_per_block, blocks_per_step):
    del pt_ref
    n_pages = pages_per_block * blocks_per_step
    k_refs, v_refs = refs[:n_pages], refs[n_pages:2 * n_pages]
    o_ref, kmean_scr, m_scr, l_scr, acc_scr = refs[2 * n_pages:]
    step = pl.program_id(1)
    t_len, n_heads = q_ref.shape[1], q_ref.shape[2]
    rows = t_len * n_heads
    page = k_refs[0].shape[2]
    cols = page * n_heads
    qall = q_ref[0].reshape(rows, HEAD_DIM)
    qs = qall * (HEAD_DIM ** -0.5)
    same_head = (lax.broadcasted_iota(jnp.int32, (rows, cols), 1) % n_heads
                 == lax.broadcasted_iota(jnp.int32, (rows, cols), 0) % n_heads)

    scores = [_dot_nt(qs, r[0, 0].reshape(cols, HEAD_DIM)) for r in k_refs]
    probs = []
    for bb in range(blocks_per_step):
        pages = range(bb * pages_per_block, (bb + 1) * pages_per_block)
        s = [jnp.where(same_head, scores[j], NEG) for j in pages]
        m_b = functools.reduce(jnp.maximum, [jnp.max(x, axis=1, keepdims=True) for x in s])
        p = [jnp.exp(x - m_b) for x in s]
        probs.append(p)
        idx = step * blocks_per_step + bb
        kmean_scr[idx] = sum(jnp.sum(k_refs[j][0, 0], axis=0) for j in pages) / (page * pages_per_block)
        m_scr[idx] = jnp.broadcast_to(m_b, (rows, LANES))
        l_scr[idx] = jnp.broadcast_to(sum(jnp.sum(x, axis=1, keepdims=True) for x in p), (rows, LANES))
    for bb in range(blocks_per_step):
        acc_scr[step * blocks_per_step + bb] = sum(
            jnp.dot(x, v_refs[bb * pages_per_block + j][0, 0].reshape(cols, HEAD_DIM), preferred_element_type=F32)
            for j, x in enumerate(probs[bb]))

    @pl.when(step == pl.num_programs(1) - 1)
    def _():
        nb = kmean_scr.shape[0]
        per_row = lambda a: jnp.broadcast_to(a[None], (t_len, n_heads, HEAD_DIM)).reshape(rows, HEAD_DIM)
        gate = jnp.concatenate([jnp.sum(qall * per_row(kmean_scr[j]), axis=1, keepdims=True)
                                for j in range(nb)], axis=1)
        sel = _topk_select(gate, jnp.full(gate.shape, True), nb, 1)
        tok = lax.broadcasted_iota(jnp.int32, (rows, 1), 0) // n_heads
        s_own = [jnp.sum(qs * per_row(kn_ref[0, j]), axis=1, keepdims=True) for j in range(t_len)]
        m_all = functools.reduce(jnp.maximum, [jnp.where(tok >= j, s_own[j], NEG) for j in range(t_len)])
        for j in range(nb):
            m_all = jnp.maximum(m_all, jnp.where(sel[:, j:j + 1], m_scr[j][:, 0:1], NEG))
        den = jnp.zeros((rows, 1), F32)
        num = jnp.zeros((rows, HEAD_DIM), F32)
        for j in range(t_len):
            pj = jnp.where(tok >= j, jnp.exp(s_own[j] - m_all), 0.0)
            den = den + pj
            num = num + pj * per_row(vn_ref[0, j])
        for j in range(nb):
            wj = jnp.where(sel[:, j:j + 1], jnp.exp(m_scr[j][:, 0:1] - m_all), 0.0)
            den = den + wj * l_scr[j][:, 0:1]
            num = num + wj * acc_scr[j]
        o_ref[0] = (num / den).reshape(t_len, n_heads, HEAD_DIM)


def _moba_decode(q, k_new, v_new, cache_k, cache_v, page_table, layer, blocks_per_step):
    n, t_len, n_heads, _ = q.shape
    n_pages = page_table.shape[1]
    page = cache_k.shape[2]
    assert MOBA_BLOCK % page == 0 and (n_pages * page) % (MOBA_BLOCK * blocks_per_step) == 0
    assert t_len <= MOBA_BLOCK
    ppb = MOBA_BLOCK // page
    nb = n_pages // ppb
    pps = ppb * blocks_per_step
    tok = pl.BlockSpec((1, t_len, n_heads, HEAD_DIM), lambda s, b, pt: (s, 0, 0, 0))

    def page_spec(j):
        return pl.BlockSpec((1, 1, page, n_heads, HEAD_DIM),
                            lambda s, b, pt: (layer, pt[s, b * pps + j], 0, 0, 0))

    rows = t_len * n_heads
    grid_spec = pltpu.PrefetchScalarGridSpec(
        num_scalar_prefetch=1,
        grid=(n, nb // blocks_per_step),
        in_specs=[tok, tok, tok] + [page_spec(j) for j in range(pps)] * 2,
        out_specs=tok,
        scratch_shapes=[pltpu.VMEM((nb, n_heads, HEAD_DIM), F32), pltpu.VMEM((nb, rows, LANES), F32),
                        pltpu.VMEM((nb, rows, LANES), F32), pltpu.VMEM((nb, rows, HEAD_DIM), F32)])
    return pl.pallas_call(
        functools.partial(_moba_decode_kernel, pages_per_block=ppb, blocks_per_step=blocks_per_step),
        grid_spec=grid_spec,
        out_shape=jax.ShapeDtypeStruct((n, t_len, n_heads, HEAD_DIM), F32),
        compiler_params=_cparams(("parallel", "arbitrary")),
        name="moba_decode",
    )(page_table, q, k_new, v_new, *([cache_k] * pps), *([cache_v] * pps))


def _attn_out_kernel(o_ref, sz_ref, x_ref, gate_ref, w_ref, y_ref):
    out = jnp.dot((o_ref[0] * sz_ref[0]).astype(BF16), w_ref[...], preferred_element_type=F32)
    y_ref[0] = x_ref[0] + gate_ref[0] * out


def _attn_out(o, sz, x, gate, w_out, tm):
    n, t, d = x.shape
    row = lambda: pl.BlockSpec((1, tm, d), lambda b, i: (b, i, 0))
    gspec = (pl.BlockSpec((1, 1, d), lambda b, i: (b, 0, 0)) if gate.shape[1] == 1 else row())
    return pl.pallas_call(
        _attn_out_kernel,
        grid=(n, t // tm),
        in_specs=[row(), row(), row(), gspec, pl.BlockSpec(w_out.shape, lambda b, i: (0, 0))],
        out_specs=row(),
        out_shape=jax.ShapeDtypeStruct((n, t, d), F32),
        compiler_params=_cparams(("parallel", "parallel")),
        name="attn_out",
    )(o, sz, x, gate, w_out)


def _pad_lanes(a):
    return jnp.pad(a, ((0, 0), (0, LANES - a.shape[-1])))


def kernel(x_prompt, x_sample, state_delta, state_qkv_conv, state_short_conv, cache_k, cache_v, page_table, c_prompt, c_sample, norm_g, ada_w, ada_b, dn_w_in, dn_conv_w, dn_a_log, dn_dt_bias, dn_norm_g, sc_conv_w, dn_w_out, att_w_in, att_qn_g, att_kn_g, att_w_out):
    bp, seq, d = x_prompt.shape
    bs, t_s, _ = x_sample.shape
    n_heads = dn_a_log.shape[-1]
    w = n_heads * HEAD_DIM
    sc_w = sc_conv_w.shape[-1]
    assert dn_w_in.shape[-1] == 4 * w + 2 * n_heads + 4 * sc_w and state_delta.shape[-2:] == (HEAD_DIM, HEAD_DIM)

    mod = _adaln(jnp.concatenate([c_prompt, c_sample], axis=0), ada_w, ada_b)
    mod_p, mod_s = mod[:, :bp], mod[:, bp:]

    wi = dn_w_in[0]
    w0 = jnp.concatenate([wi[:, :4 * w], _pad_lanes(wi[:, 4 * w:4 * w + 2 * n_heads]),
                          wi[:, 4 * w + 2 * n_heads:]], axis=1).astype(BF16)
    alog, dtb = _pad_lanes(dn_a_log[0][None]), _pad_lanes(dn_dt_bias[0][None])
    ng0 = norm_g[0][None]
    dng = dn_norm_g[0][None]
    w_out0 = dn_w_out[0].astype(BF16)
    nq, nsc = dn_conv_w.shape[1] - 1, sc_conv_w.shape[1] - 1

    q, k, v, gb, sza, yb, qkv_tail, sc_tail = _dn_in_prompt(
        x_prompt, mod_p[0][:, None], ng0, w0, dn_conv_w[0], alog, dtb, sc_conv_w[0], n_heads, sc_w)
    o_p, p_delta = _gated_delta_rule(q, k, v, gb, jnp.zeros((bp, n_heads, HEAD_DIM, HEAD_DIM), F32),
                                     DN_CHUNK_PROMPT, (1, DN_PREP_ROWS), bp)
    xp1 = _dn_out(o_p, sza, yb, x_prompt, mod_p[0][:, None, 2 * d:], dng, w_out0, 256)
    p_qkv, p_sc = qkv_tail[:, SUBLANES - nq:], sc_tail[:, SUBLANES - nsc:]

    tm_of = lambda a: jnp.swapaxes(a, 0, 1)
    qs, ks, vs, gbs, szas, ybs, s_qkv_tm, s_sc_tm = _dn_in_sample(
        tm_of(x_sample), mod_s[0], ng0, w0, tm_of(state_qkv_conv[0]), tm_of(state_short_conv[0]),
        dn_conv_w[0], alog, dtb, sc_conv_w[0], n_heads, sc_w)
    cs = DN_CHUNK_SAMPLE
    t_pad = -(-t_s // cs) * cs
    pad_t = lambda a: jnp.pad(tm_of(a), ((0, 0), (0, t_pad - t_s), (0, 0)))
    o_s, s_delta = _gated_delta_rule(pad_t(qs), pad_t(ks), pad_t(vs), pad_t(gbs), state_delta[0],
                                     cs, (8, t_pad), 8)
    rows_s = bs * t_s
    flat = lambda a: a.reshape(1, rows_s, a.shape[-1])
    gate_rows = lambda m: flat(jnp.repeat(m, t_s, axis=0))
    xs1 = _dn_out(flat(o_s[:, :t_s]), flat(tm_of(szas)), flat(tm_of(ybs)), flat(x_sample),
                  gate_rows(mod_s[0][:, 2 * d:]), dng, w_out0, rows_s)
    s_qkv, s_sc = tm_of(s_qkv_tm), tm_of(s_sc_tm)

    wa = att_w_in[0].astype(BF16)
    ng1, qn, kn = norm_g[1][None], att_qn_g[0][None], att_kn_g[0][None]
    w_out1 = att_w_out[0].astype(BF16)
    n_att_heads = d // HEAD_DIM

    q1, k1, v1, sz1, kb1, vt1, kmean = _attn_in(xp1, mod_p[1][:, None], ng1, wa, qn, kn, MOBA_BLOCK, True)
    o1 = _moba_prompt(q1, kb1, vt1, kmean.reshape(bp, seq // MOBA_BLOCK, d), MOBA_HEADS_PER_STEP)
    y_prompt = _attn_out(o1, sz1, xp1, mod_p[1][:, None, 2 * d:], w_out1, 256)

    q1s, k1s, v1s, sz1s = _attn_in(xs1, gate_rows(mod_s[1]), ng1, wa, qn, kn, rows_s, False)
    seqs = lambda a: a.reshape(bs, t_s, n_att_heads, HEAD_DIM)
    o1s = _moba_decode(seqs(q1s), seqs(k1s), seqs(v1s), cache_k, cache_v, page_table, 0, DECODE_BLOCKS_PER_STEP)
    y_sample = _attn_out(flat(o1s.reshape(bs, t_s, d)), sz1s, xs1, gate_rows(mod_s[1][:, 2 * d:]), w_out1, rows_s)

    heads = lambda a, n: a.reshape(1, n, -1, n_att_heads, HEAD_DIM)
    return (y_prompt, y_sample.reshape(bs, t_s, d),
            p_delta[None], p_qkv[None], p_sc[None], heads(k1, bp), heads(v1, bp),
            s_delta[None], s_qkv[None], s_sc[None], heads(k1s, bs), heads(v1s, bs))
```

```python
import functools

import jax
import jax.numpy as jnp
from jax import lax
from jax.experimental import pallas as pl
from jax.experimental.pallas import tpu as pltpu

F32 = jnp.float32
BF16 = jnp.bfloat16
HI = lax.Precision.HIGHEST

EPS = 1e-6
LANES = 128
SUBLANES = 8
SUBLANES_BF16 = 16
HEAD_DIM = 128
DN_CHUNK_PROMPT = 64
DN_CHUNK_SAMPLE = 16
DN_SOLVE_BLOCK = 16
MOBA_BLOCK = 256
MOBA_TOPK = 3
DECODE_VMEM_RESERVE = 12 * 1024 * 1024
MOBA_KV_GROUP = 4
DN_PREP_ROWS = 512
PROMPT_ROW_TILE = 512
NEG = -1e30
DN_GRAM_PASSES = 1
DN_SOLVE_PASSES = 1
DN_SCAN_PASSES = 1
VMEM_LIMIT = 48 * 1024 * 1024
MOBA_VMEM_LIMIT = 58 * 1024 * 1024


def _cparams(sem, vmem_limit=VMEM_LIMIT):
    return pltpu.CompilerParams(dimension_semantics=sem, vmem_limit_bytes=vmem_limit)


def _silu(x):
    return x * jax.nn.sigmoid(x)


def _softplus(x):
    return jnp.maximum(x, 0.0) + jnp.log1p(jnp.exp(-jnp.abs(x)))


def _rms(x, g):
    return x * lax.rsqrt(jnp.mean(x * x, axis=-1, keepdims=True) + EPS) * g


def _modulate(x, norm_g, mod):
    d = x.shape[-1]
    return _rms(x, norm_g) * (1.0 + mod[:, d:2 * d]) + mod[:, :d]


def _dot_nt(a, b, precision=None):
    return lax.dot_general(a, b, (((1,), (1,)), ((), ())), precision=precision,
                           preferred_element_type=F32)


def _dot_tn(a, b, precision=None):
    return lax.dot_general(a, b, (((0,), (0,)), ((), ())), precision=precision,
                           preferred_element_type=F32)


def _adaln_kernel(c_ref, w_ref, b_ref, o_ref):
    o_ref[0] = jnp.dot(c_ref[...], w_ref[0], preferred_element_type=F32) + b_ref[0]


def _adaln(c_all, ada_w, ada_b):
    n_layers, d, d3 = ada_w.shape
    n = c_all.shape[0]
    tn = d
    return pl.pallas_call(
        _adaln_kernel,
        grid=(n_layers, d3 // tn),
        in_specs=[pl.BlockSpec((n, d), lambda l, j: (0, 0)),
                  pl.BlockSpec((1, d, tn), lambda l, j: (l, 0, j)),
                  pl.BlockSpec((1, 1, tn), lambda l, j: (l, 0, j))],
        out_specs=pl.BlockSpec((1, n, tn), lambda l, j: (l, 0, j)),
        out_shape=jax.ShapeDtypeStruct((n_layers, n, d3), F32),
        compiler_params=_cparams(("parallel", "parallel")),
        name="adaln_mod",
    )(c_all, ada_w, ada_b.reshape(n_layers, 1, d3))


def _dn_activations(qkv, ab, alog, dtb, n_heads):
    w = n_heads * HEAD_DIM
    act = _silu(qkv)
    qs, ks = [], []
    for h in range(n_heads):
        qh = act[:, h * HEAD_DIM:(h + 1) * HEAD_DIM]
        kh = act[:, w + h * HEAD_DIM:w + (h + 1) * HEAD_DIM]
        qs.append(qh * lax.rsqrt(jnp.sum(qh * qh, axis=-1, keepdims=True) + EPS) * (HEAD_DIM ** -0.5))
        ks.append(kh * lax.rsqrt(jnp.sum(kh * kh, axis=-1, keepdims=True) + EPS))
    q = jnp.concatenate(qs, axis=-1)
    k = jnp.concatenate(ks, axis=-1)
    v = act[:, 2 * w:3 * w]
    lane = lax.broadcasted_iota(jnp.int32, ab.shape, 1)
    g = -jnp.exp(alog) * _softplus(ab + dtb)
    beta = jax.nn.sigmoid(ab)
    gb = jnp.where(lane < n_heads, g, jnp.where(lane < 2 * n_heads, beta, 0.0))
    return q, k, v, gb


def _shift_rows(cur, prev, s):
    ext = jnp.concatenate([prev, cur], axis=0)
    return ext[SUBLANES - s:SUBLANES - s + cur.shape[0]]


def _conv_rows(cur, prev, w):
    width = w.shape[0]
    y = cur * w[width - 1:width]
    for s in range(1, width):
        y = y + _shift_rows(cur, prev, s) * w[width - 1 - s:width - s]
    return y


def _dn_in_prompt_kernel(x_ref, mod_ref, ng_ref, w_ref, cw_ref, alog_ref, dtb_ref, scw_ref,
                         q_ref, k_ref, v_ref, gb_ref, sza_ref, yb_ref, qkvt_ref, sct_ref,
                         prev_qkv, prev_sc, *, n_heads, sc_w):
    i = pl.program_id(1)
    w = n_heads * HEAD_DIM
    o_z, o_ab, o_sc = 3 * w, 4 * w, 4 * w + LANES

    @pl.when(i == 0)
    def _():
        prev_qkv[...] = jnp.zeros_like(prev_qkv)
        prev_sc[...] = jnp.zeros_like(prev_sc)

    h = _modulate(x_ref[0], ng_ref[...], mod_ref[0]).astype(BF16)
    raw = jnp.dot(h, w_ref[:, 0:o_z], preferred_element_type=F32)
    qkv = _conv_rows(raw, prev_qkv[...], cw_ref[...])
    ab = jnp.dot(h, w_ref[:, o_ab:o_sc], preferred_element_type=F32)
    q, k, v, gb = _dn_activations(qkv, ab, alog_ref[...], dtb_ref[...], n_heads)
    q_ref[0], k_ref[0], v_ref[0], gb_ref[0] = q, k, v, gb
    prev_qkv[...] = raw[-SUBLANES:]
    qkvt_ref[0] = raw[-SUBLANES:]

    sza_ref[0] = _silu(jnp.dot(h, w_ref[:, o_z:o_ab], preferred_element_type=F32)).astype(BF16)

    sc = jnp.dot(h, w_ref[:, o_sc:o_sc + 4 * sc_w], preferred_element_type=F32)
    cx = sc[:, sc_w:2 * sc_w] * sc[:, 2 * sc_w:3 * sc_w]
    cv = _conv_rows(cx, prev_sc[...], scw_ref[...])
    yb_ref[0] = (sc[:, 0:sc_w] * cv * _silu(sc[:, 3 * sc_w:4 * sc_w])).astype(BF16)
    prev_sc[...] = cx[-SUBLANES:]
    sct_ref[0] = cx[-SUBLANES:]


def _dn_in_prompt(x, mod, norm_g, w0, conv_w, alog, dtb, sc_conv_w, n_heads, sc_w):
    n, t, d = x.shape
    tm = PROMPT_ROW_TILE
    w = n_heads * HEAD_DIM
    wtot = w0.shape[1]
    row = lambda c: pl.BlockSpec((1, tm, c), lambda b, i: (b, i, 0))
    full = lambda a: pl.BlockSpec(a.shape, lambda b, i: (0,) * a.ndim)
    tail = lambda c: pl.BlockSpec((1, SUBLANES, c), lambda b, i: (b, 0, 0))
    outs = [jax.ShapeDtypeStruct((n, t, w), F32)] * 3 + [
        jax.ShapeDtypeStruct((n, t, LANES), F32), jax.ShapeDtypeStruct((n, t, w), BF16),
        jax.ShapeDtypeStruct((n, t, sc_w), BF16),
        jax.ShapeDtypeStruct((n, SUBLANES, 3 * w), F32), jax.ShapeDtypeStruct((n, SUBLANES, sc_w), F32)]
    return pl.pallas_call(
        functools.partial(_dn_in_prompt_kernel, n_heads=n_heads, sc_w=sc_w),
        grid=(n, t // tm),
        in_specs=[row(d), pl.BlockSpec((1, 1, 3 * d), lambda b, i: (b, 0, 0)), full(norm_g),
                  pl.BlockSpec((d, wtot), lambda b, i: (0, 0)), full(conv_w), full(alog), full(dtb),
                  full(sc_conv_w)],
        out_specs=[row(w), row(w), row(w), row(LANES), row(w), row(sc_w), tail(3 * w), tail(sc_w)],
        out_shape=outs,
        scratch_shapes=[pltpu.VMEM((SUBLANES, 3 * w), F32), pltpu.VMEM((SUBLANES, sc_w), F32)],
        compiler_params=_cparams(("parallel", "arbitrary")),
        name="dn_in_prompt",
    )(x, mod, norm_g, w0, conv_w, alog, dtb, sc_conv_w)


def _dn_in_sample_kernel(x_ref, mod_ref, ng_ref, w_ref, sq_ref, ssc_ref, cw_ref, alog_ref, dtb_ref,
                         scw_ref, q_ref, k_ref, v_ref, gb_ref, sza_ref, yb_ref, nq_ref, nsc_ref,
                         *, n_heads, sc_w):
    t_len = x_ref.shape[0]
    w = n_heads * HEAD_DIM
    o_z, o_ab, o_sc = 3 * w, 4 * w, 4 * w + LANES
    cw, scw = cw_ref[...], scw_ref[...]
    nq, nsc = cw.shape[0] - 1, scw.shape[0] - 1
    xq = [sq_ref[j] for j in range(nq)]
    xsc = [ssc_ref[j] for j in range(nsc)]
    hs, sc_all = [], []
    for t in range(t_len):
        h = _modulate(x_ref[t], ng_ref[...], mod_ref[...]).astype(BF16)
        hs.append(h)
        xq.append(jnp.dot(h, w_ref[:, 0:o_z], preferred_element_type=F32))
        sc = jnp.dot(h, w_ref[:, o_sc:o_sc + 4 * sc_w], preferred_element_type=F32)
        sc_all.append(sc)
        xsc.append(sc[:, sc_w:2 * sc_w] * sc[:, 2 * sc_w:3 * sc_w])
    for t in range(t_len):
        qkv = sum(xq[t + j] * cw[j:j + 1] for j in range(nq + 1))
        ab = jnp.dot(hs[t], w_ref[:, o_ab:o_sc], preferred_element_type=F32)
        q, k, v, gb = _dn_activations(qkv, ab, alog_ref[...], dtb_ref[...], n_heads)
        q_ref[t], k_ref[t], v_ref[t], gb_ref[t] = q, k, v, gb
        sza_ref[t] = _silu(jnp.dot(hs[t], w_ref[:, o_z:o_ab], preferred_element_type=F32)).astype(BF16)
        cv = sum(xsc[t + j] * scw[j:j + 1] for j in range(nsc + 1))
        sc = sc_all[t]
        yb_ref[t] = (sc[:, 0:sc_w] * cv * _silu(sc[:, 3 * sc_w:4 * sc_w])).astype(BF16)
    for j in range(nq):
        nq_ref[j] = xq[t_len + j]
    for j in range(nsc):
        nsc_ref[j] = xsc[t_len + j]


def _dn_in_sample(x_tm, mod, norm_g, w0, s_qkv_tm, s_sc_tm, conv_w, alog, dtb, sc_conv_w, n_heads, sc_w):
    t, n, d = x_tm.shape
    w = n_heads * HEAD_DIM
    gs = min(n, 128)
    wtot = w0.shape[1]
    tm3 = lambda r, c: pl.BlockSpec((r, gs, c), lambda g: (0, g, 0))
    full = lambda a: pl.BlockSpec(a.shape, lambda g: (0,) * a.ndim)
    nq, nsc = conv_w.shape[0] - 1, sc_conv_w.shape[0] - 1
    outs = [jax.ShapeDtypeStruct((t, n, w), F32)] * 3 + [
        jax.ShapeDtypeStruct((t, n, LANES), F32), jax.ShapeDtypeStruct((t, n, w), BF16),
        jax.ShapeDtypeStruct((t, n, sc_w), BF16),
        jax.ShapeDtypeStruct((nq, n, 3 * w), F32), jax.ShapeDtypeStruct((nsc, n, sc_w), F32)]
    return pl.pallas_call(
        functools.partial(_dn_in_sample_kernel, n_heads=n_heads, sc_w=sc_w),
        grid=(n // gs,),
        in_specs=[tm3(t, d), pl.BlockSpec((gs, 3 * d), lambda g: (g, 0)), full(norm_g),
                  pl.BlockSpec((d, wtot), lambda g: (0, 0)), tm3(nq, 3 * w), tm3(nsc, sc_w),
                  full(conv_w), full(alog), full(dtb), full(sc_conv_w)],
        out_specs=[tm3(t, w), tm3(t, w), tm3(t, w), tm3(t, LANES), tm3(t, w), tm3(t, sc_w),
                   tm3(nq, 3 * w), tm3(nsc, sc_w)],
        out_shape=outs,
        compiler_params=_cparams(("parallel",)),
        name="dn_in_sample",
    )(x_tm, mod, norm_g, w0, s_qkv_tm, s_sc_tm, conv_w, alog, dtb, sc_conv_w)


def _bf16_parts(x, n):
    parts, r = [], x
    for i in range(n):
        p = r.astype(BF16)
        parts.append(p)
        if i + 1 < n:
            r = r - p.astype(F32)
    return parts


def _mm(dot, a, b, passes):
    if passes == 1:
        return dot(a.astype(BF16), b.astype(BF16))
    a_hi, a_lo = _bf16_parts(a, 2)
    b_hi, b_lo = _bf16_parts(b, 2)
    return dot(a_hi, b_hi) + (dot(a_hi, b_lo) + dot(a_lo, b_hi))


def _mm_exact01(dot, sel, x):
    hi, mid, lo = (p.astype(sel.dtype) for p in _bf16_parts(x, 3))
    return dot(sel, hi) + (dot(sel, mid) + dot(sel, lo))


def _bdot(a, b):
    return jnp.einsum("bij,bjk->bik", a, b, preferred_element_type=F32)


def _bdot_nt(a, b):
    return jnp.einsum("bik,bjk->bij", a, b, preferred_element_type=F32)


def _dot(a, b):
    return jnp.dot(a, b, preferred_element_type=F32)


def _neumann_inverse(m, order, eye):
    p = eye - m
    mk, k = m, 1
    while 2 * k < order:
        mk = _mm(_bdot, mk, mk, DN_SOLVE_PASSES)
        p = p + _mm(_bdot, p, mk, DN_SOLVE_PASSES)
        k *= 2
    return p


def _unit_lower_inverse(m, c):
    row = lax.broadcasted_iota(jnp.int32, (c, c), 0)
    col = lax.broadcasted_iota(jnp.int32, (c, c), 1)
    eye = (row == col).astype(F32)
    if c <= DN_SOLVE_BLOCK:
        return _neumann_inverse(m, c, eye)
    on_diag = (row // DN_SOLVE_BLOCK) == (col // DN_SOLVE_BLOCK)
    m_diag = jnp.where(on_diag, m, 0.0)
    d_inv = _neumann_inverse(m_diag, DN_SOLVE_BLOCK, eye)
    n_off = _mm(_bdot, d_inv, m - m_diag, DN_SOLVE_PASSES)
    return _mm(_bdot, _neumann_inverse(n_off, c // DN_SOLVE_BLOCK, eye), d_inv, DN_SOLVE_PASSES)


def _delta_prep_kernel(q_ref, k_ref, v_ref, gb_ref, u_ref, w_ref, qg_ref, kd_ref, el_ref, a_ref, *, c):
    g_dim, r_dim, wq = q_ref.shape
    n_heads = wq // HEAD_DIM
    rows = g_dim * r_dim
    nb = rows // c
    row = lax.broadcasted_iota(jnp.int32, (c, c), 0)
    col = lax.broadcasted_iota(jnp.int32, (c, c), 1)
    gb2 = gb_ref[...].reshape(rows, LANES)
    tril = jnp.broadcast_to((row >= col).astype(BF16), (nb, c, c))
    gc3 = _mm_exact01(_bdot, tril, gb2.reshape(nb, c, LANES))
    gc2 = gc3.reshape(rows, LANES)
    gt2 = _mm_exact01(_bdot, jnp.ones((nb, c, c), BF16), gb2.reshape(nb, c, LANES)).reshape(rows, LANES)
    hs = lambda h: slice(h * HEAD_DIM, (h + 1) * HEAD_DIM)
    stack = lambda f: jnp.concatenate([f(h) for h in range(n_heads)], axis=0)
    gch = stack(lambda h: jnp.broadcast_to(gc2[:, h:h + 1], (rows, LANES)).reshape(nb, c, LANES))
    beta = stack(lambda h: jnp.broadcast_to(gb2[:, n_heads + h:n_heads + h + 1], (rows, LANES)).reshape(nb, c, LANES))
    diag_gc = jnp.where(row == col, gch[:, :, :c], 0.0)
    gc_cols = _mm_exact01(_bdot, jnp.ones(diag_gc.shape, BF16), diag_gc)
    gc_last = stack(lambda h: jnp.broadcast_to(gt2[:, h:h + 1], (rows, LANES)).reshape(nb, c, LANES))
    decay = jnp.where(row >= col, jnp.exp(gch[:, :, :c] - gc_cols), 0.0)
    qh = stack(lambda h: q_ref[:, :, hs(h)].reshape(nb, c, HEAD_DIM))
    kh = stack(lambda h: k_ref[:, :, hs(h)].reshape(nb, c, HEAD_DIM))
    vh = stack(lambda h: v_ref[:, :, hs(h)].reshape(nb, c, HEAD_DIM))
    kb = kh * beta
    m = jnp.where(row > col, _mm(_bdot_nt, kb, kh, DN_GRAM_PASSES) * decay, 0.0)
    a = _mm(_bdot_nt, qh, kh, DN_GRAM_PASSES) * decay
    t_inv = _unit_lower_inverse(m, c)
    egc = jnp.exp(gch)
    sol = _mm(_bdot, t_inv, jnp.concatenate([vh * beta, kb * egc], axis=-1), DN_SOLVE_PASSES)
    qg, kd, el = qh * egc, kh * jnp.exp(gc_last - gch), jnp.exp(gc_last)
    for h in range(n_heads):
        of_head = lambda x: x[h * nb:(h + 1) * nb].reshape(g_dim, r_dim, x.shape[-1])
        u_ref[:, :, hs(h)] = of_head(sol[:, :, :HEAD_DIM])
        el_ref[:, :, hs(h)] = of_head(el)
        w_ref[:, :, hs(h)] = of_head(sol[:, :, HEAD_DIM:]).astype(w_ref.dtype)
        qg_ref[:, :, hs(h)] = of_head(qg).astype(qg_ref.dtype)
        kd_ref[:, :, hs(h)] = of_head(kd).astype(kd_ref.dtype)
        a_ref[:, :, h * c:(h + 1) * c] = of_head(a).astype(a_ref.dtype)


def _delta_prep(q, k, v, gb, c, g_dim, r_dim):
    n, t, wq = q.shape
    n_heads = wq // HEAD_DIM
    blk = lambda cc: pl.BlockSpec((g_dim, r_dim, cc), lambda b, i: (b, i, 0))
    operand = BF16 if DN_SCAN_PASSES == 1 else F32
    outs = ([jax.ShapeDtypeStruct((n, t, wq), F32)] + [jax.ShapeDtypeStruct((n, t, wq), operand)] * 3
            + [jax.ShapeDtypeStruct((n, t, wq), F32), jax.ShapeDtypeStruct((n, t, n_heads * c), operand)])
    return pl.pallas_call(
        functools.partial(_delta_prep_kernel, c=c),
        grid=(n // g_dim, t // r_dim),
        in_specs=[blk(wq), blk(wq), blk(wq), blk(LANES)],
        out_specs=[blk(wq)] * 5 + [blk(n_heads * c)],
        out_shape=outs,
        compiler_params=_cparams(("parallel", "parallel")),
        name="delta_prep",
    )(q, k, v, gb)


def _delta_scan_kernel(u_ref, w_ref, qg_ref, kd_ref, el_ref, a_ref, s0_ref, o_ref, s_ref, *, c):
    @pl.when(pl.program_id(1) == 0)
    def _():
        s_ref[...] = s0_ref[...]

    g_dim, _, wq = u_ref.shape
    chains = [(g, h, slice(h * HEAD_DIM, (h + 1) * HEAD_DIM)) for g in range(g_dim) for h in range(wq // HEAD_DIM)]
    states = [s_ref[g, h] for g, h, _ in chains]
    v_new = [u_ref[g, :, sl] - _mm(_dot, w_ref[g, :, sl], s, DN_SCAN_PASSES)
             for (g, _, sl), s in zip(chains, states)]
    o_inter = [_mm(_dot, qg_ref[g, :, sl], s, DN_SCAN_PASSES) for (g, _, sl), s in zip(chains, states)]
    for (g, h, sl), s, vn, oi in zip(chains, states, v_new, o_inter):
        o_ref[g, :, sl] = oi + _mm(_dot, a_ref[g, :, h * c:(h + 1) * c], vn, DN_SCAN_PASSES)
    for (g, h, sl), s, vn in zip(chains, states, v_new):
        s_ref[g, h] = s * el_ref[g, 0:1, sl] + _mm(_dot_tn, kd_ref[g, :, sl], vn, DN_SCAN_PASSES)


def _delta_scan(u, w, qg, kd, el, a, s0, c, g_dim):
    n, t, wq = u.shape
    n_heads = wq // HEAD_DIM
    blk = lambda cc: pl.BlockSpec((g_dim, c, cc), lambda b, i: (b, i, 0))
    st = pl.BlockSpec((g_dim, n_heads, HEAD_DIM, HEAD_DIM), lambda b, i: (b, 0, 0, 0))
    return pl.pallas_call(
        functools.partial(_delta_scan_kernel, c=c),
        grid=(n // g_dim, t // c),
        in_specs=[blk(wq)] * 5 + [blk(n_heads * c), st],
        out_specs=[blk(wq), st],
        out_shape=[jax.ShapeDtypeStruct((n, t, wq), F32),
                   jax.ShapeDtypeStruct((n, n_heads, HEAD_DIM, HEAD_DIM), F32)],
        compiler_params=_cparams(("parallel", "arbitrary")),
        name="delta_scan",
    )(u, w, qg, kd, el, a, s0)


def _gated_delta_rule(q, k, v, gb, s0, c, prep_block, scan_seqs):
    u, w, qg, kd, el, a = _delta_prep(q, k, v, gb, c, *prep_block)
    return _delta_scan(u, w, qg, kd, el, a, s0, c, scan_seqs)


def _dn_out_kernel(o_ref, sza_ref, yb_ref, x_ref, gate_ref, ng_ref, w_ref, y_ref):
    o = o_ref[0]
    wa = o.shape[-1]
    ya = jnp.concatenate([_rms(o[:, h * HEAD_DIM:(h + 1) * HEAD_DIM], ng_ref[...])
                          for h in range(wa // HEAD_DIM)], axis=-1) * sza_ref[0]
    out = (jnp.dot(ya.astype(BF16), w_ref[0:wa, :], preferred_element_type=F32)
           + jnp.dot(yb_ref[0].astype(BF16), w_ref[wa:, :], preferred_element_type=F32))
    y_ref[0] = x_ref[0] + gate_ref[0] * out


def _dn_out(o, sza, yb, x, gate, dn_norm_g, w_out, tm):
    n, t, d = x.shape
    wa, wb = o.shape[-1], yb.shape[-1]
    row = lambda c: pl.BlockSpec((1, tm, c), lambda b, i: (b, i, 0))
    gate_rows = gate.shape[1]
    gspec = (pl.BlockSpec((1, 1, d), lambda b, i: (b, 0, 0)) if gate_rows == 1
             else pl.BlockSpec((1, tm, d), lambda b, i: (b, i, 0)))
    return pl.pallas_call(
        _dn_out_kernel,
        grid=(n, t // tm),
        in_specs=[row(wa), row(wa), row(wb), row(d), gspec,
                  pl.BlockSpec(dn_norm_g.shape, lambda b, i: (0, 0)),
                  pl.BlockSpec(w_out.shape, lambda b, i: (0, 0))],
        out_specs=row(d),
        out_shape=jax.ShapeDtypeStruct((n, t, d), F32),
        compiler_params=_cparams(("parallel", "parallel")),
        name="dn_out",
    )(o, sza, yb, x, gate, dn_norm_g, w_out)


def _attn_in_kernel(x_ref, mod_ref, ng_ref, w_ref, qn_ref, kn_ref, q_ref, k_ref, v_ref, sz_ref, *prompt_refs):
    d = x_ref.shape[-1]
    h = _modulate(x_ref[0], ng_ref[...], mod_ref[0]).astype(BF16)
    heads = range(d // HEAD_DIM)
    qr = jnp.dot(h, w_ref[:, 0:d], preferred_element_type=F32)
    q_ref[0] = jnp.concatenate([_rms(qr[:, i * HEAD_DIM:(i + 1) * HEAD_DIM], qn_ref[...]) for i in heads], axis=-1)
    kr = jnp.dot(h, w_ref[:, d:2 * d], preferred_element_type=F32)
    k = jnp.concatenate([_rms(kr[:, i * HEAD_DIM:(i + 1) * HEAD_DIM], kn_ref[...]) for i in heads], axis=-1)
    k_ref[0] = k
    v = jnp.dot(h, w_ref[:, 2 * d:3 * d], preferred_element_type=F32)
    v_ref[0] = v
    sz_ref[0] = _silu(jnp.dot(h, w_ref[:, 3 * d:4 * d], preferred_element_type=F32)).astype(BF16)
    if prompt_refs:
        kb_ref, vt_ref, kmean_ref = prompt_refs
        kb_ref[0] = k.astype(BF16)
        vt_ref[0] = v.T.astype(BF16)
        kmean_ref[0, 0] = jnp.mean(k, axis=0, keepdims=True)


def _attn_in(x, mod, norm_g, w_in, qn_g, kn_g, tm, for_prompt):
    n, t, d = x.shape
    row = lambda: pl.BlockSpec((1, tm, d), lambda b, i: (b, i, 0))
    mod_rows = mod.shape[1]
    mspec = (pl.BlockSpec((1, 1, 3 * d), lambda b, i: (b, 0, 0)) if mod_rows == 1
             else pl.BlockSpec((1, tm, 3 * d), lambda b, i: (b, i, 0)))
    full = lambda a: pl.BlockSpec(a.shape, lambda b, i: (0,) * a.ndim)
    out_specs = [row()] * 4
    outs = [jax.ShapeDtypeStruct((n, t, d), F32)] * 3 + [jax.ShapeDtypeStruct((n, t, d), BF16)]
    if for_prompt:
        assert tm == MOBA_BLOCK
        out_specs += [row(), pl.BlockSpec((1, d, tm), lambda b, i: (b, 0, i)),
                      pl.BlockSpec((1, 1, 1, d), lambda b, i: (b, i, 0, 0))]
        outs += [jax.ShapeDtypeStruct((n, t, d), BF16), jax.ShapeDtypeStruct((n, d, t), BF16),
                 jax.ShapeDtypeStruct((n, t // tm, 1, d), F32)]
    return pl.pallas_call(
        _attn_in_kernel,
        grid=(n, t // tm),
        in_specs=[row(), mspec, full(norm_g), full(w_in), full(qn_g), full(kn_g)],
        out_specs=out_specs,
        out_shape=outs,
        compiler_params=_cparams(("parallel", "parallel")),
        name="attn_in",
    )(x, mod, norm_g, w_in, qn_g, kn_g)


def _topk_select(gate, cand, n_blocks, axis):
    blk = lax.broadcasted_iota(jnp.int32, gate.shape, axis)
    gm = jnp.where(cand, gate, -jnp.inf)
    rank = jnp.zeros(gate.shape, jnp.int32)
    for b in range(n_blocks):
        other = gm[b:b + 1, :] if axis == 0 else gm[:, b:b + 1]
        rank = rank + ((other > gm) | ((other == gm) & (b < blk))).astype(jnp.int32)
    return cand & (rank < MOBA_TOPK)


def _moba_prompt_kernel(q_ref, k_ref, vt_ref, kmean_ref, sz_ref, x_ref, gate_ref, w_ref, y_ref, acc_scr, sel_scr,
                        *, group):
    i = pl.program_id(1)
    blk = q_ref.shape[1]
    n_blocks = kmean_ref.shape[1]
    heads = range(q_ref.shape[2] // HEAD_DIM)
    hs = lambda h: slice(h * HEAD_DIM, (h + 1) * HEAD_DIM)
    start = pl.multiple_of(i * blk, blk)
    causal = (lax.broadcasted_iota(jnp.int32, (blk, blk), 0) <= lax.broadcasted_iota(jnp.int32, (blk, blk), 1))

    def values_and_ones(h, st, n_keys):
        return jnp.concatenate([vt_ref[0, hs(h), pl.ds(st, n_keys)], jnp.ones((SUBLANES_BF16, n_keys), BF16)], axis=0)

    qfs = [q_ref[0, :, hs(h)] for h in heads]
    qbs = [(qf * (HEAD_DIM ** -0.5)).astype(BF16) for qf in qfs]
    s_own = [_dot_nt(k_ref[0, pl.ds(start, blk), hs(h)], qbs[h]).astype(BF16) for h in heads]
    gates = [_dot_nt(kmean_ref[0, :, hs(h)], qfs[h], precision=HI) for h in heads]
    m0, p_own = [], []
    for h in heads:
        s = jnp.where(causal, s_own[h], NEG)
        m = jnp.max(s, axis=0, keepdims=True)
        m0.append(m.astype(F32))
        p_own.append(jnp.exp(s - m))
    for h in heads:
        acc_scr[h] = jnp.dot(values_and_ones(h, start, blk), p_own[h], preferred_element_type=F32)
        cand = lax.broadcasted_iota(jnp.int32, gates[h].shape, 0) < i
        sel_scr[h] = _topk_select(gates[h], cand, n_blocks, 0).astype(F32)

    def past_group(jg, ms):
        st = pl.multiple_of(jg * (group * blk), group * blk)
        sgs = [_dot_nt(k_ref[0, pl.ds(st, group * blk), hs(h)], qbs[h]).astype(BF16) for h in heads]
        new_m, pjs = [], []
        for h in heads:
            sj = [jnp.where(sel_scr[h, pl.ds(jg * group + g, 1), :] > 0.5, sgs[h][g * blk:(g + 1) * blk], NEG)
                  for g in range(group)]
            m_blk = functools.reduce(jnp.maximum, [jnp.max(x, axis=0, keepdims=True) for x in sj])
            m_new = jnp.maximum(ms[h], m_blk.astype(F32))
            m_b16 = m_new.astype(BF16)
            pjs.append(jnp.concatenate([jnp.exp(x - m_b16) for x in sj], axis=0))
            new_m.append(m_new)
        for h in heads:
            acc_scr[h] = (jnp.exp(ms[h] - new_m[h]) * acc_scr[h]
                          + jnp.dot(values_and_ones(h, st, group * blk), pjs[h], preferred_element_type=F32))
        return tuple(new_m)

    lax.fori_loop(0, lax.div(i + (group - 1), group), past_group, tuple(m0))
    o = jnp.concatenate([(acc_scr[h][:HEAD_DIM] / acc_scr[h][HEAD_DIM:HEAD_DIM + 1]).T for h in heads], axis=1)
    out = jnp.dot((o * sz_ref[0]).astype(BF16), w_ref[...], preferred_element_type=F32)
    y_ref[0] = x_ref[0] + gate_ref[0] * out


def _moba_prompt(q, kb, vt, kmean, sz, x, gate, w_out):
    n, t, d = q.shape
    n_blocks = t // MOBA_BLOCK
    n_heads = d // HEAD_DIM
    group = max(g for g in range(1, MOBA_KV_GROUP + 1) if n_blocks % g == 0)
    row = pl.BlockSpec((1, MOBA_BLOCK, d), lambda b, i: (b, i, 0))
    once = dict(pipeline_mode=pl.Buffered(1))
    return pl.pallas_call(
        functools.partial(_moba_prompt_kernel, group=group),
        grid=(n, n_blocks),
        in_specs=[row, pl.BlockSpec((1, t, d), lambda b, i: (b, 0, 0), **once),
                  pl.BlockSpec((1, d, t), lambda b, i: (b, 0, 0), **once),
                  pl.BlockSpec((1, n_blocks, d), lambda b, i: (b, 0, 0)),
                  row, row, pl.BlockSpec((1, 1, d), lambda b, i: (b, 0, 0)),
                  pl.BlockSpec((d, d), lambda b, i: (0, 0), **once)],
        out_specs=row,
        out_shape=jax.ShapeDtypeStruct((n, t, d), F32),
        scratch_shapes=[pltpu.VMEM((n_heads, HEAD_DIM + SUBLANES_BF16, MOBA_BLOCK), F32),
                        pltpu.VMEM((n_heads, n_blocks, MOBA_BLOCK), F32)],
        compiler_params=_cparams(("parallel", "arbitrary"), MOBA_VMEM_LIMIT),
        name="moba_prompt",
    )(q, kb, vt, kmean, sz, x, gate, w_out)


def _moba_decode_kernel(pt_ref, q_ref, kn_ref, vn_ref, *refs, pages_per_block):
    del pt_ref
    n_pages = (len(refs) - 1) // 2
    k_refs, v_refs, o_ref = refs[:n_pages], refs[n_pages:2 * n_pages], refs[-1]
    nb = n_pages // pages_per_block
    t_len, n_heads = q_ref.shape[1], q_ref.shape[2]
    rows = t_len * n_heads
    page = k_refs[0].shape[2]
    cols = page * n_heads
    qall = q_ref[0].reshape(rows, HEAD_DIM)
    qs = qall * (HEAD_DIM ** -0.5)
    same_head = (lax.broadcasted_iota(jnp.int32, (rows, cols), 1) % n_heads
                 == lax.broadcasted_iota(jnp.int32, (rows, cols), 0) % n_heads)
    rep = lambda col: jnp.broadcast_to(col, (rows, LANES))
    per_row = lambda a: jnp.broadcast_to(a[None], (t_len, n_heads, HEAD_DIM)).reshape(rows, HEAD_DIM)

    scores = [_dot_nt(qs, r[0, 0].reshape(cols, HEAD_DIM)) for r in k_refs]
    m_blk, l_blk, gate, probs = [], [], [], []
    for b in range(nb):
        pages = range(b * pages_per_block, (b + 1) * pages_per_block)
        s = [jnp.where(same_head, scores[j], NEG) for j in pages]
        m_b = functools.reduce(jnp.maximum, [jnp.max(x, axis=1, keepdims=True) for x in s])
        p = [jnp.exp(x - m_b) for x in s]
        probs.append(p)
        m_blk.append(rep(m_b))
        l_blk.append(rep(sum(jnp.sum(x, axis=1, keepdims=True) for x in p)))
        kmean = sum(jnp.sum(k_refs[j][0, 0], axis=0) for j in pages) / (page * pages_per_block)
        gate.append(rep(jnp.sum(qall * per_row(kmean), axis=1, keepdims=True)))
    acc_blk = [sum(jnp.dot(x, v_refs[b * pages_per_block + j][0, 0].reshape(cols, HEAD_DIM),
                           preferred_element_type=F32) for j, x in enumerate(probs[b])) for b in range(nb)]

    sel = []
    for b in range(nb):
        beats = [(gate[o] >= gate[b]) if o < b else (gate[o] > gate[b]) for o in range(nb) if o != b]
        sel.append(sum(x.astype(jnp.int32) for x in beats) < MOBA_TOPK)
    tok = lax.broadcasted_iota(jnp.int32, (rows, LANES), 0) // n_heads
    s_own = [rep(jnp.sum(qs * per_row(kn_ref[0, j]), axis=1, keepdims=True)) for j in range(t_len)]
    m_all = functools.reduce(jnp.maximum, [jnp.where(tok >= j, s_own[j], NEG) for j in range(t_len)]
                             + [jnp.where(sel[b], m_blk[b], NEG) for b in range(nb)])
    den = jnp.zeros((rows, LANES), F32)
    num = jnp.zeros((rows, HEAD_DIM), F32)
    for j in range(t_len):
        pj = jnp.where(tok >= j, jnp.exp(s_own[j] - m_all), 0.0)
        den = den + pj
        num = num + pj * per_row(vn_ref[0, j])
    for b in range(nb):
        wb = jnp.where(sel[b], jnp.exp(m_blk[b] - m_all), 0.0)
        den = den + wb * l_blk[b]
        num = num + wb * acc_blk[b]
    o_ref[0] = (num / den).reshape(t_len, n_heads, HEAD_DIM)


def _moba_decode(q, k_new, v_new, cache_k, cache_v, page_table, layer):
    n, t_len, n_heads, _ = q.shape
    n_pages = page_table.shape[1]
    page = cache_k.shape[2]
    assert MOBA_BLOCK % page == 0 and (n_pages * page) % MOBA_BLOCK == 0 and t_len <= MOBA_BLOCK
    page_bytes = page * n_heads * HEAD_DIM * cache_k.dtype.itemsize
    assert 2 * 2 * n_pages * page_bytes <= VMEM_LIMIT - DECODE_VMEM_RESERVE
    tok = pl.BlockSpec((1, t_len, n_heads, HEAD_DIM), lambda s, pt: (s, 0, 0, 0))

    def page_spec(j):
        return pl.BlockSpec((1, 1, page, n_heads, HEAD_DIM), lambda s, pt: (layer, pt[s, j], 0, 0, 0))

    grid_spec = pltpu.PrefetchScalarGridSpec(
        num_scalar_prefetch=1,
        grid=(n,),
        in_specs=[tok, tok, tok] + [page_spec(j) for j in range(n_pages)] * 2,
        out_specs=tok)
    return pl.pallas_call(
        functools.partial(_moba_decode_kernel, pages_per_block=MOBA_BLOCK // page),
        grid_spec=grid_spec,
        out_shape=jax.ShapeDtypeStruct((n, t_len, n_heads, HEAD_DIM), F32),
        compiler_params=_cparams(("parallel",)),
        name="moba_decode",
    )(page_table, q, k_new, v_new, *([cache_k] * n_pages), *([cache_v] * n_pages))


def _attn_out_kernel(o_ref, sz_ref, x_ref, gate_ref, w_ref, y_ref):
    out = jnp.dot((o_ref[0] * sz_ref[0]).astype(BF16), w_ref[...], preferred_element_type=F32)
    y_ref[0] = x_ref[0] + gate_ref[0] * out


def _attn_out(o, sz, x, gate, w_out, tm):
    n, t, d = x.shape
    row = lambda: pl.BlockSpec((1, tm, d), lambda b, i: (b, i, 0))
    gspec = (pl.BlockSpec((1, 1, d), lambda b, i: (b, 0, 0)) if gate.shape[1] == 1 else row())
    return pl.pallas_call(
        _attn_out_kernel,
        grid=(n, t // tm),
        in_specs=[row(), row(), row(), gspec, pl.BlockSpec(w_out.shape, lambda b, i: (0, 0))],
        out_specs=row(),
        out_shape=jax.ShapeDtypeStruct((n, t, d), F32),
        compiler_params=_cparams(("parallel", "parallel")),
        name="attn_out",
    )(o, sz, x, gate, w_out)


def _pad_lanes(a):
    return jnp.pad(a, ((0, 0), (0, LANES - a.shape[-1])))


def kernel(x_prompt, x_sample, state_delta, state_qkv_conv, state_short_conv, cache_k, cache_v, page_table, c_prompt, c_sample, norm_g, ada_w, ada_b, dn_w_in, dn_conv_w, dn_a_log, dn_dt_bias, dn_norm_g, sc_conv_w, dn_w_out, att_w_in, att_qn_g, att_kn_g, att_w_out):
    bp, seq, d = x_prompt.shape
    bs, t_s, _ = x_sample.shape
    n_heads = dn_a_log.shape[-1]
    w = n_heads * HEAD_DIM
    sc_w = sc_conv_w.shape[-1]
    assert dn_w_in.shape[-1] == 4 * w + 2 * n_heads + 4 * sc_w and state_delta.shape[-2:] == (HEAD_DIM, HEAD_DIM)

    mod = _adaln(jnp.concatenate([c_prompt, c_sample], axis=0), ada_w, ada_b)
    mod_p, mod_s = mod[:, :bp], mod[:, bp:]

    wi = dn_w_in[0]
    w0 = jnp.concatenate([wi[:, :4 * w], _pad_lanes(wi[:, 4 * w:4 * w + 2 * n_heads]),
                          wi[:, 4 * w + 2 * n_heads:]], axis=1).astype(BF16)
    alog, dtb = _pad_lanes(dn_a_log[0][None]), _pad_lanes(dn_dt_bias[0][None])
    ng0 = norm_g[0][None]
    dng = dn_norm_g[0][None]
    w_out0 = dn_w_out[0].astype(BF16)
    nq, nsc = dn_conv_w.shape[1] - 1, sc_conv_w.shape[1] - 1

    q, k, v, gb, sza, yb, qkv_tail, sc_tail = _dn_in_prompt(
        x_prompt, mod_p[0][:, None], ng0, w0, dn_conv_w[0], alog, dtb, sc_conv_w[0], n_heads, sc_w)
    o_p, p_delta = _gated_delta_rule(q, k, v, gb, jnp.zeros((bp, n_heads, HEAD_DIM, HEAD_DIM), F32),
                                     DN_CHUNK_PROMPT, (1, DN_PREP_ROWS), bp)
    xp1 = _dn_out(o_p, sza, yb, x_prompt, mod_p[0][:, None, 2 * d:], dng, w_out0, PROMPT_ROW_TILE)
    p_qkv, p_sc = qkv_tail[:, SUBLANES - nq:], sc_tail[:, SUBLANES - nsc:]

    tm_of = lambda a: jnp.swapaxes(a, 0, 1)
    qs, ks, vs, gbs, szas, ybs, s_qkv_tm, s_sc_tm = _dn_in_sample(
        tm_of(x_sample), mod_s[0], ng0, w0, tm_of(state_qkv_conv[0]), tm_of(state_short_conv[0]),
        dn_conv_w[0], alog, dtb, sc_conv_w[0], n_heads, sc_w)
    cs = DN_CHUNK_SAMPLE
    t_pad = -(-t_s // cs) * cs
    pad_t = lambda a: jnp.pad(tm_of(a), ((0, 0), (0, t_pad - t_s), (0, 0)))
    o_s, s_delta = _gated_delta_rule(pad_t(qs), pad_t(ks), pad_t(vs), pad_t(gbs), state_delta[0],
                                     cs, (8, t_pad), 8)
    rows_s = bs * t_s
    flat = lambda a: a.reshape(1, rows_s, a.shape[-1])
    gate_rows = lambda m: flat(jnp.repeat(m, t_s, axis=0))
    xs1 = _dn_out(flat(o_s[:, :t_s]), flat(tm_of(szas)), flat(tm_of(ybs)), flat(x_sample),
                  gate_rows(mod_s[0][:, 2 * d:]), dng, w_out0, rows_s)
    s_qkv, s_sc = tm_of(s_qkv_tm), tm_of(s_sc_tm)

    wa = att_w_in[0].astype(BF16)
    ng1, qn, kn = norm_g[1][None], att_qn_g[0][None], att_kn_g[0][None]
    w_out1 = att_w_out[0].astype(BF16)
    n_att_heads = d // HEAD_DIM

    q1, k1, v1, sz1, kb1, vt1, kmean = _attn_in(xp1, mod_p[1][:, None], ng1, wa, qn, kn, MOBA_BLOCK, True)
    y_prompt = _moba_prompt(q1, kb1, vt1, kmean.reshape(bp, seq // MOBA_BLOCK, d), sz1, xp1,
                            mod_p[1][:, None, 2 * d:], w_out1)

    q1s, k1s, v1s, sz1s = _attn_in(xs1, gate_rows(mod_s[1]), ng1, wa, qn, kn, rows_s, False)
    seqs = lambda a: a.reshape(bs, t_s, n_att_heads, HEAD_DIM)
    o1s = _moba_decode(seqs(q1s), seqs(k1s), seqs(v1s), cache_k, cache_v, page_table, 0)
    y_sample = _attn_out(flat(o1s.reshape(bs, t_s, d)), sz1s, xs1, gate_rows(mod_s[1][:, 2 * d:]), w_out1, rows_s)

    heads = lambda a, n: a.reshape(1, n, -1, n_att_heads, HEAD_DIM)
    return (y_prompt, y_sample.reshape(bs, t_s, d),
            p_delta[None], p_qkv[None], p_sc[None], heads(k1, bp), heads(v1, bp),
            s_delta[None], s_qkv[None], s_sc[None], heads(k1s, bs), heads(v1s, bs))
```

```python
import functools

import jax
import jax.numpy as jnp
from jax import lax
from jax.experimental import pallas as pl
from jax.experimental.pallas import tpu as pltpu

F32 = jnp.float32
BF16 = jnp.bfloat16
HI = lax.Precision.HIGHEST

EPS = 1e-6
LANES = 128
SUBLANES = 8
SUBLANES_BF16 = 16
HEAD_DIM = 128
DN_CHUNK_PROMPT = 64
DN_CHUNK_SAMPLE = 16
DN_SOLVE_BLOCK = 16
MOBA_BLOCK = 256
MOBA_TOPK = 3
DECODE_VMEM_RESERVE = 12 * 1024 * 1024
MOBA_KV_GROUP = 4
PROMPT_ROW_TILE = 512
NEG = -1e30
DN_GRAM_PASSES = 1
DN_SOLVE_PASSES = 1
DN_SCAN_PASSES = 1
VMEM_LIMIT = 48 * 1024 * 1024
MOBA_VMEM_LIMIT = 58 * 1024 * 1024


def _cparams(sem, vmem_limit=VMEM_LIMIT):
    return pltpu.CompilerParams(dimension_semantics=sem, vmem_limit_bytes=vmem_limit)


def _silu(x):
    return x * jax.nn.sigmoid(x)


def _softplus(x):
    return jnp.maximum(x, 0.0) + jnp.log1p(jnp.exp(-jnp.abs(x)))


def _rms(x, g):
    return x * lax.rsqrt(jnp.mean(x * x, axis=-1, keepdims=True) + EPS) * g


def _modulate(x, norm_g, mod):
    d = x.shape[-1]
    return _rms(x, norm_g) * (1.0 + mod[:, d:2 * d]) + mod[:, :d]


def _dot_nt(a, b, precision=None):
    return lax.dot_general(a, b, (((1,), (1,)), ((), ())), precision=precision,
                           preferred_element_type=F32)


def _dot_tn(a, b, precision=None):
    return lax.dot_general(a, b, (((0,), (0,)), ((), ())), precision=precision,
                           preferred_element_type=F32)


def _adaln_kernel(c_ref, w_ref, b_ref, o_ref):
    o_ref[0] = jnp.dot(c_ref[...], w_ref[0], preferred_element_type=F32) + b_ref[0]


def _adaln(c_all, ada_w, ada_b):
    n_layers, d, d3 = ada_w.shape
    n = c_all.shape[0]
    tn = d
    return pl.pallas_call(
        _adaln_kernel,
        grid=(n_layers, d3 // tn),
        in_specs=[pl.BlockSpec((n, d), lambda l, j: (0, 0)),
                  pl.BlockSpec((1, d, tn), lambda l, j: (l, 0, j)),
                  pl.BlockSpec((1, 1, tn), lambda l, j: (l, 0, j))],
        out_specs=pl.BlockSpec((1, n, tn), lambda l, j: (l, 0, j)),
        out_shape=jax.ShapeDtypeStruct((n_layers, n, d3), F32),
        compiler_params=_cparams(("parallel", "parallel")),
        name="adaln_mod",
    )(c_all, ada_w, ada_b.reshape(n_layers, 1, d3))


def _dn_activations(qkv, ab, alog, dtb, n_heads):
    w = n_heads * HEAD_DIM
    act = _silu(qkv)
    qs, ks = [], []
    for h in range(n_heads):
        qh = act[:, h * HEAD_DIM:(h + 1) * HEAD_DIM]
        kh = act[:, w + h * HEAD_DIM:w + (h + 1) * HEAD_DIM]
        qs.append(qh * lax.rsqrt(jnp.sum(qh * qh, axis=-1, keepdims=True) + EPS) * (HEAD_DIM ** -0.5))
        ks.append(kh * lax.rsqrt(jnp.sum(kh * kh, axis=-1, keepdims=True) + EPS))
    q = jnp.concatenate(qs, axis=-1)
    k = jnp.concatenate(ks, axis=-1)
    v = act[:, 2 * w:3 * w]
    lane = lax.broadcasted_iota(jnp.int32, ab.shape, 1)
    g = -jnp.exp(alog) * _softplus(ab + dtb)
    beta = jax.nn.sigmoid(ab)
    gb = jnp.where(lane < n_heads, g, jnp.where(lane < 2 * n_heads, beta, 0.0))
    return q, k, v, gb


def _shift_rows(cur, prev, s):
    ext = jnp.concatenate([prev, cur], axis=0)
    return ext[SUBLANES - s:SUBLANES - s + cur.shape[0]]


def _conv_rows(cur, prev, w):
    width = w.shape[0]
    y = cur * w[width - 1:width]
    for s in range(1, width):
        y = y + _shift_rows(cur, prev, s) * w[width - 1 - s:width - s]
    return y


def _dn_in_prompt_kernel(x_ref, mod_ref, ng_ref, w_ref, cw_ref, alog_ref, dtb_ref, scw_ref,
                         q_ref, k_ref, v_ref, gb_ref, sza_ref, yb_ref, qkvt_ref, sct_ref,
                         prev_qkv, prev_sc, *, n_heads, sc_w):
    i = pl.program_id(1)
    w = n_heads * HEAD_DIM
    o_z, o_ab, o_sc = 3 * w, 4 * w, 4 * w + LANES

    @pl.when(i == 0)
    def _():
        prev_qkv[...] = jnp.zeros_like(prev_qkv)
        prev_sc[...] = jnp.zeros_like(prev_sc)

    h = _modulate(x_ref[0], ng_ref[...], mod_ref[0]).astype(BF16)
    raw = jnp.dot(h, w_ref[:, 0:o_z], preferred_element_type=F32)
    qkv = _conv_rows(raw, prev_qkv[...], cw_ref[...])
    ab = jnp.dot(h, w_ref[:, o_ab:o_sc], preferred_element_type=F32)
    q, k, v, gb = _dn_activations(qkv, ab, alog_ref[...], dtb_ref[...], n_heads)
    q_ref[0], k_ref[0], v_ref[0], gb_ref[0] = q, k, v, gb
    prev_qkv[...] = raw[-SUBLANES:]
    qkvt_ref[0] = raw[-SUBLANES:]

    sza_ref[0] = _silu(jnp.dot(h, w_ref[:, o_z:o_ab], preferred_element_type=F32)).astype(BF16)

    sc = jnp.dot(h, w_ref[:, o_sc:o_sc + 4 * sc_w], preferred_element_type=F32)
    cx = sc[:, sc_w:2 * sc_w] * sc[:, 2 * sc_w:3 * sc_w]
    cv = _conv_rows(cx, prev_sc[...], scw_ref[...])
    yb_ref[0] = (sc[:, 0:sc_w] * cv * _silu(sc[:, 3 * sc_w:4 * sc_w])).astype(BF16)
    prev_sc[...] = cx[-SUBLANES:]
    sct_ref[0] = cx[-SUBLANES:]


def _dn_in_prompt(x, mod, norm_g, w0, conv_w, alog, dtb, sc_conv_w, n_heads, sc_w):
    n, t, d = x.shape
    tm = PROMPT_ROW_TILE
    w = n_heads * HEAD_DIM
    wtot = w0.shape[1]
    row = lambda c: pl.BlockSpec((1, tm, c), lambda b, i: (b, i, 0))
    full = lambda a: pl.BlockSpec(a.shape, lambda b, i: (0,) * a.ndim)
    tail = lambda c: pl.BlockSpec((1, SUBLANES, c), lambda b, i: (b, 0, 0))
    outs = [jax.ShapeDtypeStruct((n, t, w), F32)] * 3 + [
        jax.ShapeDtypeStruct((n, t, LANES), F32), jax.ShapeDtypeStruct((n, t, w), BF16),
        jax.ShapeDtypeStruct((n, t, sc_w), BF16),
        jax.ShapeDtypeStruct((n, SUBLANES, 3 * w), F32), jax.ShapeDtypeStruct((n, SUBLANES, sc_w), F32)]
    return pl.pallas_call(
        functools.partial(_dn_in_prompt_kernel, n_heads=n_heads, sc_w=sc_w),
        grid=(n, t // tm),
        in_specs=[row(d), pl.BlockSpec((1, 1, 3 * d), lambda b, i: (b, 0, 0)), full(norm_g),
                  pl.BlockSpec((d, wtot), lambda b, i: (0, 0)), full(conv_w), full(alog), full(dtb),
                  full(sc_conv_w)],
        out_specs=[row(w), row(w), row(w), row(LANES), row(w), row(sc_w), tail(3 * w), tail(sc_w)],
        out_shape=outs,
        scratch_shapes=[pltpu.VMEM((SUBLANES, 3 * w), F32), pltpu.VMEM((SUBLANES, sc_w), F32)],
        compiler_params=_cparams(("parallel", "arbitrary")),
        name="dn_in_prompt",
    )(x, mod, norm_g, w0, conv_w, alog, dtb, sc_conv_w)


def _dn_in_sample_kernel(x_ref, mod_ref, ng_ref, w_ref, sq_ref, ssc_ref, cw_ref, alog_ref, dtb_ref,
                         scw_ref, q_ref, k_ref, v_ref, gb_ref, sza_ref, yb_ref, nq_ref, nsc_ref,
                         *, n_heads, sc_w):
    t_len = x_ref.shape[0]
    w = n_heads * HEAD_DIM
    o_z, o_ab, o_sc = 3 * w, 4 * w, 4 * w + LANES
    cw, scw = cw_ref[...], scw_ref[...]
    nq, nsc = cw.shape[0] - 1, scw.shape[0] - 1
    xq = [sq_ref[j] for j in range(nq)]
    xsc = [ssc_ref[j] for j in range(nsc)]
    hs, sc_all = [], []
    for t in range(t_len):
        h = _modulate(x_ref[t], ng_ref[...], mod_ref[...]).astype(BF16)
        hs.append(h)
        xq.append(jnp.dot(h, w_ref[:, 0:o_z], preferred_element_type=F32))
        sc = jnp.dot(h, w_ref[:, o_sc:o_sc + 4 * sc_w], preferred_element_type=F32)
        sc_all.append(sc)
        xsc.append(sc[:, sc_w:2 * sc_w] * sc[:, 2 * sc_w:3 * sc_w])
    for t in range(t_len):
        qkv = sum(xq[t + j] * cw[j:j + 1] for j in range(nq + 1))
        ab = jnp.dot(hs[t], w_ref[:, o_ab:o_sc], preferred_element_type=F32)
        q, k, v, gb = _dn_activations(qkv, ab, alog_ref[...], dtb_ref[...], n_heads)
        q_ref[t], k_ref[t], v_ref[t], gb_ref[t] = q, k, v, gb
        sza_ref[t] = _silu(jnp.dot(hs[t], w_ref[:, o_z:o_ab], preferred_element_type=F32)).astype(BF16)
        cv = sum(xsc[t + j] * scw[j:j + 1] for j in range(nsc + 1))
        sc = sc_all[t]
        yb_ref[t] = (sc[:, 0:sc_w] * cv * _silu(sc[:, 3 * sc_w:4 * sc_w])).astype(BF16)
    for j in range(nq):
        nq_ref[j] = xq[t_len + j]
    for j in range(nsc):
        nsc_ref[j] = xsc[t_len + j]


def _dn_in_sample(x_tm, mod, norm_g, w0, s_qkv_tm, s_sc_tm, conv_w, alog, dtb, sc_conv_w, n_heads, sc_w):
    t, n, d = x_tm.shape
    w = n_heads * HEAD_DIM
    gs = min(n, 128)
    wtot = w0.shape[1]
    tm3 = lambda r, c: pl.BlockSpec((r, gs, c), lambda g: (0, g, 0))
    full = lambda a: pl.BlockSpec(a.shape, lambda g: (0,) * a.ndim)
    nq, nsc = conv_w.shape[0] - 1, sc_conv_w.shape[0] - 1
    outs = [jax.ShapeDtypeStruct((t, n, w), F32)] * 3 + [
        jax.ShapeDtypeStruct((t, n, LANES), F32), jax.ShapeDtypeStruct((t, n, w), BF16),
        jax.ShapeDtypeStruct((t, n, sc_w), BF16),
        jax.ShapeDtypeStruct((nq, n, 3 * w), F32), jax.ShapeDtypeStruct((nsc, n, sc_w), F32)]
    return pl.pallas_call(
        functools.partial(_dn_in_sample_kernel, n_heads=n_heads, sc_w=sc_w),
        grid=(n // gs,),
        in_specs=[tm3(t, d), pl.BlockSpec((gs, 3 * d), lambda g: (g, 0)), full(norm_g),
                  pl.BlockSpec((d, wtot), lambda g: (0, 0)), tm3(nq, 3 * w), tm3(nsc, sc_w),
                  full(conv_w), full(alog), full(dtb), full(sc_conv_w)],
        out_specs=[tm3(t, w), tm3(t, w), tm3(t, w), tm3(t, LANES), tm3(t, w), tm3(t, sc_w),
                   tm3(nq, 3 * w), tm3(nsc, sc_w)],
        out_shape=outs,
        compiler_params=_cparams(("parallel",)),
        name="dn_in_sample",
    )(x_tm, mod, norm_g, w0, s_qkv_tm, s_sc_tm, conv_w, alog, dtb, sc_conv_w)


def _bf16_parts(x, n):
    parts, r = [], x
    for i in range(n):
        p = r.astype(BF16)
        parts.append(p)
        if i + 1 < n:
            r = r - p.astype(F32)
    return parts


def _mm(dot, a, b, passes):
    if passes == 1:
        return dot(a.astype(BF16), b.astype(BF16))
    a_hi, a_lo = _bf16_parts(a, 2)
    b_hi, b_lo = _bf16_parts(b, 2)
    return dot(a_hi, b_hi) + (dot(a_hi, b_lo) + dot(a_lo, b_hi))


def _mm_exact01(dot, sel, x):
    hi, mid, lo = (p.astype(sel.dtype) for p in _bf16_parts(x, 3))
    return dot(sel, hi) + (dot(sel, mid) + dot(sel, lo))


def _bdot(a, b):
    return jnp.einsum("bij,bjk->bik", a, b, preferred_element_type=F32)


def _bdot_nt(a, b):
    return jnp.einsum("bik,bjk->bij", a, b, preferred_element_type=F32)


def _dot(a, b):
    return jnp.dot(a, b, preferred_element_type=F32)


def _neumann_inverse(m, order, eye):
    p = eye - m
    mk, k = m, 1
    while 2 * k < order:
        mk = _mm(_bdot, mk, mk, DN_SOLVE_PASSES)
        p = p + _mm(_bdot, p, mk, DN_SOLVE_PASSES)
        k *= 2
    return p


def _unit_lower_inverse(m, c):
    row = lax.broadcasted_iota(jnp.int32, (c, c), 0)
    col = lax.broadcasted_iota(jnp.int32, (c, c), 1)
    eye = (row == col).astype(F32)
    if c <= DN_SOLVE_BLOCK:
        return _neumann_inverse(m, c, eye)
    on_diag = (row // DN_SOLVE_BLOCK) == (col // DN_SOLVE_BLOCK)
    m_diag = jnp.where(on_diag, m, 0.0)
    d_inv = _neumann_inverse(m_diag, DN_SOLVE_BLOCK, eye)
    n_off = _mm(_bdot, d_inv, m - m_diag, DN_SOLVE_PASSES)
    return _mm(_bdot, _neumann_inverse(n_off, c // DN_SOLVE_BLOCK, eye), d_inv, DN_SOLVE_PASSES)


def _delta_prep_kernel(q_ref, k_ref, v_ref, gb_ref, u_ref, w_ref, qg_ref, kd_ref, el_ref, a_ref, *, c):
    g_dim, r_dim, wq = q_ref.shape
    n_heads = wq // HEAD_DIM
    rows = g_dim * r_dim
    nb = rows // c
    row = lax.broadcasted_iota(jnp.int32, (c, c), 0)
    col = lax.broadcasted_iota(jnp.int32, (c, c), 1)
    gb2 = gb_ref[...].reshape(rows, LANES)
    tril = jnp.broadcast_to((row >= col).astype(BF16), (nb, c, c))
    gc3 = _mm_exact01(_bdot, tril, gb2.reshape(nb, c, LANES))
    gc2 = gc3.reshape(rows, LANES)
    gt2 = _mm_exact01(_bdot, jnp.ones((nb, c, c), BF16), gb2.reshape(nb, c, LANES)).reshape(rows, LANES)
    hs = lambda h: slice(h * HEAD_DIM, (h + 1) * HEAD_DIM)
    stack = lambda f: jnp.concatenate([f(h) for h in range(n_heads)], axis=0)
    gch = stack(lambda h: jnp.broadcast_to(gc2[:, h:h + 1], (rows, LANES)).reshape(nb, c, LANES))
    beta = stack(lambda h: jnp.broadcast_to(gb2[:, n_heads + h:n_heads + h + 1], (rows, LANES)).reshape(nb, c, LANES))
    diag_gc = jnp.where(row == col, gch[:, :, :c], 0.0)
    gc_cols = _mm_exact01(_bdot, jnp.ones(diag_gc.shape, BF16), diag_gc)
    gc_last = stack(lambda h: jnp.broadcast_to(gt2[:, h:h + 1], (rows, LANES)).reshape(nb, c, LANES))
    decay = jnp.where(row >= col, jnp.exp(gch[:, :, :c] - gc_cols), 0.0)
    qh = stack(lambda h: q_ref[:, :, hs(h)].reshape(nb, c, HEAD_DIM))
    kh = stack(lambda h: k_ref[:, :, hs(h)].reshape(nb, c, HEAD_DIM))
    vh = stack(lambda h: v_ref[:, :, hs(h)].reshape(nb, c, HEAD_DIM))
    kb = kh * beta
    m = jnp.where(row > col, _mm(_bdot_nt, kb, kh, DN_GRAM_PASSES) * decay, 0.0)
    a = _mm(_bdot_nt, qh, kh, DN_GRAM_PASSES) * decay
    t_inv = _unit_lower_inverse(m, c)
    egc = jnp.exp(gch)
    sol = _mm(_bdot, t_inv, jnp.concatenate([vh * beta, kb * egc], axis=-1), DN_SOLVE_PASSES)
    qg, kd, el = qh * egc, kh * jnp.exp(gc_last - gch), jnp.exp(gc_last)
    for h in range(n_heads):
        of_head = lambda x: x[h * nb:(h + 1) * nb].reshape(g_dim, r_dim, x.shape[-1])
        u_ref[:, :, hs(h)] = of_head(sol[:, :, :HEAD_DIM])
        el_ref[:, :, hs(h)] = of_head(el)
        w_ref[:, :, hs(h)] = of_head(sol[:, :, HEAD_DIM:]).astype(w_ref.dtype)
        qg_ref[:, :, hs(h)] = of_head(qg).astype(qg_ref.dtype)
        kd_ref[:, :, hs(h)] = of_head(kd).astype(kd_ref.dtype)
        a_ref[:, :, h * c:(h + 1) * c] = of_head(a).astype(a_ref.dtype)


def _delta_scan_kernel(u_ref, w_ref, qg_ref, kd_ref, el_ref, a_ref, s0_ref, o_ref, s_ref, *, c):
    @pl.when(pl.program_id(1) == 0)
    def _():
        s_ref[...] = s0_ref[...]

    _delta_scan_step(u_ref, w_ref, qg_ref, kd_ref, el_ref, a_ref, o_ref, s_ref, c=c)


def _delta_one_chunk_kernel(q_ref, k_ref, v_ref, gb_ref, s0_ref, o_ref, s_ref, u, w, qg, kd, el, a, *, c):
    _delta_prep_kernel(q_ref, k_ref, v_ref, gb_ref, u, w, qg, kd, el, a, c=c)
    s_ref[...] = s0_ref[...]
    _delta_scan_step(u, w, qg, kd, el, a, o_ref, s_ref, c=c)


def _delta_scan_step(u_ref, w_ref, qg_ref, kd_ref, el_ref, a_ref, o_ref, s_ref, *, c):
    g_dim, _, wq = u_ref.shape
    chains = [(g, h, slice(h * HEAD_DIM, (h + 1) * HEAD_DIM)) for g in range(g_dim) for h in range(wq // HEAD_DIM)]
    states = [s_ref[g, h] for g, h, _ in chains]
    v_new = [u_ref[g, :, sl] - _mm(_dot, w_ref[g, :, sl], s, DN_SCAN_PASSES)
             for (g, _, sl), s in zip(chains, states)]
    o_inter = [_mm(_dot, qg_ref[g, :, sl], s, DN_SCAN_PASSES) for (g, _, sl), s in zip(chains, states)]
    for (g, h, sl), s, vn, oi in zip(chains, states, v_new, o_inter):
        o_ref[g, :, sl] = oi + _mm(_dot, a_ref[g, :, h * c:(h + 1) * c], vn, DN_SCAN_PASSES)
    for (g, h, sl), s, vn in zip(chains, states, v_new):
        s_ref[g, h] = s * el_ref[g, 0:1, sl] + _mm(_dot_tn, kd_ref[g, :, sl], vn, DN_SCAN_PASSES)


def _delta_scan(u, w, qg, kd, el, a, s0, c, g_dim):
    n, t, wq = u.shape
    n_heads = wq // HEAD_DIM
    blk = lambda cc: pl.BlockSpec((g_dim, c, cc), lambda b, i: (b, i, 0))
    st = pl.BlockSpec((g_dim, n_heads, HEAD_DIM, HEAD_DIM), lambda b, i: (b, 0, 0, 0))
    return pl.pallas_call(
        functools.partial(_delta_scan_kernel, c=c),
        grid=(n // g_dim, t // c),
        in_specs=[blk(wq)] * 5 + [blk(n_heads * c), st],
        out_specs=[blk(wq), st],
        out_shape=[jax.ShapeDtypeStruct((n, t, wq), F32),
                   jax.ShapeDtypeStruct((n, n_heads, HEAD_DIM, HEAD_DIM), F32)],
        compiler_params=_cparams(("parallel", "arbitrary")),
        name="delta_scan",
    )(u, w, qg, kd, el, a, s0)


def _delta_one_chunk(q, k, v, gb, s0, g_dim):
    n, c, wq = q.shape
    n_heads = wq // HEAD_DIM
    blk = lambda cc: pl.BlockSpec((g_dim, c, cc), lambda b: (b, 0, 0))
    st = pl.BlockSpec((g_dim, n_heads, HEAD_DIM, HEAD_DIM), lambda b: (b, 0, 0, 0))
    operand = BF16 if DN_SCAN_PASSES == 1 else F32
    tmp = lambda cc, dt: pltpu.VMEM((g_dim, c, cc), dt)
    return pl.pallas_call(
        functools.partial(_delta_one_chunk_kernel, c=c),
        grid=(n // g_dim,),
        in_specs=[blk(wq), blk(wq), blk(wq), blk(LANES), st],
        out_specs=[blk(wq), st],
        out_shape=[jax.ShapeDtypeStruct((n, c, wq), F32),
                   jax.ShapeDtypeStruct((n, n_heads, HEAD_DIM, HEAD_DIM), F32)],
        scratch_shapes=[tmp(wq, F32), tmp(wq, operand), tmp(wq, operand), tmp(wq, operand), tmp(wq, F32),
                        tmp(n_heads * c, operand)],
        compiler_params=_cparams(("parallel",)),
        name="delta_one_chunk",
    )(q, k, v, gb, s0)


def _dn_out_kernel(o_ref, sza_ref, yb_ref, x_ref, gate_ref, ng_ref, w_ref, y_ref):
    o = o_ref[0]
    wa = o.shape[-1]
    ya = jnp.concatenate([_rms(o[:, h * HEAD_DIM:(h + 1) * HEAD_DIM], ng_ref[...])
                          for h in range(wa // HEAD_DIM)], axis=-1) * sza_ref[0]
    out = (jnp.dot(ya.astype(BF16), w_ref[0:wa, :], preferred_element_type=F32)
           + jnp.dot(yb_ref[0].astype(BF16), w_ref[wa:, :], preferred_element_type=F32))
    y_ref[0] = x_ref[0] + gate_ref[0] * out


def _dn_out(o, sza, yb, x, gate, dn_norm_g, w_out, tm):
    n, t, d = x.shape
    wa, wb = o.shape[-1], yb.shape[-1]
    row = lambda c: pl.BlockSpec((1, tm, c), lambda b, i: (b, i, 0))
    gate_rows = gate.shape[1]
    gspec = (pl.BlockSpec((1, 1, d), lambda b, i: (b, 0, 0)) if gate_rows == 1
             else pl.BlockSpec((1, tm, d), lambda b, i: (b, i, 0)))
    return pl.pallas_call(
        _dn_out_kernel,
        grid=(n, t // tm),
        in_specs=[row(wa), row(wa), row(wb), row(d), gspec,
                  pl.BlockSpec(dn_norm_g.shape, lambda b, i: (0, 0)),
                  pl.BlockSpec(w_out.shape, lambda b, i: (0, 0))],
        out_specs=row(d),
        out_shape=jax.ShapeDtypeStruct((n, t, d), F32),
        compiler_params=_cparams(("parallel", "parallel")),
        name="dn_out",
    )(o, sza, yb, x, gate, dn_norm_g, w_out)


def _attn_in_kernel(x_ref, mod_ref, ng_ref, w_ref, qn_ref, kn_ref, q_ref, k_ref, v_ref, sz_ref, *prompt_refs):
    d = x_ref.shape[-1]
    h = _modulate(x_ref[0], ng_ref[...], mod_ref[0]).astype(BF16)
    heads = range(d // HEAD_DIM)
    qr = jnp.dot(h, w_ref[:, 0:d], preferred_element_type=F32)
    q_ref[0] = jnp.concatenate([_rms(qr[:, i * HEAD_DIM:(i + 1) * HEAD_DIM], qn_ref[...]) for i in heads], axis=-1)
    kr = jnp.dot(h, w_ref[:, d:2 * d], preferred_element_type=F32)
    k = jnp.concatenate([_rms(kr[:, i * HEAD_DIM:(i + 1) * HEAD_DIM], kn_ref[...]) for i in heads], axis=-1)
    k_ref[0] = k
    v = jnp.dot(h, w_ref[:, 2 * d:3 * d], preferred_element_type=F32)
    v_ref[0] = v
    sz_ref[0] = _silu(jnp.dot(h, w_ref[:, 3 * d:4 * d], preferred_element_type=F32)).astype(BF16)
    if prompt_refs:
        kb_ref, vt_ref, kmean_ref = prompt_refs
        kb_ref[0] = k.astype(BF16)
        vt_ref[0] = v.T.astype(BF16)
        kmean_ref[0, 0] = jnp.mean(k, axis=0, keepdims=True)


def _attn_in(x, mod, norm_g, w_in, qn_g, kn_g, tm, for_prompt):
    n, t, d = x.shape
    row = lambda: pl.BlockSpec((1, tm, d), lambda b, i: (b, i, 0))
    mod_rows = mod.shape[1]
    mspec = (pl.BlockSpec((1, 1, 3 * d), lambda b, i: (b, 0, 0)) if mod_rows == 1
             else pl.BlockSpec((1, tm, 3 * d), lambda b, i: (b, i, 0)))
    full = lambda a: pl.BlockSpec(a.shape, lambda b, i: (0,) * a.ndim)
    out_specs = [row()] * 4
    outs = [jax.ShapeDtypeStruct((n, t, d), F32)] * 3 + [jax.ShapeDtypeStruct((n, t, d), BF16)]
    if for_prompt:
        assert tm == MOBA_BLOCK
        out_specs += [row(), pl.BlockSpec((1, d, tm), lambda b, i: (b, 0, i)),
                      pl.BlockSpec((1, 1, 1, d), lambda b, i: (b, i, 0, 0))]
        outs += [jax.ShapeDtypeStruct((n, t, d), BF16), jax.ShapeDtypeStruct((n, d, t), BF16),
                 jax.ShapeDtypeStruct((n, t // tm, 1, d), F32)]
    return pl.pallas_call(
        _attn_in_kernel,
        grid=(n, t // tm),
        in_specs=[row(), mspec, full(norm_g), full(w_in), full(qn_g), full(kn_g)],
        out_specs=out_specs,
        out_shape=outs,
        compiler_params=_cparams(("parallel", "parallel")),
        name="attn_in",
    )(x, mod, norm_g, w_in, qn_g, kn_g)


def _topk_select(gate, cand, n_blocks, axis):
    blk = lax.broadcasted_iota(jnp.int32, gate.shape, axis)
    gm = jnp.where(cand, gate, -jnp.inf)
    rank = jnp.zeros(gate.shape, jnp.int32)
    for b in range(n_blocks):
        other = gm[b:b + 1, :] if axis == 0 else gm[:, b:b + 1]
        rank = rank + ((other > gm) | ((other == gm) & (b < blk))).astype(jnp.int32)
    return cand & (rank < MOBA_TOPK)


def _moba_prompt_kernel(q_ref, k_ref, vt_ref, kmean_ref, sz_ref, x_ref, gate_ref, w_ref, y_ref, acc_scr, sel_scr,
                        *, group):
    i = pl.program_id(1)
    blk = q_ref.shape[1]
    n_blocks = kmean_ref.shape[1]
    heads = range(q_ref.shape[2] // HEAD_DIM)
    hs = lambda h: slice(h * HEAD_DIM, (h + 1) * HEAD_DIM)
    start = pl.multiple_of(i * blk, blk)
    causal = (lax.broadcasted_iota(jnp.int32, (blk, blk), 0) <= lax.broadcasted_iota(jnp.int32, (blk, blk), 1))

    def values_and_ones(h, st, n_keys):
        return jnp.concatenate([vt_ref[0, hs(h), pl.ds(st, n_keys)], jnp.ones((SUBLANES_BF16, n_keys), BF16)], axis=0)

    qfs = [q_ref[0, :, hs(h)] for h in heads]
    qbs = [(qf * (HEAD_DIM ** -0.5)).astype(BF16) for qf in qfs]
    s_own = [_dot_nt(k_ref[0, pl.ds(start, blk), hs(h)], qbs[h]).astype(BF16) for h in heads]
    gates = [_dot_nt(kmean_ref[0, :, hs(h)], qfs[h], precision=HI) for h in heads]
    m0, p_own = [], []
    for h in heads:
        s = jnp.where(causal, s_own[h], NEG)
        m = jnp.max(s, axis=0, keepdims=True)
        m0.append(m.astype(F32))
        p_own.append(jnp.exp(s - m))
    for h in heads:
        acc_scr[h] = jnp.dot(values_and_ones(h, start, blk), p_own[h], preferred_element_type=F32)
        cand = lax.broadcasted_iota(jnp.int32, gates[h].shape, 0) < i
        sel_scr[h] = _topk_select(gates[h], cand, n_blocks, 0).astype(F32)

    def past_group(jg, ms):
        st = pl.multiple_of(jg * (group * blk), group * blk)
        sgs = [_dot_nt(k_ref[0, pl.ds(st, group * blk), hs(h)], qbs[h]).astype(BF16) for h in heads]
        new_m, pjs = [], []
        for h in heads:
            sj = [jnp.where(sel_scr[h, pl.ds(jg * group + g, 1), :] > 0.5, sgs[h][g * blk:(g + 1) * blk], NEG)
                  for g in range(group)]
            m_blk = functools.reduce(jnp.maximum, [jnp.max(x, axis=0, keepdims=True) for x in sj])
            m_new = jnp.maximum(ms[h], m_blk.astype(F32))
            m_b16 = m_new.astype(BF16)
            pjs.append(jnp.concatenate([jnp.exp(x - m_b16) for x in sj], axis=0))
            new_m.append(m_new)
        for h in heads:
            acc_scr[h] = (jnp.exp(ms[h] - new_m[h]) * acc_scr[h]
                          + jnp.dot(values_and_ones(h, st, group * blk), pjs[h], preferred_element_type=F32))
        return tuple(new_m)

    lax.fori_loop(0, lax.div(i + (group - 1), group), past_group, tuple(m0))
    o = jnp.concatenate([(acc_scr[h][:HEAD_DIM] / acc_scr[h][HEAD_DIM:HEAD_DIM + 1]).T for h in heads], axis=1)
    out = jnp.dot((o * sz_ref[0]).astype(BF16), w_ref[...], preferred_element_type=F32)
    y_ref[0] = x_ref[0] + gate_ref[0] * out


def _moba_prompt(q, kb, vt, kmean, sz, x, gate, w_out):
    n, t, d = q.shape
    n_blocks = t // MOBA_BLOCK
    n_heads = d // HEAD_DIM
    group = max(g for g in range(1, MOBA_KV_GROUP + 1) if n_blocks % g == 0)
    row = pl.BlockSpec((1, MOBA_BLOCK, d), lambda b, i: (b, i, 0))
    once = dict(pipeline_mode=pl.Buffered(1))
    return pl.pallas_call(
        functools.partial(_moba_prompt_kernel, group=group),
        grid=(n, n_blocks),
        in_specs=[row, pl.BlockSpec((1, t, d), lambda b, i: (b, 0, 0), **once),
                  pl.BlockSpec((1, d, t), lambda b, i: (b, 0, 0), **once),
                  pl.BlockSpec((1, n_blocks, d), lambda b, i: (b, 0, 0)),
                  row, row, pl.BlockSpec((1, 1, d), lambda b, i: (b, 0, 0)),
                  pl.BlockSpec((d, d), lambda b, i: (0, 0), **once)],
        out_specs=row,
        out_shape=jax.ShapeDtypeStruct((n, t, d), F32),
        scratch_shapes=[pltpu.VMEM((n_heads, HEAD_DIM + SUBLANES_BF16, MOBA_BLOCK), F32),
                        pltpu.VMEM((n_heads, n_blocks, MOBA_BLOCK), F32)],
        compiler_params=_cparams(("parallel", "arbitrary"), MOBA_VMEM_LIMIT),
        name="moba_prompt",
    )(q, kb, vt, kmean, sz, x, gate, w_out)


def _moba_decode_kernel(pt_ref, q_ref, kn_ref, vn_ref, *refs, n_pages, pages_per_block, prep_chunk):
    del pt_ref
    k_refs, v_refs = refs[:n_pages], refs[n_pages:2 * n_pages]
    prep_in, o_ref, prep_out = refs[2 * n_pages:2 * n_pages + 4], refs[2 * n_pages + 4], refs[2 * n_pages + 5:]
    nb = n_pages // pages_per_block
    t_len, n_heads = q_ref.shape[1], q_ref.shape[2]
    rows = t_len * n_heads
    page = k_refs[0].shape[2]
    cols = page * n_heads
    qall = q_ref[0].reshape(rows, HEAD_DIM)
    qs = qall * (HEAD_DIM ** -0.5)
    same_head = (lax.broadcasted_iota(jnp.int32, (rows, cols), 1) % n_heads
                 == lax.broadcasted_iota(jnp.int32, (rows, cols), 0) % n_heads)
    rep = lambda col: jnp.broadcast_to(col, (rows, LANES))
    per_row = lambda a: jnp.broadcast_to(a[None], (t_len, n_heads, HEAD_DIM)).reshape(rows, HEAD_DIM)

    scores = [_dot_nt(qs, r[0, 0].reshape(cols, HEAD_DIM)) for r in k_refs]
    m_blk, l_blk, gate, probs = [], [], [], []
    for b in range(nb):
        pages = range(b * pages_per_block, (b + 1) * pages_per_block)
        s = [jnp.where(same_head, scores[j], NEG) for j in pages]
        m_b = functools.reduce(jnp.maximum, [jnp.max(x, axis=1, keepdims=True) for x in s])
        p = [jnp.exp(x - m_b) for x in s]
        probs.append(p)
        m_blk.append(rep(m_b))
        l_blk.append(rep(sum(jnp.sum(x, axis=1, keepdims=True) for x in p)))
        kmean = sum(jnp.sum(k_refs[j][0, 0], axis=0) for j in pages) / (page * pages_per_block)
        gate.append(rep(jnp.sum(qall * per_row(kmean), axis=1, keepdims=True)))
    acc_blk = [sum(jnp.dot(x, v_refs[b * pages_per_block + j][0, 0].reshape(cols, HEAD_DIM),
                           preferred_element_type=F32) for j, x in enumerate(probs[b])) for b in range(nb)]
    _delta_prep_kernel(*prep_in, *prep_out, c=prep_chunk)

    sel = []
    for b in range(nb):
        beats = [(gate[o] >= gate[b]) if o < b else (gate[o] > gate[b]) for o in range(nb) if o != b]
        sel.append(sum(x.astype(jnp.int32) for x in beats) < MOBA_TOPK)
    tok = lax.broadcasted_iota(jnp.int32, (rows, LANES), 0) // n_heads
    s_own = [rep(jnp.sum(qs * per_row(kn_ref[0, j]), axis=1, keepdims=True)) for j in range(t_len)]
    m_all = functools.reduce(jnp.maximum, [jnp.where(tok >= j, s_own[j], NEG) for j in range(t_len)]
                             + [jnp.where(sel[b], m_blk[b], NEG) for b in range(nb)])
    den = jnp.zeros((rows, LANES), F32)
    num = jnp.zeros((rows, HEAD_DIM), F32)
    for j in range(t_len):
        pj = jnp.where(tok >= j, jnp.exp(s_own[j] - m_all), 0.0)
        den = den + pj
        num = num + pj * per_row(vn_ref[0, j])
    for b in range(nb):
        wb = jnp.where(sel[b], jnp.exp(m_blk[b] - m_all), 0.0)
        den = den + wb * l_blk[b]
        num = num + wb * acc_blk[b]
    o_ref[0] = (num / den).reshape(t_len, n_heads, HEAD_DIM)


def _moba_decode_with_prep(q, k_new, v_new, cache_k, cache_v, page_table, layer, prep_args, prep_chunk):
    n, t_len, n_heads, _ = q.shape
    n_pages = page_table.shape[1]
    page = cache_k.shape[2]
    assert MOBA_BLOCK % page == 0 and (n_pages * page) % MOBA_BLOCK == 0 and t_len <= MOBA_BLOCK
    page_bytes = page * n_heads * HEAD_DIM * cache_k.dtype.itemsize
    assert 2 * 2 * n_pages * page_bytes <= VMEM_LIMIT - DECODE_VMEM_RESERVE
    tok = pl.BlockSpec((1, t_len, n_heads, HEAD_DIM), lambda s, pt: (s, 0, 0, 0))

    def page_spec(j):
        return pl.BlockSpec((1, 1, page, n_heads, HEAD_DIM), lambda s, pt: (layer, pt[s, j], 0, 0, 0))

    pn, pt_len, wq = prep_args[0].shape
    share = (pn * pt_len) // n
    assert share * n == pn * pt_len and share % prep_chunk == 0 and pt_len % share == 0
    dn_heads = wq // HEAD_DIM
    flat = lambda a: a.reshape(1, pn * pt_len, a.shape[-1])
    rows = lambda c: pl.BlockSpec((1, share, c), lambda s, pt: (0, s, 0))
    operand = BF16 if DN_SCAN_PASSES == 1 else F32
    widths = [(wq, F32), (wq, operand), (wq, operand), (wq, operand), (wq, F32), (dn_heads * prep_chunk, operand)]
    grid_spec = pltpu.PrefetchScalarGridSpec(
        num_scalar_prefetch=1,
        grid=(n,),
        in_specs=[tok, tok, tok] + [page_spec(j) for j in range(n_pages)] * 2 + [rows(wq)] * 3 + [rows(LANES)],
        out_specs=[tok] + [rows(c) for c, _ in widths])
    out = pl.pallas_call(
        functools.partial(_moba_decode_kernel, n_pages=n_pages, pages_per_block=MOBA_BLOCK // page,
                          prep_chunk=prep_chunk),
        grid_spec=grid_spec,
        out_shape=[jax.ShapeDtypeStruct((n, t_len, n_heads, HEAD_DIM), F32)]
                  + [jax.ShapeDtypeStruct((1, pn * pt_len, c), dt) for c, dt in widths],
        compiler_params=_cparams(("parallel",)),
        name="moba_decode",
    )(page_table, q, k_new, v_new, *([cache_k] * n_pages), *([cache_v] * n_pages), *[flat(a) for a in prep_args])
    return out[0], [a.reshape(pn, pt_len, a.shape[-1]) for a in out[1:]]


def _attn_out_kernel(o_ref, sz_ref, x_ref, gate_ref, w_ref, y_ref):
    out = jnp.dot((o_ref[0] * sz_ref[0]).astype(BF16), w_ref[...], preferred_element_type=F32)
    y_ref[0] = x_ref[0] + gate_ref[0] * out


def _attn_out(o, sz, x, gate, w_out, tm):
    n, t, d = x.shape
    row = lambda: pl.BlockSpec((1, tm, d), lambda b, i: (b, i, 0))
    gspec = (pl.BlockSpec((1, 1, d), lambda b, i: (b, 0, 0)) if gate.shape[1] == 1 else row())
    return pl.pallas_call(
        _attn_out_kernel,
        grid=(n, t // tm),
        in_specs=[row(), row(), row(), gspec, pl.BlockSpec(w_out.shape, lambda b, i: (0, 0))],
        out_specs=row(),
        out_shape=jax.ShapeDtypeStruct((n, t, d), F32),
        compiler_params=_cparams(("parallel", "parallel")),
        name="attn_out",
    )(o, sz, x, gate, w_out)


def _pad_lanes(a):
    return jnp.pad(a, ((0, 0), (0, LANES - a.shape[-1])))


def kernel(x_prompt, x_sample, state_delta, state_qkv_conv, state_short_conv, cache_k, cache_v, page_table, c_prompt, c_sample, norm_g, ada_w, ada_b, dn_w_in, dn_conv_w, dn_a_log, dn_dt_bias, dn_norm_g, sc_conv_w, dn_w_out, att_w_in, att_qn_g, att_kn_g, att_w_out):
    bp, seq, d = x_prompt.shape
    bs, t_s, _ = x_sample.shape
    n_heads = dn_a_log.shape[-1]
    w = n_heads * HEAD_DIM
    sc_w = sc_conv_w.shape[-1]
    assert dn_w_in.shape[-1] == 4 * w + 2 * n_heads + 4 * sc_w and state_delta.shape[-2:] == (HEAD_DIM, HEAD_DIM)

    mod = _adaln(jnp.concatenate([c_prompt, c_sample], axis=0), ada_w, ada_b)
    mod_p, mod_s = mod[:, :bp], mod[:, bp:]

    wi = dn_w_in[0]
    w0 = jnp.concatenate([wi[:, :4 * w], _pad_lanes(wi[:, 4 * w:4 * w + 2 * n_heads]),
                          wi[:, 4 * w + 2 * n_heads:]], axis=1).astype(BF16)
    alog, dtb = _pad_lanes(dn_a_log[0][None]), _pad_lanes(dn_dt_bias[0][None])
    ng0 = norm_g[0][None]
    dng = dn_norm_g[0][None]
    w_out0 = dn_w_out[0].astype(BF16)
    nq, nsc = dn_conv_w.shape[1] - 1, sc_conv_w.shape[1] - 1

    q, k, v, gb, sza, yb, qkv_tail, sc_tail = _dn_in_prompt(
        x_prompt, mod_p[0][:, None], ng0, w0, dn_conv_w[0], alog, dtb, sc_conv_w[0], n_heads, sc_w)
    p_qkv, p_sc = qkv_tail[:, SUBLANES - nq:], sc_tail[:, SUBLANES - nsc:]

    tm_of = lambda a: jnp.swapaxes(a, 0, 1)
    qs, ks, vs, gbs, szas, ybs, s_qkv_tm, s_sc_tm = _dn_in_sample(
        tm_of(x_sample), mod_s[0], ng0, w0, tm_of(state_qkv_conv[0]), tm_of(state_short_conv[0]),
        dn_conv_w[0], alog, dtb, sc_conv_w[0], n_heads, sc_w)
    cs = DN_CHUNK_SAMPLE
    assert t_s <= cs
    pad_t = lambda a: jnp.pad(tm_of(a), ((0, 0), (0, cs - t_s), (0, 0)))
    o_s, s_delta = _delta_one_chunk(pad_t(qs), pad_t(ks), pad_t(vs), pad_t(gbs), state_delta[0], 8)
    rows_s = bs * t_s
    flat = lambda a: a.reshape(1, rows_s, a.shape[-1])
    gate_rows = lambda m: flat(jnp.repeat(m, t_s, axis=0))
    xs1 = _dn_out(flat(o_s[:, :t_s]), flat(tm_of(szas)), flat(tm_of(ybs)), flat(x_sample),
                  gate_rows(mod_s[0][:, 2 * d:]), dng, w_out0, rows_s)
    s_qkv, s_sc = tm_of(s_qkv_tm), tm_of(s_sc_tm)

    wa = att_w_in[0].astype(BF16)
    ng1, qn, kn = norm_g[1][None], att_qn_g[0][None], att_kn_g[0][None]
    w_out1 = att_w_out[0].astype(BF16)
    n_att_heads = d // HEAD_DIM

    q1s, k1s, v1s, sz1s = _attn_in(xs1, gate_rows(mod_s[1]), ng1, wa, qn, kn, rows_s, False)
    seqs = lambda a: a.reshape(bs, t_s, n_att_heads, HEAD_DIM)
    o1s, prep = _moba_decode_with_prep(seqs(q1s), seqs(k1s), seqs(v1s), cache_k, cache_v, page_table, 0,
                                       (q, k, v, gb), DN_CHUNK_PROMPT)
    y_sample = _attn_out(flat(o1s.reshape(bs, t_s, d)), sz1s, xs1, gate_rows(mod_s[1][:, 2 * d:]), w_out1, rows_s)

    o_p, p_delta = _delta_scan(*prep, jnp.zeros((bp, n_heads, HEAD_DIM, HEAD_DIM), F32), DN_CHUNK_PROMPT, bp)
    xp1 = _dn_out(o_p, sza, yb, x_prompt, mod_p[0][:, None, 2 * d:], dng, w_out0, PROMPT_ROW_TILE)
    q1, k1, v1, sz1, kb1, vt1, kmean = _attn_in(xp1, mod_p[1][:, None], ng1, wa, qn, kn, MOBA_BLOCK, True)
    y_prompt = _moba_prompt(q1, kb1, vt1, kmean.reshape(bp, seq // MOBA_BLOCK, d), sz1, xp1,
                            mod_p[1][:, None, 2 * d:], w_out1)

    heads = lambda a, n: a.reshape(1, n, -1, n_att_heads, HEAD_DIM)
    return (y_prompt, y_sample.reshape(bs, t_s, d),
            p_delta[None], p_qkv[None], p_sc[None], heads(k1, bp), heads(v1, bp),
            s_delta[None], s_qkv[None], s_sc[None], heads(k1s, bs), heads(v1s, bs))
```

```python
import functools

import jax
import jax.numpy as jnp
from jax import lax
from jax.experimental import pallas as pl
from jax.experimental.pallas import tpu as pltpu

F32 = jnp.float32
BF16 = jnp.bfloat16
HI = lax.Precision.HIGHEST

EPS = 1e-6
LANES = 128
SUBLANES = 8
SUBLANES_BF16 = 16
MXU_COLS = 256
HEAD_DIM = 128
DN_CHUNK_PROMPT = 64
DN_CHUNK_SAMPLE = 16
DN_SOLVE_BLOCK = 16
MOBA_BLOCK = 256
MOBA_TOPK = 3
DECODE_VMEM_LIMIT = 56 * 1024 * 1024
DECODE_VMEM_RESERVE = 20 * 1024 * 1024
MOBA_KV_GROUP = 4
PROMPT_ROW_TILE = 512
DECODE_PREP_EVERY = 4
DN_SCAN_CHUNKS_PER_STEP = 2
NEG = -1e30
DN_GRAM_PASSES = 1
DN_SOLVE_PASSES = 1
DN_SCAN_PASSES = 1
VMEM_LIMIT = 48 * 1024 * 1024
MOBA_VMEM_LIMIT = 58 * 1024 * 1024


def _cparams(sem, vmem_limit=VMEM_LIMIT):
    return pltpu.CompilerParams(dimension_semantics=sem, vmem_limit_bytes=vmem_limit)


def _silu(x):
    return x * jax.nn.sigmoid(x)


def _softplus(x):
    return jnp.maximum(x, 0.0) + jnp.log1p(jnp.exp(-jnp.abs(x)))


def _rms(x, g):
    return x * lax.rsqrt(jnp.mean(x * x, axis=-1, keepdims=True) + EPS) * g


def _modulate(x, norm_g, mod):
    d = x.shape[-1]
    return _rms(x, norm_g) * (1.0 + mod[:, d:2 * d]) + mod[:, :d]


def _dot_nt(a, b, precision=None):
    return lax.dot_general(a, b, (((1,), (1,)), ((), ())), precision=precision,
                           preferred_element_type=F32)


def _dot_tn(a, b, precision=None):
    return lax.dot_general(a, b, (((0,), (0,)), ((), ())), precision=precision,
                           preferred_element_type=F32)


def _adaln_kernel(c_ref, w_ref, b_ref, o_ref):
    o_ref[0] = jnp.dot(c_ref[...], w_ref[0], preferred_element_type=F32) + b_ref[0]


def _adaln(c_all, ada_w, ada_b):
    n_layers, d, d3 = ada_w.shape
    n = c_all.shape[0]
    tn = d
    return pl.pallas_call(
        _adaln_kernel,
        grid=(n_layers, d3 // tn),
        in_specs=[pl.BlockSpec((n, d), lambda l, j: (0, 0)),
                  pl.BlockSpec((1, d, tn), lambda l, j: (l, 0, j)),
                  pl.BlockSpec((1, 1, tn), lambda l, j: (l, 0, j))],
        out_specs=pl.BlockSpec((1, n, tn), lambda l, j: (l, 0, j)),
        out_shape=jax.ShapeDtypeStruct((n_layers, n, d3), F32),
        compiler_params=_cparams(("parallel", "parallel")),
        name="adaln_mod",
    )(c_all, ada_w, ada_b.reshape(n_layers, 1, d3))


def _dn_activations(qkv, ab, alog, dtb, n_heads):
    w = n_heads * HEAD_DIM
    act = _silu(qkv)
    qs, ks = [], []
    for h in range(n_heads):
        qh = act[:, h * HEAD_DIM:(h + 1) * HEAD_DIM]
        kh = act[:, w + h * HEAD_DIM:w + (h + 1) * HEAD_DIM]
        qs.append(qh * lax.rsqrt(jnp.sum(qh * qh, axis=-1, keepdims=True) + EPS) * (HEAD_DIM ** -0.5))
        ks.append(kh * lax.rsqrt(jnp.sum(kh * kh, axis=-1, keepdims=True) + EPS))
    q = jnp.concatenate(qs, axis=-1)
    k = jnp.concatenate(ks, axis=-1)
    v = act[:, 2 * w:3 * w]
    lane = lax.broadcasted_iota(jnp.int32, ab.shape, 1)
    g = -jnp.exp(alog) * _softplus(ab + dtb)
    beta = jax.nn.sigmoid(ab)
    gb = jnp.where(lane < n_heads, g, jnp.where(lane < 2 * n_heads, beta, 0.0))
    return q, k, v, gb


def _shift_rows(cur, prev, s):
    ext = jnp.concatenate([prev, cur], axis=0)
    return ext[SUBLANES - s:SUBLANES - s + cur.shape[0]]


def _conv_rows(cur, prev, w):
    width = w.shape[0]
    y = cur * w[width - 1:width]
    for s in range(1, width):
        y = y + _shift_rows(cur, prev, s) * w[width - 1 - s:width - s]
    return y


def _dn_in_prompt_kernel(x_ref, mod_ref, ng_ref, w_ref, cw_ref, alog_ref, dtb_ref, scw_ref,
                         q_ref, k_ref, v_ref, gb_ref, sza_ref, yb_ref, qkvt_ref, sct_ref,
                         prev_qkv, prev_sc, *, n_heads, sc_w):
    i = pl.program_id(1)
    w = n_heads * HEAD_DIM
    o_z, o_ab, o_sc = 3 * w, 4 * w, 4 * w + LANES

    @pl.when(i == 0)
    def _():
        prev_qkv[...] = jnp.zeros_like(prev_qkv)
        prev_sc[...] = jnp.zeros_like(prev_sc)

    h = _modulate(x_ref[0], ng_ref[...], mod_ref[0]).astype(BF16)
    proj = lambda lo, hi: jnp.dot(h, w_ref[:, lo:hi], preferred_element_type=F32)
    gw = MXU_COLS
    assert w % gw == 0 and sc_w % gw == 0 and gw % HEAD_DIM == 0
    norm_heads = lambda a, scale: jnp.concatenate(
        [a[:, j:j + HEAD_DIM] * (lax.rsqrt(jnp.sum(a[:, j:j + HEAD_DIM] ** 2, axis=-1, keepdims=True) + EPS) * scale)
         for j in range(0, gw, HEAD_DIM)], axis=-1)
    for lo in range(0, 3 * w, gw):
        cs = slice(lo, lo + gw)
        raw = proj(lo, lo + gw)
        act = _silu(_conv_rows(raw, prev_qkv[:, cs], cw_ref[:, cs]))
        prev_qkv[:, cs] = raw[-SUBLANES:]
        qkvt_ref[0, :, cs] = raw[-SUBLANES:]
        part, off = lo // w, lo % w
        if part == 0:
            q_ref[0, :, off:off + gw] = norm_heads(act, HEAD_DIM ** -0.5)
        elif part == 1:
            k_ref[0, :, off:off + gw] = norm_heads(act, 1.0)
        else:
            v_ref[0, :, off:off + gw] = act

    ab = proj(o_ab, o_sc)
    lane = lax.broadcasted_iota(jnp.int32, ab.shape, 1)
    g = -jnp.exp(alog_ref[...]) * _softplus(ab + dtb_ref[...])
    gb_ref[0] = jnp.where(lane < n_heads, g, jnp.where(lane < 2 * n_heads, jax.nn.sigmoid(ab), 0.0))

    for lo in range(0, w, gw):
        sza_ref[0, :, lo:lo + gw] = _silu(proj(o_z + lo, o_z + lo + gw)).astype(BF16)

    for lo in range(0, sc_w, gw):
        cs = slice(lo, lo + gw)
        b_gate, c_gate, x_b, z_b = (proj(o_sc + j * sc_w + lo, o_sc + j * sc_w + lo + gw) for j in range(4))
        cx = c_gate * x_b
        cv = _conv_rows(cx, prev_sc[:, cs], scw_ref[:, cs])
        yb_ref[0, :, cs] = (b_gate * cv * _silu(z_b)).astype(BF16)
        prev_sc[:, cs] = cx[-SUBLANES:]
        sct_ref[0, :, cs] = cx[-SUBLANES:]


def _dn_in_prompt(x, mod, norm_g, w0, conv_w, alog, dtb, sc_conv_w, n_heads, sc_w):
    n, t, d = x.shape
    tm = PROMPT_ROW_TILE
    w = n_heads * HEAD_DIM
    wtot = w0.shape[1]
    row = lambda c: pl.BlockSpec((1, tm, c), lambda b, i: (b, i, 0))
    full = lambda a: pl.BlockSpec(a.shape, lambda b, i: (0,) * a.ndim)
    tail = lambda c: pl.BlockSpec((1, SUBLANES, c), lambda b, i: (b, 0, 0))
    outs = [jax.ShapeDtypeStruct((n, t, w), F32)] * 3 + [
        jax.ShapeDtypeStruct((n, t, LANES), F32), jax.ShapeDtypeStruct((n, t, w), BF16),
        jax.ShapeDtypeStruct((n, t, sc_w), BF16),
        jax.ShapeDtypeStruct((n, SUBLANES, 3 * w), F32), jax.ShapeDtypeStruct((n, SUBLANES, sc_w), F32)]
    return pl.pallas_call(
        functools.partial(_dn_in_prompt_kernel, n_heads=n_heads, sc_w=sc_w),
        grid=(n, t // tm),
        in_specs=[row(d), pl.BlockSpec((1, 1, 3 * d), lambda b, i: (b, 0, 0)), full(norm_g),
                  pl.BlockSpec((d, wtot), lambda b, i: (0, 0)), full(conv_w), full(alog), full(dtb),
                  full(sc_conv_w)],
        out_specs=[row(w), row(w), row(w), row(LANES), row(w), row(sc_w), tail(3 * w), tail(sc_w)],
        out_shape=outs,
        scratch_shapes=[pltpu.VMEM((SUBLANES, 3 * w), F32), pltpu.VMEM((SUBLANES, sc_w), F32)],
        compiler_params=_cparams(("parallel", "arbitrary")),
        name="dn_in_prompt",
    )(x, mod, norm_g, w0, conv_w, alog, dtb, sc_conv_w)


def _dn_in_sample_kernel(x_ref, mod_ref, ng_ref, w_ref, sq_ref, ssc_ref, cw_ref, alog_ref, dtb_ref,
                         scw_ref, q_ref, k_ref, v_ref, gb_ref, sza_ref, yb_ref, nq_ref, nsc_ref,
                         *, n_heads, sc_w):
    t_len = x_ref.shape[0]
    w = n_heads * HEAD_DIM
    o_z, o_ab, o_sc = 3 * w, 4 * w, 4 * w + LANES
    cw, scw = cw_ref[...], scw_ref[...]
    nq, nsc = cw.shape[0] - 1, scw.shape[0] - 1
    xq = [sq_ref[j] for j in range(nq)]
    xsc = [ssc_ref[j] for j in range(nsc)]
    hs, sc_all = [], []
    for t in range(t_len):
        h = _modulate(x_ref[t], ng_ref[...], mod_ref[...]).astype(BF16)
        hs.append(h)
        xq.append(jnp.dot(h, w_ref[:, 0:o_z], preferred_element_type=F32))
        sc = jnp.dot(h, w_ref[:, o_sc:o_sc + 4 * sc_w], preferred_element_type=F32)
        sc_all.append(sc)
        xsc.append(sc[:, sc_w:2 * sc_w] * sc[:, 2 * sc_w:3 * sc_w])
    for t in range(t_len):
        qkv = sum(xq[t + j] * cw[j:j + 1] for j in range(nq + 1))
        ab = jnp.dot(hs[t], w_ref[:, o_ab:o_sc], preferred_element_type=F32)
        q, k, v, gb = _dn_activations(qkv, ab, alog_ref[...], dtb_ref[...], n_heads)
        q_ref[t], k_ref[t], v_ref[t], gb_ref[t] = q, k, v, gb
        sza_ref[t] = _silu(jnp.dot(hs[t], w_ref[:, o_z:o_ab], preferred_element_type=F32)).astype(BF16)
        cv = sum(xsc[t + j] * scw[j:j + 1] for j in range(nsc + 1))
        sc = sc_all[t]
        yb_ref[t] = (sc[:, 0:sc_w] * cv * _silu(sc[:, 3 * sc_w:4 * sc_w])).astype(BF16)
    for j in range(nq):
        nq_ref[j] = xq[t_len + j]
    for j in range(nsc):
        nsc_ref[j] = xsc[t_len + j]


def _dn_in_sample(x_tm, mod, norm_g, w0, s_qkv_tm, s_sc_tm, conv_w, alog, dtb, sc_conv_w, n_heads, sc_w):
    t, n, d = x_tm.shape
    w = n_heads * HEAD_DIM
    gs = min(n, 128)
    wtot = w0.shape[1]
    tm3 = lambda r, c: pl.BlockSpec((r, gs, c), lambda g: (0, g, 0))
    full = lambda a: pl.BlockSpec(a.shape, lambda g: (0,) * a.ndim)
    nq, nsc = conv_w.shape[0] - 1, sc_conv_w.shape[0] - 1
    outs = [jax.ShapeDtypeStruct((t, n, w), F32)] * 3 + [
        jax.ShapeDtypeStruct((t, n, LANES), F32), jax.ShapeDtypeStruct((t, n, w), BF16),
        jax.ShapeDtypeStruct((t, n, sc_w), BF16),
        jax.ShapeDtypeStruct((nq, n, 3 * w), F32), jax.ShapeDtypeStruct((nsc, n, sc_w), F32)]
    return pl.pallas_call(
        functools.partial(_dn_in_sample_kernel, n_heads=n_heads, sc_w=sc_w),
        grid=(n // gs,),
        in_specs=[tm3(t, d), pl.BlockSpec((gs, 3 * d), lambda g: (g, 0)), full(norm_g),
                  pl.BlockSpec((d, wtot), lambda g: (0, 0)), tm3(nq, 3 * w), tm3(nsc, sc_w),
                  full(conv_w), full(alog), full(dtb), full(sc_conv_w)],
        out_specs=[tm3(t, w), tm3(t, w), tm3(t, w), tm3(t, LANES), tm3(t, w), tm3(t, sc_w),
                   tm3(nq, 3 * w), tm3(nsc, sc_w)],
        out_shape=outs,
        compiler_params=_cparams(("parallel",)),
        name="dn_in_sample",
    )(x_tm, mod, norm_g, w0, s_qkv_tm, s_sc_tm, conv_w, alog, dtb, sc_conv_w)


def _bf16_parts(x, n):
    parts, r = [], x
    for i in range(n):
        p = r.astype(BF16)
        parts.append(p)
        if i + 1 < n:
            r = r - p.astype(F32)
    return parts


def _mm(dot, a, b, passes):
    if passes == 1:
        return dot(a.astype(BF16), b.astype(BF16))
    a_hi, a_lo = _bf16_parts(a, 2)
    b_hi, b_lo = _bf16_parts(b, 2)
    return dot(a_hi, b_hi) + (dot(a_hi, b_lo) + dot(a_lo, b_hi))


def _mm_exact01(dot, sel, x):
    hi, mid, lo = (p.astype(sel.dtype) for p in _bf16_parts(x, 3))
    return dot(sel, hi) + (dot(sel, mid) + dot(sel, lo))


def _bdot(a, b):
    return jnp.einsum("bij,bjk->bik", a, b, preferred_element_type=F32)


def _bdot_nt(a, b):
    return jnp.einsum("bik,bjk->bij", a, b, preferred_element_type=F32)


def _dot(a, b):
    return jnp.dot(a, b, preferred_element_type=F32)


def _neumann_inverse(m, order, eye):
    p = eye - m
    mk, k = m, 1
    while 2 * k < order:
        mk = _mm(_bdot, mk, mk, DN_SOLVE_PASSES)
        p = p + _mm(_bdot, p, mk, DN_SOLVE_PASSES)
        k *= 2
    return p


def _unit_lower_inverse(m, c):
    row = lax.broadcasted_iota(jnp.int32, (c, c), 0)
    col = lax.broadcasted_iota(jnp.int32, (c, c), 1)
    eye = (row == col).astype(F32)
    if c <= DN_SOLVE_BLOCK:
        return _neumann_inverse(m, c, eye)
    on_diag = (row // DN_SOLVE_BLOCK) == (col // DN_SOLVE_BLOCK)
    m_diag = jnp.where(on_diag, m, 0.0)
    d_inv = _neumann_inverse(m_diag, DN_SOLVE_BLOCK, eye)
    n_off = _mm(_bdot, d_inv, m - m_diag, DN_SOLVE_PASSES)
    return _mm(_bdot, _neumann_inverse(n_off, c // DN_SOLVE_BLOCK, eye), d_inv, DN_SOLVE_PASSES)


def _delta_prep_kernel(q_ref, k_ref, v_ref, gb_ref, u_ref, w_ref, qg_ref, kd_ref, el_ref, a_ref, *, c):
    g_dim, r_dim, wq = q_ref.shape
    n_heads = wq // HEAD_DIM
    rows = g_dim * r_dim
    nb = rows // c
    row = lax.broadcasted_iota(jnp.int32, (c, c), 0)
    col = lax.broadcasted_iota(jnp.int32, (c, c), 1)
    gb2 = gb_ref[...].reshape(rows, LANES)
    tril = jnp.broadcast_to((row >= col).astype(BF16), (nb, c, c))
    gc3 = _mm_exact01(_bdot, tril, gb2.reshape(nb, c, LANES))
    gc2 = gc3.reshape(rows, LANES)
    gt2 = _mm_exact01(_bdot, jnp.ones((nb, c, c), BF16), gb2.reshape(nb, c, LANES)).reshape(rows, LANES)
    hs = lambda h: slice(h * HEAD_DIM, (h + 1) * HEAD_DIM)
    stack = lambda f: jnp.concatenate([f(h) for h in range(n_heads)], axis=0)
    gch = stack(lambda h: jnp.broadcast_to(gc2[:, h:h + 1], (rows, LANES)).reshape(nb, c, LANES))
    beta = stack(lambda h: jnp.broadcast_to(gb2[:, n_heads + h:n_heads + h + 1], (rows, LANES)).reshape(nb, c, LANES))
    diag_gc = jnp.where(row == col, gch[:, :, :c], 0.0)
    gc_cols = _mm_exact01(_bdot, jnp.ones(diag_gc.shape, BF16), diag_gc)
    gc_last = stack(lambda h: jnp.broadcast_to(gt2[:, h:h + 1], (rows, LANES)).reshape(nb, c, LANES))
    decay = jnp.where(row >= col, jnp.exp(gch[:, :, :c] - gc_cols), 0.0)
    qh = stack(lambda h: q_ref[:, :, hs(h)].reshape(nb, c, HEAD_DIM))
    kh = stack(lambda h: k_ref[:, :, hs(h)].reshape(nb, c, HEAD_DIM))
    vh = stack(lambda h: v_ref[:, :, hs(h)].reshape(nb, c, HEAD_DIM))
    kb = kh * beta
    m = jnp.where(row > col, _mm(_bdot_nt, kb, kh, DN_GRAM_PASSES) * decay, 0.0)
    a = _mm(_bdot_nt, qh, kh, DN_GRAM_PASSES) * decay
    t_inv = _unit_lower_inverse(m, c)
    egc = jnp.exp(gch)
    sol = _mm(_bdot, t_inv, jnp.concatenate([vh * beta, kb * egc], axis=-1), DN_SOLVE_PASSES)
    qg, kd, el = qh * egc, kh * jnp.exp(gc_last - gch), jnp.exp(gc_last)
    for h in range(n_heads):
        of_head = lambda x: x[h * nb:(h + 1) * nb].reshape(g_dim, r_dim, x.shape[-1])
        u_ref[:, :, hs(h)] = of_head(sol[:, :, :HEAD_DIM])
        el_ref[:, :, hs(h)] = of_head(el)
        w_ref[:, :, hs(h)] = of_head(sol[:, :, HEAD_DIM:]).astype(w_ref.dtype)
        qg_ref[:, :, hs(h)] = of_head(qg).astype(qg_ref.dtype)
        kd_ref[:, :, hs(h)] = of_head(kd).astype(kd_ref.dtype)
        a_ref[:, :, h * c:(h + 1) * c] = of_head(a).astype(a_ref.dtype)


def _delta_scan_kernel(u_ref, w_ref, qg_ref, kd_ref, el_ref, a_ref, s0_ref, o_ref, s_ref, *, c):
    @pl.when(pl.program_id(1) == 0)
    def _():
        s_ref[...] = s0_ref[...]

    for j in range(u_ref.shape[1] // c):
        _delta_scan_step(u_ref, w_ref, qg_ref, kd_ref, el_ref, a_ref, o_ref, s_ref, c=c, row0=j * c)


def _delta_one_chunk_kernel(q_ref, k_ref, v_ref, gb_ref, s0_ref, o_ref, s_ref, u, w, qg, kd, el, a, *, c):
    _delta_prep_kernel(q_ref, k_ref, v_ref, gb_ref, u, w, qg, kd, el, a, c=c)
    s_ref[...] = s0_ref[...]
    _delta_scan_step(u, w, qg, kd, el, a, o_ref, s_ref, c=c)


def _delta_scan_step(u_ref, w_ref, qg_ref, kd_ref, el_ref, a_ref, o_ref, s_ref, *, c, row0=0):
    g_dim, _, wq = u_ref.shape
    rs = slice(row0, row0 + c)
    chains = [(g, h, slice(h * HEAD_DIM, (h + 1) * HEAD_DIM)) for g in range(g_dim) for h in range(wq // HEAD_DIM)]
    states = [s_ref[g, h] for g, h, _ in chains]
    v_new = [u_ref[g, rs, sl] - _mm(_dot, w_ref[g, rs, sl], s, DN_SCAN_PASSES)
             for (g, _, sl), s in zip(chains, states)]
    o_inter = [_mm(_dot, qg_ref[g, rs, sl], s, DN_SCAN_PASSES) for (g, _, sl), s in zip(chains, states)]
    for (g, h, sl), s, vn, oi in zip(chains, states, v_new, o_inter):
        o_ref[g, rs, sl] = oi + _mm(_dot, a_ref[g, rs, h * c:(h + 1) * c], vn, DN_SCAN_PASSES)
    for (g, h, sl), s, vn in zip(chains, states, v_new):
        s_ref[g, h] = s * el_ref[g, row0:row0 + 1, sl] + _mm(_dot_tn, kd_ref[g, rs, sl], vn, DN_SCAN_PASSES)


def _delta_scan(u, w, qg, kd, el, a, s0, c, g_dim):
    n, t, wq = u.shape
    n_heads = wq // HEAD_DIM
    rows = c * DN_SCAN_CHUNKS_PER_STEP
    assert t % rows == 0
    blk = lambda cc: pl.BlockSpec((g_dim, rows, cc), lambda b, i: (b, i, 0))
    st = pl.BlockSpec((g_dim, n_heads, HEAD_DIM, HEAD_DIM), lambda b, i: (b, 0, 0, 0))
    return pl.pallas_call(
        functools.partial(_delta_scan_kernel, c=c),
        grid=(n // g_dim, t // rows),
        in_specs=[blk(wq)] * 5 + [blk(n_heads * c), st],
        out_specs=[blk(wq), st],
        out_shape=[jax.ShapeDtypeStruct((n, t, wq), F32),
                   jax.ShapeDtypeStruct((n, n_heads, HEAD_DIM, HEAD_DIM), F32)],
        compiler_params=_cparams(("parallel", "arbitrary")),
        name="delta_scan",
    )(u, w, qg, kd, el, a, s0)


def _delta_one_chunk(q, k, v, gb, s0, g_dim):
    n, c, wq = q.shape
    n_heads = wq // HEAD_DIM
    blk = lambda cc: pl.BlockSpec((g_dim, c, cc), lambda b: (b, 0, 0))
    st = pl.BlockSpec((g_dim, n_heads, HEAD_DIM, HEAD_DIM), lambda b: (b, 0, 0, 0))
    operand = BF16 if DN_SCAN_PASSES == 1 else F32
    tmp = lambda cc, dt: pltpu.VMEM((g_dim, c, cc), dt)
    return pl.pallas_call(
        functools.partial(_delta_one_chunk_kernel, c=c),
        grid=(n // g_dim,),
        in_specs=[blk(wq), blk(wq), blk(wq), blk(LANES), st],
        out_specs=[blk(wq), st],
        out_shape=[jax.ShapeDtypeStruct((n, c, wq), F32),
                   jax.ShapeDtypeStruct((n, n_heads, HEAD_DIM, HEAD_DIM), F32)],
        scratch_shapes=[tmp(wq, F32), tmp(wq, operand), tmp(wq, operand), tmp(wq, operand), tmp(wq, F32),
                        tmp(n_heads * c, operand)],
        compiler_params=_cparams(("parallel",)),
        name="delta_one_chunk",
    )(q, k, v, gb, s0)


def _dn_out_kernel(o_ref, sza_ref, yb_ref, x_ref, gate_ref, ng_ref, w_ref, y_ref):
    o = o_ref[0]
    wa = o.shape[-1]
    ya = jnp.concatenate([_rms(o[:, h * HEAD_DIM:(h + 1) * HEAD_DIM], ng_ref[...])
                          for h in range(wa // HEAD_DIM)], axis=-1) * sza_ref[0]
    out = (jnp.dot(ya.astype(BF16), w_ref[0:wa, :], preferred_element_type=F32)
           + jnp.dot(yb_ref[0].astype(BF16), w_ref[wa:, :], preferred_element_type=F32))
    y_ref[0] = x_ref[0] + gate_ref[0] * out


def _dn_out(o, sza, yb, x, gate, dn_norm_g, w_out, tm):
    n, t, d = x.shape
    wa, wb = o.shape[-1], yb.shape[-1]
    row = lambda c: pl.BlockSpec((1, tm, c), lambda b, i: (b, i, 0))
    gate_rows = gate.shape[1]
    gspec = (pl.BlockSpec((1, 1, d), lambda b, i: (b, 0, 0)) if gate_rows == 1
             else pl.BlockSpec((1, tm, d), lambda b, i: (b, i, 0)))
    return pl.pallas_call(
        _dn_out_kernel,
        grid=(n, t // tm),
        in_specs=[row(wa), row(wa), row(wb), row(d), gspec,
                  pl.BlockSpec(dn_norm_g.shape, lambda b, i: (0, 0)),
                  pl.BlockSpec(w_out.shape, lambda b, i: (0, 0))],
        out_specs=row(d),
        out_shape=jax.ShapeDtypeStruct((n, t, d), F32),
        compiler_params=_cparams(("parallel", "parallel")),
        name="dn_out",
    )(o, sza, yb, x, gate, dn_norm_g, w_out)


def _attn_in_kernel(x_ref, mod_ref, ng_ref, w_ref, qn_ref, kn_ref, q_ref, k_ref, v_ref, sz_ref, *prompt_refs):
    d = x_ref.shape[-1]
    h = _modulate(x_ref[0], ng_ref[...], mod_ref[0]).astype(BF16)
    heads = range(d // HEAD_DIM)
    qr = jnp.dot(h, w_ref[:, 0:d], preferred_element_type=F32)
    q_ref[0] = jnp.concatenate([_rms(qr[:, i * HEAD_DIM:(i + 1) * HEAD_DIM], qn_ref[...]) for i in heads], axis=-1)
    kr = jnp.dot(h, w_ref[:, d:2 * d], preferred_element_type=F32)
    k = jnp.concatenate([_rms(kr[:, i * HEAD_DIM:(i + 1) * HEAD_DIM], kn_ref[...]) for i in heads], axis=-1)
    k_ref[0] = k
    v = jnp.dot(h, w_ref[:, 2 * d:3 * d], preferred_element_type=F32)
    v_ref[0] = v
    sz_ref[0] = _silu(jnp.dot(h, w_ref[:, 3 * d:4 * d], preferred_element_type=F32)).astype(BF16)
    if prompt_refs:
        kb_ref, vt_ref, kmean_ref = prompt_refs
        kb_ref[0] = k.astype(BF16)
        vt_ref[0] = v.T.astype(BF16)
        kmean_ref[0, 0] = jnp.mean(k, axis=0, keepdims=True)


def _attn_in(x, mod, norm_g, w_in, qn_g, kn_g, tm, for_prompt):
    n, t, d = x.shape
    row = lambda: pl.BlockSpec((1, tm, d), lambda b, i: (b, i, 0))
    mod_rows = mod.shape[1]
    mspec = (pl.BlockSpec((1, 1, 3 * d), lambda b, i: (b, 0, 0)) if mod_rows == 1
             else pl.BlockSpec((1, tm, 3 * d), lambda b, i: (b, i, 0)))
    full = lambda a: pl.BlockSpec(a.shape, lambda b, i: (0,) * a.ndim)
    out_specs = [row()] * 4
    outs = [jax.ShapeDtypeStruct((n, t, d), F32)] * 3 + [jax.ShapeDtypeStruct((n, t, d), BF16)]
    if for_prompt:
        assert tm == MOBA_BLOCK
        out_specs += [row(), pl.BlockSpec((1, d, tm), lambda b, i: (b, 0, i)),
                      pl.BlockSpec((1, 1, 1, d), lambda b, i: (b, i, 0, 0))]
        outs += [jax.ShapeDtypeStruct((n, t, d), BF16), jax.ShapeDtypeStruct((n, d, t), BF16),
                 jax.ShapeDtypeStruct((n, t // tm, 1, d), F32)]
    return pl.pallas_call(
        _attn_in_kernel,
        grid=(n, t // tm),
        in_specs=[row(), mspec, full(norm_g), full(w_in), full(qn_g), full(kn_g)],
        out_specs=out_specs,
        out_shape=outs,
        compiler_params=_cparams(("parallel", "parallel")),
        name="attn_in",
    )(x, mod, norm_g, w_in, qn_g, kn_g)


def _topk_select(gate, cand, n_blocks, axis):
    blk = lax.broadcasted_iota(jnp.int32, gate.shape, axis)
    gm = jnp.where(cand, gate, -jnp.inf)
    rank = jnp.zeros(gate.shape, jnp.int32)
    for b in range(n_blocks):
        other = gm[b:b + 1, :] if axis == 0 else gm[:, b:b + 1]
        rank = rank + ((other > gm) | ((other == gm) & (b < blk))).astype(jnp.int32)
    return cand & (rank < MOBA_TOPK)


def _moba_prompt_kernel(q_ref, k_ref, vt_ref, kmean_ref, sz_ref, x_ref, gate_ref, w_ref, y_ref, acc_scr, sel_scr,
                        *, group):
    i = pl.program_id(1)
    blk = q_ref.shape[1]
    n_blocks = kmean_ref.shape[1]
    heads = range(q_ref.shape[2] // HEAD_DIM)
    hs = lambda h: slice(h * HEAD_DIM, (h + 1) * HEAD_DIM)
    start = pl.multiple_of(i * blk, blk)
    causal = (lax.broadcasted_iota(jnp.int32, (blk, blk), 0) <= lax.broadcasted_iota(jnp.int32, (blk, blk), 1))

    def values_and_ones(h, st, n_keys):
        return jnp.concatenate([vt_ref[0, hs(h), pl.ds(st, n_keys)], jnp.ones((SUBLANES_BF16, n_keys), BF16)], axis=0)

    qfs = [q_ref[0, :, hs(h)] for h in heads]
    qbs = [(qf * (HEAD_DIM ** -0.5)).astype(BF16) for qf in qfs]
    s_own = [_dot_nt(k_ref[0, pl.ds(start, blk), hs(h)], qbs[h]).astype(BF16) for h in heads]
    gates = [_dot_nt(kmean_ref[0, :, hs(h)], qfs[h], precision=HI) for h in heads]
    m0, p_own = [], []
    for h in heads:
        s = jnp.where(causal, s_own[h], NEG)
        m = jnp.max(s, axis=0, keepdims=True)
        m0.append(m.astype(F32))
        p_own.append(jnp.exp(s - m))
    for h in heads:
        acc_scr[h] = jnp.dot(values_and_ones(h, start, blk), p_own[h], preferred_element_type=F32)
        cand = lax.broadcasted_iota(jnp.int32, gates[h].shape, 0) < i
        sel_scr[h] = _topk_select(gates[h], cand, n_blocks, 0).astype(F32)

    def past_group(jg, ms):
        st = pl.multiple_of(jg * (group * blk), group * blk)
        sgs = [_dot_nt(k_ref[0, pl.ds(st, group * blk), hs(h)], qbs[h]).astype(BF16) for h in heads]
        new_m, pjs = [], []
        for h in heads:
            sj = [jnp.where(sel_scr[h, pl.ds(jg * group + g, 1), :] > 0.5, sgs[h][g * blk:(g + 1) * blk], NEG)
                  for g in range(group)]
            m_blk = functools.reduce(jnp.maximum, [jnp.max(x, axis=0, keepdims=True) for x in sj])
            m_new = jnp.maximum(ms[h], m_blk.astype(F32))
            m_b16 = m_new.astype(BF16)
            pjs.append(jnp.concatenate([jnp.exp(x - m_b16) for x in sj], axis=0))
            new_m.append(m_new)
        for h in heads:
            acc_scr[h] = (jnp.exp(ms[h] - new_m[h]) * acc_scr[h]
                          + jnp.dot(values_and_ones(h, st, group * blk), pjs[h], preferred_element_type=F32))
        return tuple(new_m)

    lax.fori_loop(0, lax.div(i + (group - 1), group), past_group, tuple(m0))
    o = jnp.concatenate([(acc_scr[h][:HEAD_DIM] / acc_scr[h][HEAD_DIM:HEAD_DIM + 1]).T for h in heads], axis=1)
    out = jnp.dot((o * sz_ref[0]).astype(BF16), w_ref[...], preferred_element_type=F32)
    y_ref[0] = x_ref[0] + gate_ref[0] * out


def _moba_prompt(q, kb, vt, kmean, sz, x, gate, w_out):
    n, t, d = q.shape
    n_blocks = t // MOBA_BLOCK
    n_heads = d // HEAD_DIM
    group = max(g for g in range(1, MOBA_KV_GROUP + 1) if n_blocks % g == 0)
    row = pl.BlockSpec((1, MOBA_BLOCK, d), lambda b, i: (b, i, 0))
    once = dict(pipeline_mode=pl.Buffered(1))
    return pl.pallas_call(
        functools.partial(_moba_prompt_kernel, group=group),
        grid=(n, n_blocks),
        in_specs=[row, pl.BlockSpec((1, t, d), lambda b, i: (b, 0, 0), **once),
                  pl.BlockSpec((1, d, t), lambda b, i: (b, 0, 0), **once),
                  pl.BlockSpec((1, n_blocks, d), lambda b, i: (b, 0, 0)),
                  row, row, pl.BlockSpec((1, 1, d), lambda b, i: (b, 0, 0)),
                  pl.BlockSpec((d, d), lambda b, i: (0, 0), **once)],
        out_specs=row,
        out_shape=jax.ShapeDtypeStruct((n, t, d), F32),
        scratch_shapes=[pltpu.VMEM((n_heads, HEAD_DIM + SUBLANES_BF16, MOBA_BLOCK), F32),
                        pltpu.VMEM((n_heads, n_blocks, MOBA_BLOCK), F32)],
        compiler_params=_cparams(("parallel", "arbitrary"), MOBA_VMEM_LIMIT),
        name="moba_prompt",
    )(q, kb, vt, kmean, sz, x, gate, w_out)


def _moba_decode_kernel(pt_ref, q_ref, kn_ref, vn_ref, *refs, n_pages, pages_per_block, prep_chunk, prep_every):
    del pt_ref
    k_refs, v_refs = refs[:n_pages], refs[n_pages:2 * n_pages]
    prep_in, o_ref, prep_out = refs[2 * n_pages:2 * n_pages + 4], refs[2 * n_pages + 4], refs[2 * n_pages + 5:]
    nb = n_pages // pages_per_block
    t_len, n_heads = q_ref.shape[1], q_ref.shape[2]
    rows = t_len * n_heads
    page = k_refs[0].shape[2]
    cols = page * n_heads
    qall = q_ref[0].reshape(rows, HEAD_DIM)
    qs = qall * (HEAD_DIM ** -0.5)
    same_head = (lax.broadcasted_iota(jnp.int32, (rows, cols), 1) % n_heads
                 == lax.broadcasted_iota(jnp.int32, (rows, cols), 0) % n_heads)
    rep = lambda col: jnp.broadcast_to(col, (rows, LANES))
    per_row = lambda a: jnp.broadcast_to(a[None], (t_len, n_heads, HEAD_DIM)).reshape(rows, HEAD_DIM)

    scores = [_dot_nt(qs, r[0, 0].reshape(cols, HEAD_DIM)) for r in k_refs]
    m_blk, l_blk, gate, probs = [], [], [], []
    for b in range(nb):
        pages = range(b * pages_per_block, (b + 1) * pages_per_block)
        s = [jnp.where(same_head, scores[j], NEG) for j in pages]
        m_b = functools.reduce(jnp.maximum, [jnp.max(x, axis=1, keepdims=True) for x in s])
        p = [jnp.exp(x - m_b) for x in s]
        probs.append(p)
        m_blk.append(rep(m_b))
        l_blk.append(rep(sum(jnp.sum(x, axis=1, keepdims=True) for x in p)))
        kmean = sum(jnp.sum(k_refs[j][0, 0], axis=0) for j in pages) / (page * pages_per_block)
        gate.append(rep(jnp.sum(qall * per_row(kmean), axis=1, keepdims=True)))
    acc_blk = [sum(jnp.dot(x, v_refs[b * pages_per_block + j][0, 0].reshape(cols, HEAD_DIM),
                           preferred_element_type=F32) for j, x in enumerate(probs[b])) for b in range(nb)]
    @pl.when(pl.program_id(0) % prep_every == 0)
    def _():
        _delta_prep_kernel(*prep_in, *prep_out, c=prep_chunk)


    sel = []
    for b in range(nb):
        beats = [(gate[o] >= gate[b]) if o < b else (gate[o] > gate[b]) for o in range(nb) if o != b]
        sel.append(sum(x.astype(jnp.int32) for x in beats) < MOBA_TOPK)
    tok = lax.broadcasted_iota(jnp.int32, (rows, LANES), 0) // n_heads
    s_own = [rep(jnp.sum(qs * per_row(kn_ref[0, j]), axis=1, keepdims=True)) for j in range(t_len)]
    m_all = functools.reduce(jnp.maximum, [jnp.where(tok >= j, s_own[j], NEG) for j in range(t_len)]
                             + [jnp.where(sel[b], m_blk[b], NEG) for b in range(nb)])
    den = jnp.zeros((rows, LANES), F32)
    num = jnp.zeros((rows, HEAD_DIM), F32)
    for j in range(t_len):
        pj = jnp.where(tok >= j, jnp.exp(s_own[j] - m_all), 0.0)
        den = den + pj
        num = num + pj * per_row(vn_ref[0, j])
    for b in range(nb):
        wb = jnp.where(sel[b], jnp.exp(m_blk[b] - m_all), 0.0)
        den = den + wb * l_blk[b]
        num = num + wb * acc_blk[b]
    o_ref[0] = (num / den).reshape(t_len, n_heads, HEAD_DIM)


def _moba_decode_with_prep(q, k_new, v_new, cache_k, cache_v, page_table, layer, prep_args, prep_chunk):
    n, t_len, n_heads, _ = q.shape
    n_pages = page_table.shape[1]
    page = cache_k.shape[2]
    assert MOBA_BLOCK % page == 0 and (n_pages * page) % MOBA_BLOCK == 0 and t_len <= MOBA_BLOCK
    page_bytes = page * n_heads * HEAD_DIM * cache_k.dtype.itemsize
    assert 2 * 2 * n_pages * page_bytes <= DECODE_VMEM_LIMIT - DECODE_VMEM_RESERVE
    tok = pl.BlockSpec((1, t_len, n_heads, HEAD_DIM), lambda s, pt: (s, 0, 0, 0))

    def page_spec(j):
        return pl.BlockSpec((1, 1, page, n_heads, HEAD_DIM), lambda s, pt: (layer, pt[s, j], 0, 0, 0))

    pn, pt_len, wq = prep_args[0].shape
    every = DECODE_PREP_EVERY
    share = (pn * pt_len * every) // n
    assert n % every == 0 and share * n == pn * pt_len * every and share % prep_chunk == 0 and pt_len % share == 0
    dn_heads = wq // HEAD_DIM
    flat = lambda a: a.reshape(1, pn * pt_len, a.shape[-1])
    rows = lambda c: pl.BlockSpec((1, share, c), lambda s, pt: (0, s // every, 0))
    operand = BF16 if DN_SCAN_PASSES == 1 else F32
    widths = [(wq, F32), (wq, operand), (wq, operand), (wq, operand), (wq, F32), (dn_heads * prep_chunk, operand)]
    grid_spec = pltpu.PrefetchScalarGridSpec(
        num_scalar_prefetch=1,
        grid=(n,),
        in_specs=[tok, tok, tok] + [page_spec(j) for j in range(n_pages)] * 2 + [rows(wq)] * 3 + [rows(LANES)],
        out_specs=[tok] + [rows(c) for c, _ in widths])
    out = pl.pallas_call(
        functools.partial(_moba_decode_kernel, n_pages=n_pages, pages_per_block=MOBA_BLOCK // page,
                          prep_chunk=prep_chunk, prep_every=every),
        grid_spec=grid_spec,
        out_shape=[jax.ShapeDtypeStruct((n, t_len, n_heads, HEAD_DIM), F32)]
                  + [jax.ShapeDtypeStruct((1, pn * pt_len, c), dt) for c, dt in widths],
        compiler_params=_cparams(("arbitrary",), DECODE_VMEM_LIMIT),
        name="moba_decode",
    )(page_table, q, k_new, v_new, *([cache_k] * n_pages), *([cache_v] * n_pages), *[flat(a) for a in prep_args])
    return out[0], [a.reshape(pn, pt_len, a.shape[-1]) for a in out[1:]]


def _attn_out_kernel(o_ref, sz_ref, x_ref, gate_ref, w_ref, y_ref):
    out = jnp.dot((o_ref[0] * sz_ref[0]).astype(BF16), w_ref[...], preferred_element_type=F32)
    y_ref[0] = x_ref[0] + gate_ref[0] * out


def _attn_out(o, sz, x, gate, w_out, tm):
    n, t, d = x.shape
    row = lambda: pl.BlockSpec((1, tm, d), lambda b, i: (b, i, 0))
    gspec = (pl.BlockSpec((1, 1, d), lambda b, i: (b, 0, 0)) if gate.shape[1] == 1 else row())
    return pl.pallas_call(
        _attn_out_kernel,
        grid=(n, t // tm),
        in_specs=[row(), row(), row(), gspec, pl.BlockSpec(w_out.shape, lambda b, i: (0, 0))],
        out_specs=row(),
        out_shape=jax.ShapeDtypeStruct((n, t, d), F32),
        compiler_params=_cparams(("parallel", "parallel")),
        name="attn_out",
    )(o, sz, x, gate, w_out)


def _pad_lanes(a):
    return jnp.pad(a, ((0, 0), (0, LANES - a.shape[-1])))


def kernel(x_prompt, x_sample, state_delta, state_qkv_conv, state_short_conv, cache_k, cache_v, page_table, c_prompt, c_sample, norm_g, ada_w, ada_b, dn_w_in, dn_conv_w, dn_a_log, dn_dt_bias, dn_norm_g, sc_conv_w, dn_w_out, att_w_in, att_qn_g, att_kn_g, att_w_out):
    bp, seq, d = x_prompt.shape
    bs, t_s, _ = x_sample.shape
    n_heads = dn_a_log.shape[-1]
    w = n_heads * HEAD_DIM
    sc_w = sc_conv_w.shape[-1]
    assert dn_w_in.shape[-1] == 4 * w + 2 * n_heads + 4 * sc_w and state_delta.shape[-2:] == (HEAD_DIM, HEAD_DIM)

    mod = _adaln(jnp.concatenate([c_prompt, c_sample], axis=0), ada_w, ada_b)
    mod_p, mod_s = mod[:, :bp], mod[:, bp:]

    wi = dn_w_in[0]
    w0 = jnp.concatenate([wi[:, :4 * w], _pad_lanes(wi[:, 4 * w:4 * w + 2 * n_heads]),
                          wi[:, 4 * w + 2 * n_heads:]], axis=1).astype(BF16)
    alog, dtb = _pad_lanes(dn_a_log[0][None]), _pad_lanes(dn_dt_bias[0][None])
    ng0 = norm_g[0][None]
    dng = dn_norm_g[0][None]
    w_out0 = dn_w_out[0].astype(BF16)
    nq, nsc = dn_conv_w.shape[1] - 1, sc_conv_w.shape[1] - 1

    q, k, v, gb, sza, yb, qkv_tail, sc_tail = _dn_in_prompt(
        x_prompt, mod_p[0][:, None], ng0, w0, dn_conv_w[0], alog, dtb, sc_conv_w[0], n_heads, sc_w)
    p_qkv, p_sc = qkv_tail[:, SUBLANES - nq:], sc_tail[:, SUBLANES - nsc:]

    tm_of = lambda a: jnp.swapaxes(a, 0, 1)
    qs, ks, vs, gbs, szas, ybs, s_qkv_tm, s_sc_tm = _dn_in_sample(
        tm_of(x_sample), mod_s[0], ng0, w0, tm_of(state_qkv_conv[0]), tm_of(state_short_conv[0]),
        dn_conv_w[0], alog, dtb, sc_conv_w[0], n_heads, sc_w)
    cs = DN_CHUNK_SAMPLE
    assert t_s <= cs
    pad_t = lambda a: jnp.pad(tm_of(a), ((0, 0), (0, cs - t_s), (0, 0)))
    o_s, s_delta = _delta_one_chunk(pad_t(qs), pad_t(ks), pad_t(vs), pad_t(gbs), state_delta[0], 8)
    rows_s = bs * t_s
    flat = lambda a: a.reshape(1, rows_s, a.shape[-1])
    gate_rows = lambda m: flat(jnp.repeat(m, t_s, axis=0))
    xs1 = _dn_out(flat(o_s[:, :t_s]), flat(tm_of(szas)), flat(tm_of(ybs)), flat(x_sample),
                  gate_rows(mod_s[0][:, 2 * d:]), dng, w_out0, rows_s)
    s_qkv, s_sc = tm_of(s_qkv_tm), tm_of(s_sc_tm)

    wa = att_w_in[0].astype(BF16)
    ng1, qn, kn = norm_g[1][None], att_qn_g[0][None], att_kn_g[0][None]
    w_out1 = att_w_out[0].astype(BF16)
    n_att_heads = d // HEAD_DIM

    q1s, k1s, v1s, sz1s = _attn_in(xs1, gate_rows(mod_s[1]), ng1, wa, qn, kn, rows_s, False)
    seqs = lambda a: a.reshape(bs, t_s, n_att_heads, HEAD_DIM)
    o1s, prep = _moba_decode_with_prep(seqs(q1s), seqs(k1s), seqs(v1s), cache_k, cache_v, page_table, 0,
                                       (q, k, v, gb), DN_CHUNK_PROMPT)
    y_sample = _attn_out(flat(o1s.reshape(bs, t_s, d)), sz1s, xs1, gate_rows(mod_s[1][:, 2 * d:]), w_out1, rows_s)

    o_p, p_delta = _delta_scan(*prep, jnp.zeros((bp, n_heads, HEAD_DIM, HEAD_DIM), F32), DN_CHUNK_PROMPT, bp)
    xp1 = _dn_out(o_p, sza, yb, x_prompt, mod_p[0][:, None, 2 * d:], dng, w_out0, PROMPT_ROW_TILE)
    q1, k1, v1, sz1, kb1, vt1, kmean = _attn_in(xp1, mod_p[1][:, None], ng1, wa, qn, kn, MOBA_BLOCK, True)
    y_prompt = _moba_prompt(q1, kb1, vt1, kmean.reshape(bp, seq // MOBA_BLOCK, d), sz1, xp1,
                            mod_p[1][:, None, 2 * d:], w_out1)

    heads = lambda a, n: a.reshape(1, n, -1, n_att_heads, HEAD_DIM)
    return (y_prompt, y_sample.reshape(bs, t_s, d),
            p_delta[None], p_qkv[None], p_sc[None], heads(k1, bp), heads(v1, bp),
            s_delta[None], s_qkv[None], s_sc[None], heads(k1s, bs), heads(v1s, bs))
```

```python
import functools

import jax
import jax.numpy as jnp
from jax import lax
from jax.experimental import pallas as pl
from jax.experimental.pallas import tpu as pltpu

F32 = jnp.float32
BF16 = jnp.bfloat16
HI = lax.Precision.HIGHEST

EPS = 1e-6
LANES = 128
SUBLANES = 8
SUBLANES_BF16 = 16
HEAD_DIM = 128
DN_CHUNK_PROMPT = 64
DN_CHUNK_SAMPLE = 16
DN_SOLVE_BLOCK = 16
MOBA_BLOCK = 256
MOBA_TOPK = 3
DECODE_VMEM_LIMIT = 48 * 1024 * 1024
DECODE_VMEM_RESERVE = 12 * 1024 * 1024
MOBA_KV_GROUP = 4
PROMPT_ROW_TILE = 512
DECODE_PREP_EVERY = 1
DN_SCAN_CHUNKS_PER_STEP = 4
NEG = -1e30
DN_GRAM_PASSES = 1
DN_SOLVE_PASSES = 1
DN_SCAN_PASSES = 1
VMEM_LIMIT = 48 * 1024 * 1024
MOBA_VMEM_LIMIT = 58 * 1024 * 1024


def _cparams(sem, vmem_limit=VMEM_LIMIT):
    return pltpu.CompilerParams(dimension_semantics=sem, vmem_limit_bytes=vmem_limit)


def _silu(x):
    return x * jax.nn.sigmoid(x)


def _softplus(x):
    return jnp.maximum(x, 0.0) + jnp.log1p(jnp.exp(-jnp.abs(x)))


def _rms(x, g):
    return x * lax.rsqrt(jnp.mean(x * x, axis=-1, keepdims=True) + EPS) * g


def _modulate(x, norm_g, mod):
    d = x.shape[-1]
    return _rms(x, norm_g) * (1.0 + mod[:, d:2 * d]) + mod[:, :d]


def _dot_nt(a, b, precision=None):
    return lax.dot_general(a, b, (((1,), (1,)), ((), ())), precision=precision,
                           preferred_element_type=F32)


def _dot_tn(a, b, precision=None):
    return lax.dot_general(a, b, (((0,), (0,)), ((), ())), precision=precision,
                           preferred_element_type=F32)


def _adaln_kernel(c_ref, w_ref, b_ref, o_ref):
    o_ref[0] = jnp.dot(c_ref[...], w_ref[0], preferred_element_type=F32) + b_ref[0]


def _adaln(c_all, ada_w, ada_b):
    n_layers, d, d3 = ada_w.shape
    n = c_all.shape[0]
    tn = d
    return pl.pallas_call(
        _adaln_kernel,
        grid=(n_layers, d3 // tn),
        in_specs=[pl.BlockSpec((n, d), lambda l, j: (0, 0)),
                  pl.BlockSpec((1, d, tn), lambda l, j: (l, 0, j)),
                  pl.BlockSpec((1, 1, tn), lambda l, j: (l, 0, j))],
        out_specs=pl.BlockSpec((1, n, tn), lambda l, j: (l, 0, j)),
        out_shape=jax.ShapeDtypeStruct((n_layers, n, d3), F32),
        compiler_params=_cparams(("parallel", "parallel")),
        name="adaln_mod",
    )(c_all, ada_w, ada_b.reshape(n_layers, 1, d3))


def _dn_activations(qkv, ab, alog, dtb, n_heads):
    w = n_heads * HEAD_DIM
    act = _silu(qkv)
    qs, ks = [], []
    for h in range(n_heads):
        qh = act[:, h * HEAD_DIM:(h + 1) * HEAD_DIM]
        kh = act[:, w + h * HEAD_DIM:w + (h + 1) * HEAD_DIM]
        qs.append(qh * lax.rsqrt(jnp.sum(qh * qh, axis=-1, keepdims=True) + EPS) * (HEAD_DIM ** -0.5))
        ks.append(kh * lax.rsqrt(jnp.sum(kh * kh, axis=-1, keepdims=True) + EPS))
    q = jnp.concatenate(qs, axis=-1)
    k = jnp.concatenate(ks, axis=-1)
    v = act[:, 2 * w:3 * w]
    lane = lax.broadcasted_iota(jnp.int32, ab.shape, 1)
    g = -jnp.exp(alog) * _softplus(ab + dtb)
    beta = jax.nn.sigmoid(ab)
    gb = jnp.where(lane < n_heads, g, jnp.where(lane < 2 * n_heads, beta, 0.0))
    return q, k, v, gb


def _shift_rows(cur, prev, s):
    ext = jnp.concatenate([prev, cur], axis=0)
    return ext[SUBLANES - s:SUBLANES - s + cur.shape[0]]


def _conv_rows(cur, prev, w):
    width = w.shape[0]
    y = cur * w[width - 1:width]
    for s in range(1, width):
        y = y + _shift_rows(cur, prev, s) * w[width - 1 - s:width - s]
    return y


def _dn_in_prompt_kernel(x_ref, mod_ref, ng_ref, w_ref, cw_ref, alog_ref, dtb_ref, scw_ref,
                         q_ref, k_ref, v_ref, gb_ref, sza_ref, yb_ref, qkvt_ref, sct_ref,
                         prev_qkv, prev_sc, *, n_heads, sc_w):
    i = pl.program_id(1)
    w = n_heads * HEAD_DIM
    o_z, o_ab, o_sc = 3 * w, 4 * w, 4 * w + LANES

    @pl.when(i == 0)
    def _():
        prev_qkv[...] = jnp.zeros_like(prev_qkv)
        prev_sc[...] = jnp.zeros_like(prev_sc)

    h = _modulate(x_ref[0], ng_ref[...], mod_ref[0]).astype(BF16)
    raw = jnp.dot(h, w_ref[:, 0:o_z], preferred_element_type=F32)
    qkv = _conv_rows(raw, prev_qkv[...], cw_ref[...])
    ab = jnp.dot(h, w_ref[:, o_ab:o_sc], preferred_element_type=F32)
    q, k, v, gb = _dn_activations(qkv, ab, alog_ref[...], dtb_ref[...], n_heads)
    q_ref[0], k_ref[0], v_ref[0], gb_ref[0] = q, k, v, gb
    prev_qkv[...] = raw[-SUBLANES:]
    qkvt_ref[0] = raw[-SUBLANES:]

    sza_ref[0] = _silu(jnp.dot(h, w_ref[:, o_z:o_ab], preferred_element_type=F32)).astype(BF16)

    sc = jnp.dot(h, w_ref[:, o_sc:o_sc + 4 * sc_w], preferred_element_type=F32)
    cx = sc[:, sc_w:2 * sc_w] * sc[:, 2 * sc_w:3 * sc_w]
    cv = _conv_rows(cx, prev_sc[...], scw_ref[...])
    yb_ref[0] = (sc[:, 0:sc_w] * cv * _silu(sc[:, 3 * sc_w:4 * sc_w])).astype(BF16)
    prev_sc[...] = cx[-SUBLANES:]
    sct_ref[0] = cx[-SUBLANES:]


def _dn_in_prompt(x, mod, norm_g, w0, conv_w, alog, dtb, sc_conv_w, n_heads, sc_w):
    n, t, d = x.shape
    tm = PROMPT_ROW_TILE
    w = n_heads * HEAD_DIM
    wtot = w0.shape[1]
    row = lambda c: pl.BlockSpec((1, tm, c), lambda b, i: (b, i, 0))
    full = lambda a: pl.BlockSpec(a.shape, lambda b, i: (0,) * a.ndim)
    tail = lambda c: pl.BlockSpec((1, SUBLANES, c), lambda b, i: (b, 0, 0))
    outs = [jax.ShapeDtypeStruct((n, t, w), F32)] * 3 + [
        jax.ShapeDtypeStruct((n, t, LANES), F32), jax.ShapeDtypeStruct((n, t, w), BF16),
        jax.ShapeDtypeStruct((n, t, sc_w), BF16),
        jax.ShapeDtypeStruct((n, SUBLANES, 3 * w), F32), jax.ShapeDtypeStruct((n, SUBLANES, sc_w), F32)]
    return pl.pallas_call(
        functools.partial(_dn_in_prompt_kernel, n_heads=n_heads, sc_w=sc_w),
        grid=(n, t // tm),
        in_specs=[row(d), pl.BlockSpec((1, 1, 3 * d), lambda b, i: (b, 0, 0)), full(norm_g),
                  pl.BlockSpec((d, wtot), lambda b, i: (0, 0)), full(conv_w), full(alog), full(dtb),
                  full(sc_conv_w)],
        out_specs=[row(w), row(w), row(w), row(LANES), row(w), row(sc_w), tail(3 * w), tail(sc_w)],
        out_shape=outs,
        scratch_shapes=[pltpu.VMEM((SUBLANES, 3 * w), F32), pltpu.VMEM((SUBLANES, sc_w), F32)],
        compiler_params=_cparams(("parallel", "arbitrary")),
        name="dn_in_prompt",
    )(x, mod, norm_g, w0, conv_w, alog, dtb, sc_conv_w)


def _dn_in_sample_kernel(x_ref, mod_ref, ng_ref, w_ref, sq_ref, ssc_ref, cw_ref, alog_ref, dtb_ref,
                         scw_ref, q_ref, k_ref, v_ref, gb_ref, sza_ref, yb_ref, nq_ref, nsc_ref,
                         *, n_heads, sc_w):
    t_len = x_ref.shape[0]
    w = n_heads * HEAD_DIM
    o_z, o_ab, o_sc = 3 * w, 4 * w, 4 * w + LANES
    cw, scw = cw_ref[...], scw_ref[...]
    nq, nsc = cw.shape[0] - 1, scw.shape[0] - 1
    xq = [sq_ref[j] for j in range(nq)]
    xsc = [ssc_ref[j] for j in range(nsc)]
    hs, sc_all = [], []
    for t in range(t_len):
        h = _modulate(x_ref[t], ng_ref[...], mod_ref[...]).astype(BF16)
        hs.append(h)
        xq.append(jnp.dot(h, w_ref[:, 0:o_z], preferred_element_type=F32))
        sc = jnp.dot(h, w_ref[:, o_sc:o_sc + 4 * sc_w], preferred_element_type=F32)
        sc_all.append(sc)
        xsc.append(sc[:, sc_w:2 * sc_w] * sc[:, 2 * sc_w:3 * sc_w])
    for t in range(t_len):
        qkv = sum(xq[t + j] * cw[j:j + 1] for j in range(nq + 1))
        ab = jnp.dot(hs[t], w_ref[:, o_ab:o_sc], preferred_element_type=F32)
        q, k, v, gb = _dn_activations(qkv, ab, alog_ref[...], dtb_ref[...], n_heads)
        q_ref[t], k_ref[t], v_ref[t], gb_ref[t] = q, k, v, gb
        sza_ref[t] = _silu(jnp.dot(hs[t], w_ref[:, o_z:o_ab], preferred_element_type=F32)).astype(BF16)
        cv = sum(xsc[t + j] * scw[j:j + 1] for j in range(nsc + 1))
        sc = sc_all[t]
        yb_ref[t] = (sc[:, 0:sc_w] * cv * _silu(sc[:, 3 * sc_w:4 * sc_w])).astype(BF16)
    for j in range(nq):
        nq_ref[j] = xq[t_len + j]
    for j in range(nsc):
        nsc_ref[j] = xsc[t_len + j]


def _dn_in_sample(x_tm, mod, norm_g, w0, s_qkv_tm, s_sc_tm, conv_w, alog, dtb, sc_conv_w, n_heads, sc_w):
    t, n, d = x_tm.shape
    w = n_heads * HEAD_DIM
    gs = min(n, 128)
    wtot = w0.shape[1]
    tm3 = lambda r, c: pl.BlockSpec((r, gs, c), lambda g: (0, g, 0))
    full = lambda a: pl.BlockSpec(a.shape, lambda g: (0,) * a.ndim)
    nq, nsc = conv_w.shape[0] - 1, sc_conv_w.shape[0] - 1
    outs = [jax.ShapeDtypeStruct((t, n, w), F32)] * 3 + [
        jax.ShapeDtypeStruct((t, n, LANES), F32), jax.ShapeDtypeStruct((t, n, w), BF16),
        jax.ShapeDtypeStruct((t, n, sc_w), BF16),
        jax.ShapeDtypeStruct((nq, n, 3 * w), F32), jax.ShapeDtypeStruct((nsc, n, sc_w), F32)]
    return pl.pallas_call(
        functools.partial(_dn_in_sample_kernel, n_heads=n_heads, sc_w=sc_w),
        grid=(n // gs,),
        in_specs=[tm3(t, d), pl.BlockSpec((gs, 3 * d), lambda g: (g, 0)), full(norm_g),
                  pl.BlockSpec((d, wtot), lambda g: (0, 0)), tm3(nq, 3 * w), tm3(nsc, sc_w),
                  full(conv_w), full(alog), full(dtb), full(sc_conv_w)],
        out_specs=[tm3(t, w), tm3(t, w), tm3(t, w), tm3(t, LANES), tm3(t, w), tm3(t, sc_w),
                   tm3(nq, 3 * w), tm3(nsc, sc_w)],
        out_shape=outs,
        compiler_params=_cparams(("parallel",)),
        name="dn_in_sample",
    )(x_tm, mod, norm_g, w0, s_qkv_tm, s_sc_tm, conv_w, alog, dtb, sc_conv_w)


def _bf16_parts(x, n):
    parts, r = [], x
    for i in range(n):
        p = r.astype(BF16)
        parts.append(p)
        if i + 1 < n:
            r = r - p.astype(F32)
    return parts


def _mm(dot, a, b, passes):
    if passes == 1:
        return dot(a.astype(BF16), b.astype(BF16))
    a_hi, a_lo = _bf16_parts(a, 2)
    b_hi, b_lo = _bf16_parts(b, 2)
    return dot(a_hi, b_hi) + (dot(a_hi, b_lo) + dot(a_lo, b_hi))


def _mm_exact01(dot, sel, x):
    hi, mid, lo = (p.astype(sel.dtype) for p in _bf16_parts(x, 3))
    return dot(sel, hi) + (dot(sel, mid) + dot(sel, lo))


def _bdot(a, b):
    return jnp.einsum("bij,bjk->bik", a, b, preferred_element_type=F32)


def _bdot_nt(a, b):
    return jnp.einsum("bik,bjk->bij", a, b, preferred_element_type=F32)


def _dot(a, b):
    return jnp.dot(a, b, preferred_element_type=F32)


def _neumann_inverse(m, order, eye):
    p = eye - m
    mk, k = m, 1
    while 2 * k < order:
        mk = _mm(_bdot, mk, mk, DN_SOLVE_PASSES)
        p = p + _mm(_bdot, p, mk, DN_SOLVE_PASSES)
        k *= 2
    return p


def _unit_lower_inverse(m, c):
    row = lax.broadcasted_iota(jnp.int32, (c, c), 0)
    col = lax.broadcasted_iota(jnp.int32, (c, c), 1)
    eye = (row == col).astype(F32)
    if c <= DN_SOLVE_BLOCK:
        return _neumann_inverse(m, c, eye)
    on_diag = (row // DN_SOLVE_BLOCK) == (col // DN_SOLVE_BLOCK)
    m_diag = jnp.where(on_diag, m, 0.0)
    d_inv = _neumann_inverse(m_diag, DN_SOLVE_BLOCK, eye)
    n_off = _mm(_bdot, d_inv, m - m_diag, DN_SOLVE_PASSES)
    return _mm(_bdot, _neumann_inverse(n_off, c // DN_SOLVE_BLOCK, eye), d_inv, DN_SOLVE_PASSES)


def _delta_prep_kernel(q_ref, k_ref, v_ref, gb_ref, u_ref, w_ref, qg_ref, kd_ref, el_ref, a_ref, *, c):
    g_dim, r_dim, wq = q_ref.shape
    n_heads = wq // HEAD_DIM
    rows = g_dim * r_dim
    nb = rows // c
    row = lax.broadcasted_iota(jnp.int32, (c, c), 0)
    col = lax.broadcasted_iota(jnp.int32, (c, c), 1)
    gb2 = gb_ref[...].reshape(rows, LANES)
    tril = jnp.broadcast_to((row >= col).astype(BF16), (nb, c, c))
    gc3 = _mm_exact01(_bdot, tril, gb2.reshape(nb, c, LANES))
    gc2 = gc3.reshape(rows, LANES)
    gt2 = _mm_exact01(_bdot, jnp.ones((nb, c, c), BF16), gb2.reshape(nb, c, LANES)).reshape(rows, LANES)
    hs = lambda h: slice(h * HEAD_DIM, (h + 1) * HEAD_DIM)
    stack = lambda f: jnp.concatenate([f(h) for h in range(n_heads)], axis=0)
    gch = stack(lambda h: jnp.broadcast_to(gc2[:, h:h + 1], (rows, LANES)).reshape(nb, c, LANES))
    beta = stack(lambda h: jnp.broadcast_to(gb2[:, n_heads + h:n_heads + h + 1], (rows, LANES)).reshape(nb, c, LANES))
    diag_gc = jnp.where(row == col, gch[:, :, :c], 0.0)
    gc_cols = _mm_exact01(_bdot, jnp.ones(diag_gc.shape, BF16), diag_gc)
    gc_last = stack(lambda h: jnp.broadcast_to(gt2[:, h:h + 1], (rows, LANES)).reshape(nb, c, LANES))
    decay = jnp.where(row >= col, jnp.exp(gch[:, :, :c] - gc_cols), 0.0)
    qh = stack(lambda h: q_ref[:, :, hs(h)].reshape(nb, c, HEAD_DIM))
    kh = stack(lambda h: k_ref[:, :, hs(h)].reshape(nb, c, HEAD_DIM))
    vh = stack(lambda h: v_ref[:, :, hs(h)].reshape(nb, c, HEAD_DIM))
    kb = kh * beta
    m = jnp.where(row > col, _mm(_bdot_nt, kb, kh, DN_GRAM_PASSES) * decay, 0.0)
    a = _mm(_bdot_nt, qh, kh, DN_GRAM_PASSES) * decay
    t_inv = _unit_lower_inverse(m, c)
    egc = jnp.exp(gch)
    sol = _mm(_bdot, t_inv, jnp.concatenate([vh * beta, kb * egc], axis=-1), DN_SOLVE_PASSES)
    qg, kd, el = qh * egc, kh * jnp.exp(gc_last - gch), jnp.exp(gc_last)
    for h in range(n_heads):
        of_head = lambda x: x[h * nb:(h + 1) * nb].reshape(g_dim, r_dim, x.shape[-1])
        u_ref[:, :, hs(h)] = of_head(sol[:, :, :HEAD_DIM])
        el_ref[:, :, hs(h)] = of_head(el)
        w_ref[:, :, hs(h)] = of_head(sol[:, :, HEAD_DIM:]).astype(w_ref.dtype)
        qg_ref[:, :, hs(h)] = of_head(qg).astype(qg_ref.dtype)
        kd_ref[:, :, hs(h)] = of_head(kd).astype(kd_ref.dtype)
        a_ref[:, :, h * c:(h + 1) * c] = of_head(a).astype(a_ref.dtype)


def _delta_scan_kernel(u_ref, w_ref, qg_ref, kd_ref, el_ref, a_ref, s0_ref, o_ref, s_ref, *, c):
    @pl.when(pl.program_id(1) == 0)
    def _():
        s_ref[...] = s0_ref[...]

    for j in range(u_ref.shape[1] // c):
        _delta_scan_step(u_ref, w_ref, qg_ref, kd_ref, el_ref, a_ref, o_ref, s_ref, c=c, row0=j * c)


def _delta_one_chunk_kernel(q_ref, k_ref, v_ref, gb_ref, s0_ref, o_ref, s_ref, u, w, qg, kd, el, a, *, c):
    _delta_prep_kernel(q_ref, k_ref, v_ref, gb_ref, u, w, qg, kd, el, a, c=c)
    s_ref[...] = s0_ref[...]
    _delta_scan_step(u, w, qg, kd, el, a, o_ref, s_ref, c=c)


def _delta_scan_step(u_ref, w_ref, qg_ref, kd_ref, el_ref, a_ref, o_ref, s_ref, *, c, row0=0):
    g_dim, _, wq = u_ref.shape
    rs = slice(row0, row0 + c)
    chains = [(g, h, slice(h * HEAD_DIM, (h + 1) * HEAD_DIM)) for g in range(g_dim) for h in range(wq // HEAD_DIM)]
    states = [s_ref[g, h] for g, h, _ in chains]
    v_new = [u_ref[g, rs, sl] - _mm(_dot, w_ref[g, rs, sl], s, DN_SCAN_PASSES)
             for (g, _, sl), s in zip(chains, states)]
    o_inter = [_mm(_dot, qg_ref[g, rs, sl], s, DN_SCAN_PASSES) for (g, _, sl), s in zip(chains, states)]
    for (g, h, sl), s, vn, oi in zip(chains, states, v_new, o_inter):
        o_ref[g, rs, sl] = oi + _mm(_dot, a_ref[g, rs, h * c:(h + 1) * c], vn, DN_SCAN_PASSES)
    for (g, h, sl), s, vn in zip(chains, states, v_new):
        s_ref[g, h] = s * el_ref[g, row0:row0 + 1, sl] + _mm(_dot_tn, kd_ref[g, rs, sl], vn, DN_SCAN_PASSES)


def _delta_scan(u, w, qg, kd, el, a, s0, c, g_dim):
    n, t, wq = u.shape
    n_heads = wq // HEAD_DIM
    rows = c * DN_SCAN_CHUNKS_PER_STEP
    assert t % rows == 0
    blk = lambda cc: pl.BlockSpec((g_dim, rows, cc), lambda b, i: (b, i, 0))
    st = pl.BlockSpec((g_dim, n_heads, HEAD_DIM, HEAD_DIM), lambda b, i: (b, 0, 0, 0))
    return pl.pallas_call(
        functools.partial(_delta_scan_kernel, c=c),
        grid=(n // g_dim, t // rows),
        in_specs=[blk(wq)] * 5 + [blk(n_heads * c), st],
        out_specs=[blk(wq), st],
        out_shape=[jax.ShapeDtypeStruct((n, t, wq), F32),
                   jax.ShapeDtypeStruct((n, n_heads, HEAD_DIM, HEAD_DIM), F32)],
        compiler_params=_cparams(("parallel", "arbitrary")),
        name="delta_scan",
    )(u, w, qg, kd, el, a, s0)


def _delta_one_chunk(q, k, v, gb, s0, g_dim):
    n, c, wq = q.shape
    n_heads = wq // HEAD_DIM
    blk = lambda cc: pl.BlockSpec((g_dim, c, cc), lambda b: (b, 0, 0))
    st = pl.BlockSpec((g_dim, n_heads, HEAD_DIM, HEAD_DIM), lambda b: (b, 0, 0, 0))
    operand = BF16 if DN_SCAN_PASSES == 1 else F32
    tmp = lambda cc, dt: pltpu.VMEM((g_dim, c, cc), dt)
    return pl.pallas_call(
        functools.partial(_delta_one_chunk_kernel, c=c),
        grid=(n // g_dim,),
        in_specs=[blk(wq), blk(wq), blk(wq), blk(LANES), st],
        out_specs=[blk(wq), st],
        out_shape=[jax.ShapeDtypeStruct((n, c, wq), F32),
                   jax.ShapeDtypeStruct((n, n_heads, HEAD_DIM, HEAD_DIM), F32)],
        scratch_shapes=[tmp(wq, F32), tmp(wq, operand), tmp(wq, operand), tmp(wq, operand), tmp(wq, F32),
                        tmp(n_heads * c, operand)],
        compiler_params=_cparams(("parallel",)),
        name="delta_one_chunk",
    )(q, k, v, gb, s0)


def _dn_out_kernel(o_ref, sza_ref, yb_ref, x_ref, gate_ref, ng_ref, w_ref, y_ref):
    o = o_ref[0]
    wa = o.shape[-1]
    ya = jnp.concatenate([_rms(o[:, h * HEAD_DIM:(h + 1) * HEAD_DIM], ng_ref[...])
                          for h in range(wa // HEAD_DIM)], axis=-1) * sza_ref[0]
    out = (jnp.dot(ya.astype(BF16), w_ref[0:wa, :], preferred_element_type=F32)
           + jnp.dot(yb_ref[0].astype(BF16), w_ref[wa:, :], preferred_element_type=F32))
    y_ref[0] = x_ref[0] + gate_ref[0] * out


def _dn_out(o, sza, yb, x, gate, dn_norm_g, w_out, tm):
    n, t, d = x.shape
    wa, wb = o.shape[-1], yb.shape[-1]
    row = lambda c: pl.BlockSpec((1, tm, c), lambda b, i: (b, i, 0))
    gate_rows = gate.shape[1]
    gspec = (pl.BlockSpec((1, 1, d), lambda b, i: (b, 0, 0)) if gate_rows == 1
             else pl.BlockSpec((1, tm, d), lambda b, i: (b, i, 0)))
    return pl.pallas_call(
        _dn_out_kernel,
        grid=(n, t // tm),
        in_specs=[row(wa), row(wa), row(wb), row(d), gspec,
                  pl.BlockSpec(dn_norm_g.shape, lambda b, i: (0, 0)),
                  pl.BlockSpec(w_out.shape, lambda b, i: (0, 0))],
        out_specs=row(d),
        out_shape=jax.ShapeDtypeStruct((n, t, d), F32),
        compiler_params=_cparams(("parallel", "parallel")),
        name="dn_out",
    )(o, sza, yb, x, gate, dn_norm_g, w_out)


def _attn_in_kernel(x_ref, mod_ref, ng_ref, w_ref, qn_ref, kn_ref, q_ref, k_ref, v_ref, sz_ref, *prompt_refs):
    d = x_ref.shape[-1]
    h = _modulate(x_ref[0], ng_ref[...], mod_ref[0]).astype(BF16)
    heads = range(d // HEAD_DIM)
    qr = jnp.dot(h, w_ref[:, 0:d], preferred_element_type=F32)
    q_ref[0] = jnp.concatenate([_rms(qr[:, i * HEAD_DIM:(i + 1) * HEAD_DIM], qn_ref[...]) for i in heads], axis=-1)
    kr = jnp.dot(h, w_ref[:, d:2 * d], preferred_element_type=F32)
    k = jnp.concatenate([_rms(kr[:, i * HEAD_DIM:(i + 1) * HEAD_DIM], kn_ref[...]) for i in heads], axis=-1)
    k_ref[0] = k
    v = jnp.dot(h, w_ref[:, 2 * d:3 * d], preferred_element_type=F32)
    v_ref[0] = v
    sz_ref[0] = _silu(jnp.dot(h, w_ref[:, 3 * d:4 * d], preferred_element_type=F32)).astype(BF16)
    if prompt_refs:
        kb_ref, vt_ref, kmean_ref = prompt_refs
        kb_ref[0] = k.astype(BF16)
        vt_ref[0] = v.T.astype(BF16)
        kmean_ref[0, 0] = jnp.mean(k, axis=0, keepdims=True)


def _attn_in(x, mod, norm_g, w_in, qn_g, kn_g, tm, for_prompt):
    n, t, d = x.shape
    row = lambda: pl.BlockSpec((1, tm, d), lambda b, i: (b, i, 0))
    mod_rows = mod.shape[1]
    mspec = (pl.BlockSpec((1, 1, 3 * d), lambda b, i: (b, 0, 0)) if mod_rows == 1
             else pl.BlockSpec((1, tm, 3 * d), lambda b, i: (b, i, 0)))
    full = lambda a: pl.BlockSpec(a.shape, lambda b, i: (0,) * a.ndim)
    out_specs = [row()] * 4
    outs = [jax.ShapeDtypeStruct((n, t, d), F32)] * 3 + [jax.ShapeDtypeStruct((n, t, d), BF16)]
    if for_prompt:
        assert tm == MOBA_BLOCK
        out_specs += [row(), pl.BlockSpec((1, d, tm), lambda b, i: (b, 0, i)),
                      pl.BlockSpec((1, 1, 1, d), lambda b, i: (b, i, 0, 0))]
        outs += [jax.ShapeDtypeStruct((n, t, d), BF16), jax.ShapeDtypeStruct((n, d, t), BF16),
                 jax.ShapeDtypeStruct((n, t // tm, 1, d), F32)]
    return pl.pallas_call(
        _attn_in_kernel,
        grid=(n, t // tm),
        in_specs=[row(), mspec, full(norm_g), full(w_in), full(qn_g), full(kn_g)],
        out_specs=out_specs,
        out_shape=outs,
        compiler_params=_cparams(("parallel", "parallel")),
        name="attn_in",
    )(x, mod, norm_g, w_in, qn_g, kn_g)


def _topk_select(gate, cand, n_blocks, axis):
    blk = lax.broadcasted_iota(jnp.int32, gate.shape, axis)
    gm = jnp.where(cand, gate, -jnp.inf)
    rank = jnp.zeros(gate.shape, jnp.int32)
    for b in range(n_blocks):
        other = gm[b:b + 1, :] if axis == 0 else gm[:, b:b + 1]
        rank = rank + ((other > gm) | ((other == gm) & (b < blk))).astype(jnp.int32)
    return cand & (rank < MOBA_TOPK)


def _moba_prompt_kernel(q_ref, k_ref, vt_ref, kmean_ref, sz_ref, x_ref, gate_ref, w_ref, y_ref, acc_scr, sel_scr,
                        *, group):
    i = pl.program_id(1)
    blk = q_ref.shape[1]
    n_blocks = kmean_ref.shape[1]
    heads = range(q_ref.shape[2] // HEAD_DIM)
    hs = lambda h: slice(h * HEAD_DIM, (h + 1) * HEAD_DIM)
    start = pl.multiple_of(i * blk, blk)
    causal = (lax.broadcasted_iota(jnp.int32, (blk, blk), 0) <= lax.broadcasted_iota(jnp.int32, (blk, blk), 1))

    def values_and_ones(h, st, n_keys):
        return jnp.concatenate([vt_ref[0, hs(h), pl.ds(st, n_keys)], jnp.ones((SUBLANES_BF16, n_keys), BF16)], axis=0)

    qfs = [q_ref[0, :, hs(h)] for h in heads]
    qbs = [(qf * (HEAD_DIM ** -0.5)).astype(BF16) for qf in qfs]
    s_own = [_dot_nt(k_ref[0, pl.ds(start, blk), hs(h)], qbs[h]).astype(BF16) for h in heads]
    gates = [_dot_nt(kmean_ref[0, :, hs(h)], qfs[h], precision=HI) for h in heads]
    m0, p_own = [], []
    for h in heads:
        s = jnp.where(causal, s_own[h], NEG)
        m = jnp.max(s, axis=0, keepdims=True)
        m0.append(m.astype(F32))
        p_own.append(jnp.exp(s - m))
    for h in heads:
        acc_scr[h] = jnp.dot(values_and_ones(h, start, blk), p_own[h], preferred_element_type=F32)
        cand = lax.broadcasted_iota(jnp.int32, gates[h].shape, 0) < i
        sel_scr[h] = _topk_select(gates[h], cand, n_blocks, 0).astype(F32)

    def past_group(jg, ms):
        st = pl.multiple_of(jg * (group * blk), group * blk)
        sgs = [_dot_nt(k_ref[0, pl.ds(st, group * blk), hs(h)], qbs[h]).astype(BF16) for h in heads]
        new_m, pjs = [], []
        for h in heads:
            sj = [jnp.where(sel_scr[h, pl.ds(jg * group + g, 1), :] > 0.5, sgs[h][g * blk:(g + 1) * blk], NEG)
                  for g in range(group)]
            m_blk = functools.reduce(jnp.maximum, [jnp.max(x, axis=0, keepdims=True) for x in sj])
            m_new = jnp.maximum(ms[h], m_blk.astype(F32))
            m_b16 = m_new.astype(BF16)
            pjs.append(jnp.concatenate([jnp.exp(x - m_b16) for x in sj], axis=0))
            new_m.append(m_new)
        for h in heads:
            acc_scr[h] = (jnp.exp(ms[h] - new_m[h]) * acc_scr[h]
                          + jnp.dot(values_and_ones(h, st, group * blk), pjs[h], preferred_element_type=F32))
        return tuple(new_m)

    lax.fori_loop(0, lax.div(i + (group - 1), group), past_group, tuple(m0))
    o = jnp.concatenate([(acc_scr[h][:HEAD_DIM] / acc_scr[h][HEAD_DIM:HEAD_DIM + 1]).T for h in heads], axis=1)
    out = jnp.dot((o * sz_ref[0]).astype(BF16), w_ref[...], preferred_element_type=F32)
    y_ref[0] = x_ref[0] + gate_ref[0] * out


def _moba_prompt(q, kb, vt, kmean, sz, x, gate, w_out):
    n, t, d = q.shape
    n_blocks = t // MOBA_BLOCK
    n_heads = d // HEAD_DIM
    group = max(g for g in range(1, MOBA_KV_GROUP + 1) if n_blocks % g == 0)
    row = pl.BlockSpec((1, MOBA_BLOCK, d), lambda b, i: (b, i, 0))
    once = dict(pipeline_mode=pl.Buffered(1))
    return pl.pallas_call(
        functools.partial(_moba_prompt_kernel, group=group),
        grid=(n, n_blocks),
        in_specs=[row, pl.BlockSpec((1, t, d), lambda b, i: (b, 0, 0), **once),
                  pl.BlockSpec((1, d, t), lambda b, i: (b, 0, 0), **once),
                  pl.BlockSpec((1, n_blocks, d), lambda b, i: (b, 0, 0)),
                  row, row, pl.BlockSpec((1, 1, d), lambda b, i: (b, 0, 0)),
                  pl.BlockSpec((d, d), lambda b, i: (0, 0), **once)],
        out_specs=row,
        out_shape=jax.ShapeDtypeStruct((n, t, d), F32),
        scratch_shapes=[pltpu.VMEM((n_heads, HEAD_DIM + SUBLANES_BF16, MOBA_BLOCK), F32),
                        pltpu.VMEM((n_heads, n_blocks, MOBA_BLOCK), F32)],
        compiler_params=_cparams(("parallel", "arbitrary"), MOBA_VMEM_LIMIT),
        name="moba_prompt",
    )(q, kb, vt, kmean, sz, x, gate, w_out)


def _moba_decode_kernel(pt_ref, q_ref, kn_ref, vn_ref, *refs, n_pages, pages_per_block, prep_chunk, prep_every):
    del pt_ref
    k_refs, v_refs = refs[:n_pages], refs[n_pages:2 * n_pages]
    prep_in, o_ref, prep_out = refs[2 * n_pages:2 * n_pages + 4], refs[2 * n_pages + 4], refs[2 * n_pages + 5:]
    nb = n_pages // pages_per_block
    t_len, n_heads = q_ref.shape[1], q_ref.shape[2]
    rows = t_len * n_heads
    page = k_refs[0].shape[2]
    cols = page * n_heads
    qall = q_ref[0].reshape(rows, HEAD_DIM)
    qs = qall * (HEAD_DIM ** -0.5)
    same_head = (lax.broadcasted_iota(jnp.int32, (rows, cols), 1) % n_heads
                 == lax.broadcasted_iota(jnp.int32, (rows, cols), 0) % n_heads)
    rep = lambda col: jnp.broadcast_to(col, (rows, LANES))
    per_row = lambda a: jnp.broadcast_to(a[None], (t_len, n_heads, HEAD_DIM)).reshape(rows, HEAD_DIM)

    scores = [_dot_nt(qs, r[0, 0].reshape(cols, HEAD_DIM)) for r in k_refs]
    m_blk, l_blk, gate, probs = [], [], [], []
    for b in range(nb):
        pages = range(b * pages_per_block, (b + 1) * pages_per_block)
        s = [jnp.where(same_head, scores[j], NEG) for j in pages]
        m_b = functools.reduce(jnp.maximum, [jnp.max(x, axis=1, keepdims=True) for x in s])
        p = [jnp.exp(x - m_b) for x in s]
        probs.append(p)
        m_blk.append(rep(m_b))
        l_blk.append(rep(sum(jnp.sum(x, axis=1, keepdims=True) for x in p)))
        kmean = sum(jnp.sum(k_refs[j][0, 0], axis=0) for j in pages) / (page * pages_per_block)
        gate.append(rep(jnp.sum(qall * per_row(kmean), axis=1, keepdims=True)))
    acc_blk = [sum(jnp.dot(x, v_refs[b * pages_per_block + j][0, 0].reshape(cols, HEAD_DIM),
                           preferred_element_type=F32) for j, x in enumerate(probs[b])) for b in range(nb)]
    @pl.when(pl.program_id(0) % prep_every == 0)
    def _():
        _delta_prep_kernel(*prep_in, *prep_out, c=prep_chunk)


    sel = []
    for b in range(nb):
        beats = [(gate[o] >= gate[b]) if o < b else (gate[o] > gate[b]) for o in range(nb) if o != b]
        sel.append(sum(x.astype(jnp.int32) for x in beats) < MOBA_TOPK)
    tok = lax.broadcasted_iota(jnp.int32, (rows, LANES), 0) // n_heads
    s_own = [rep(jnp.sum(qs * per_row(kn_ref[0, j]), axis=1, keepdims=True)) for j in range(t_len)]
    m_all = functools.reduce(jnp.maximum, [jnp.where(tok >= j, s_own[j], NEG) for j in range(t_len)]
                             + [jnp.where(sel[b], m_blk[b], NEG) for b in range(nb)])
    den = jnp.zeros((rows, LANES), F32)
    num = jnp.zeros((rows, HEAD_DIM), F32)
    for j in range(t_len):
        pj = jnp.where(tok >= j, jnp.exp(s_own[j] - m_all), 0.0)
        den = den + pj
        num = num + pj * per_row(vn_ref[0, j])
    for b in range(nb):
        wb = jnp.where(sel[b], jnp.exp(m_blk[b] - m_all), 0.0)
        den = den + wb * l_blk[b]
        num = num + wb * acc_blk[b]
    o_ref[0] = (num / den).reshape(t_len, n_heads, HEAD_DIM)


def _moba_decode_with_prep(q, k_new, v_new, cache_k, cache_v, page_table, layer, prep_args, prep_chunk):
    n, t_len, n_heads, _ = q.shape
    n_pages = page_table.shape[1]
    page = cache_k.shape[2]
    assert MOBA_BLOCK % page == 0 and (n_pages * page) % MOBA_BLOCK == 0 and t_len <= MOBA_BLOCK
    page_bytes = page * n_heads * HEAD_DIM * cache_k.dtype.itemsize
    assert 2 * 2 * n_pages * page_bytes <= DECODE_VMEM_LIMIT - DECODE_VMEM_RESERVE
    tok = pl.BlockSpec((1, t_len, n_heads, HEAD_DIM), lambda s, pt: (s, 0, 0, 0))

    def page_spec(j):
        return pl.BlockSpec((1, 1, page, n_heads, HEAD_DIM), lambda s, pt: (layer, pt[s, j], 0, 0, 0))

    pn, pt_len, wq = prep_args[0].shape
    every = DECODE_PREP_EVERY
    share = (pn * pt_len * every) // n
    assert n % every == 0 and share * n == pn * pt_len * every and share % prep_chunk == 0 and pt_len % share == 0
    dn_heads = wq // HEAD_DIM
    flat = lambda a: a.reshape(1, pn * pt_len, a.shape[-1])
    rows = lambda c: pl.BlockSpec((1, share, c), lambda s, pt: (0, s // every, 0))
    operand = BF16 if DN_SCAN_PASSES == 1 else F32
    widths = [(wq, F32), (wq, operand), (wq, operand), (wq, operand), (wq, F32), (dn_heads * prep_chunk, operand)]
    grid_spec = pltpu.PrefetchScalarGridSpec(
        num_scalar_prefetch=1,
        grid=(n,),
        in_specs=[tok, tok, tok] + [page_spec(j) for j in range(n_pages)] * 2 + [rows(wq)] * 3 + [rows(LANES)],
        out_specs=[tok] + [rows(c) for c, _ in widths])
    out = pl.pallas_call(
        functools.partial(_moba_decode_kernel, n_pages=n_pages, pages_per_block=MOBA_BLOCK // page,
                          prep_chunk=prep_chunk, prep_every=every),
        grid_spec=grid_spec,
        out_shape=[jax.ShapeDtypeStruct((n, t_len, n_heads, HEAD_DIM), F32)]
                  + [jax.ShapeDtypeStruct((1, pn * pt_len, c), dt) for c, dt in widths],
        compiler_params=_cparams(("arbitrary",), DECODE_VMEM_LIMIT),
        name="moba_decode",
    )(page_table, q, k_new, v_new, *([cache_k] * n_pages), *([cache_v] * n_pages), *[flat(a) for a in prep_args])
    return out[0], [a.reshape(pn, pt_len, a.shape[-1]) for a in out[1:]]


def _attn_out_kernel(o_ref, sz_ref, x_ref, gate_ref, w_ref, y_ref):
    out = jnp.dot((o_ref[0] * sz_ref[0]).astype(BF16), w_ref[...], preferred_element_type=F32)
    y_ref[0] = x_ref[0] + gate_ref[0] * out


def _attn_out(o, sz, x, gate, w_out, tm):
    n, t, d = x.shape
    row = lambda: pl.BlockSpec((1, tm, d), lambda b, i: (b, i, 0))
    gspec = (pl.BlockSpec((1, 1, d), lambda b, i: (b, 0, 0)) if gate.shape[1] == 1 else row())
    return pl.pallas_call(
        _attn_out_kernel,
        grid=(n, t // tm),
        in_specs=[row(), row(), row(), gspec, pl.BlockSpec(w_out.shape, lambda b, i: (0, 0))],
        out_specs=row(),
        out_shape=jax.ShapeDtypeStruct((n, t, d), F32),
        compiler_params=_cparams(("parallel", "parallel")),
        name="attn_out",
    )(o, sz, x, gate, w_out)


def _pad_lanes(a):
    return jnp.pad(a, ((0, 0), (0, LANES - a.shape[-1])))


def kernel(x_prompt, x_sample, state_delta, state_qkv_conv, state_short_conv, cache_k, cache_v, page_table, c_prompt, c_sample, norm_g, ada_w, ada_b, dn_w_in, dn_conv_w, dn_a_log, dn_dt_bias, dn_norm_g, sc_conv_w, dn_w_out, att_w_in, att_qn_g, att_kn_g, att_w_out):
    bp, seq, d = x_prompt.shape
    bs, t_s, _ = x_sample.shape
    n_heads = dn_a_log.shape[-1]
    w = n_heads * HEAD_DIM
    sc_w = sc_conv_w.shape[-1]
    assert dn_w_in.shape[-1] == 4 * w + 2 * n_heads + 4 * sc_w and state_delta.shape[-2:] == (HEAD_DIM, HEAD_DIM)

    mod = _adaln(jnp.concatenate([c_prompt, c_sample], axis=0), ada_w, ada_b)
    mod_p, mod_s = mod[:, :bp], mod[:, bp:]

    wi = dn_w_in[0]
    w0 = jnp.concatenate([wi[:, :4 * w], _pad_lanes(wi[:, 4 * w:4 * w + 2 * n_heads]),
                          wi[:, 4 * w + 2 * n_heads:]], axis=1).astype(BF16)
    alog, dtb = _pad_lanes(dn_a_log[0][None]), _pad_lanes(dn_dt_bias[0][None])
    ng0 = norm_g[0][None]
    dng = dn_norm_g[0][None]
    w_out0 = dn_w_out[0].astype(BF16)
    nq, nsc = dn_conv_w.shape[1] - 1, sc_conv_w.shape[1] - 1

    q, k, v, gb, sza, yb, qkv_tail, sc_tail = _dn_in_prompt(
        x_prompt, mod_p[0][:, None], ng0, w0, dn_conv_w[0], alog, dtb, sc_conv_w[0], n_heads, sc_w)
    p_qkv, p_sc = qkv_tail[:, SUBLANES - nq:], sc_tail[:, SUBLANES - nsc:]

    tm_of = lambda a: jnp.swapaxes(a, 0, 1)
    qs, ks, vs, gbs, szas, ybs, s_qkv_tm, s_sc_tm = _dn_in_sample(
        tm_of(x_sample), mod_s[0], ng0, w0, tm_of(state_qkv_conv[0]), tm_of(state_short_conv[0]),
        dn_conv_w[0], alog, dtb, sc_conv_w[0], n_heads, sc_w)
    cs = DN_CHUNK_SAMPLE
    assert t_s <= cs
    pad_t = lambda a: jnp.pad(tm_of(a), ((0, 0), (0, cs - t_s), (0, 0)))
    o_s, s_delta = _delta_one_chunk(pad_t(qs), pad_t(ks), pad_t(vs), pad_t(gbs), state_delta[0], 8)
    rows_s = bs * t_s
    flat = lambda a: a.reshape(1, rows_s, a.shape[-1])
    gate_rows = lambda m: flat(jnp.repeat(m, t_s, axis=0))
    xs1 = _dn_out(flat(o_s[:, :t_s]), flat(tm_of(szas)), flat(tm_of(ybs)), flat(x_sample),
                  gate_rows(mod_s[0][:, 2 * d:]), dng, w_out0, rows_s)
    s_qkv, s_sc = tm_of(s_qkv_tm), tm_of(s_sc_tm)

    wa = att_w_in[0].astype(BF16)
    ng1, qn, kn = norm_g[1][None], att_qn_g[0][None], att_kn_g[0][None]
    w_out1 = att_w_out[0].astype(BF16)
    n_att_heads = d // HEAD_DIM

    q1s, k1s, v1s, sz1s = _attn_in(xs1, gate_rows(mod_s[1]), ng1, wa, qn, kn, rows_s, False)
    seqs = lambda a: a.reshape(bs, t_s, n_att_heads, HEAD_DIM)
    o1s, prep = _moba_decode_with_prep(seqs(q1s), seqs(k1s), seqs(v1s), cache_k, cache_v, page_table, 0,
                                       (q, k, v, gb), DN_CHUNK_PROMPT)
    y_sample = _attn_out(flat(o1s.reshape(bs, t_s, d)), sz1s, xs1, gate_rows(mod_s[1][:, 2 * d:]), w_out1, rows_s)

    o_p, p_delta = _delta_scan(*prep, jnp.zeros((bp, n_heads, HEAD_DIM, HEAD_DIM), F32), DN_CHUNK_PROMPT, bp)
    xp1 = _dn_out(o_p, sza, yb, x_prompt, mod_p[0][:, None, 2 * d:], dng, w_out0, PROMPT_ROW_TILE)
    q1, k1, v1, sz1, kb1, vt1, kmean = _attn_in(xp1, mod_p[1][:, None], ng1, wa, qn, kn, MOBA_BLOCK, True)
    y_prompt = _moba_prompt(q1, kb1, vt1, kmean.reshape(bp, seq // MOBA_BLOCK, d), sz1, xp1,
                            mod_p[1][:, None, 2 * d:], w_out1)

    heads = lambda a, n: a.reshape(1, n, -1, n_att_heads, HEAD_DIM)
    return (y_prompt, y_sample.reshape(bs, t_s, d),
            p_delta[None], p_qkv[None], p_sc[None], heads(k1, bp), heads(v1, bp),
            s_delta[None], s_qkv[None], s_sc[None], heads(k1s, bs), heads(v1s, bs))
```

```python
import functools

import jax
import jax.numpy as jnp
from jax import lax
from jax.experimental import pallas as pl
from jax.experimental.pallas import tpu as pltpu

F32 = jnp.float32
BF16 = jnp.bfloat16

EPS = 1e-6
LANES = 128
SUBLANES = 8
SUBLANES_BF16 = 16
HEAD_DIM = 128
DN_CHUNK_PROMPT = 64
DN_CHUNK_SAMPLE = 16
DN_SOLVE_BLOCK = 16
MOBA_BLOCK = 256
MOBA_TOPK = 3
DECODE_VMEM_LIMIT = 48 * 1024 * 1024
DECODE_VMEM_RESERVE = 12 * 1024 * 1024
MOBA_KV_GROUP = 4
PROMPT_ROW_TILE = 512
DN_SCAN_CHUNKS_PER_STEP = 4
NEG = -1e30
DN_GRAM_PASSES = 1
DN_SOLVE_PASSES = 1
DN_SCAN_PASSES = 1
VMEM_LIMIT = 48 * 1024 * 1024
MOBA_VMEM_LIMIT = 58 * 1024 * 1024


def _cparams(sem, vmem_limit=VMEM_LIMIT):
    return pltpu.CompilerParams(dimension_semantics=sem, vmem_limit_bytes=vmem_limit)


def _silu(x):
    return x * jax.nn.sigmoid(x)


def _softplus(x):
    return jnp.maximum(x, 0.0) + jnp.log1p(jnp.exp(-jnp.abs(x)))


def _rms(x, g):
    return x * lax.rsqrt(jnp.mean(x * x, axis=-1, keepdims=True) + EPS) * g


def _modulate(x, norm_g, mod):
    d = x.shape[-1]
    return _rms(x, norm_g) * (1.0 + mod[:, d:2 * d]) + mod[:, :d]


def _dot_nt(a, b, precision=None):
    return lax.dot_general(a, b, (((1,), (1,)), ((), ())), precision=precision,
                           preferred_element_type=F32)


def _dot_tn(a, b, precision=None):
    return lax.dot_general(a, b, (((0,), (0,)), ((), ())), precision=precision,
                           preferred_element_type=F32)


def _adaln_kernel(c_ref, w_ref, b_ref, o_ref):
    o_ref[0] = jnp.dot(c_ref[...], w_ref[0], preferred_element_type=F32) + b_ref[0]


def _adaln(c_all, ada_w, ada_b):
    n_layers, d, d3 = ada_w.shape
    n = c_all.shape[0]
    tn = d
    return pl.pallas_call(
        _adaln_kernel,
        grid=(n_layers, d3 // tn),
        in_specs=[pl.BlockSpec((n, d), lambda l, j: (0, 0)),
                  pl.BlockSpec((1, d, tn), lambda l, j: (l, 0, j)),
                  pl.BlockSpec((1, 1, tn), lambda l, j: (l, 0, j))],
        out_specs=pl.BlockSpec((1, n, tn), lambda l, j: (l, 0, j)),
        out_shape=jax.ShapeDtypeStruct((n_layers, n, d3), F32),
        compiler_params=_cparams(("parallel", "parallel")),
        name="adaln_mod",
    )(c_all, ada_w, ada_b.reshape(n_layers, 1, d3))


def _dn_activations(qkv, ab, alog, dtb, n_heads):
    w = n_heads * HEAD_DIM
    act = _silu(qkv)
    qs, ks = [], []
    for h in range(n_heads):
        qh = act[:, h * HEAD_DIM:(h + 1) * HEAD_DIM]
        kh = act[:, w + h * HEAD_DIM:w + (h + 1) * HEAD_DIM]
        qs.append(qh * lax.rsqrt(jnp.sum(qh * qh, axis=-1, keepdims=True) + EPS) * (HEAD_DIM ** -0.5))
        ks.append(kh * lax.rsqrt(jnp.sum(kh * kh, axis=-1, keepdims=True) + EPS))
    q = jnp.concatenate(qs, axis=-1)
    k = jnp.concatenate(ks, axis=-1)
    v = act[:, 2 * w:3 * w]
    lane = lax.broadcasted_iota(jnp.int32, ab.shape, 1)
    g = -jnp.exp(alog) * _softplus(ab + dtb)
    beta = jax.nn.sigmoid(ab)
    gb = jnp.where(lane < n_heads, g, jnp.where(lane < 2 * n_heads, beta, 0.0))
    return q, k, v, gb


def _shift_rows(cur, prev, s):
    ext = jnp.concatenate([prev, cur], axis=0)
    return ext[SUBLANES - s:SUBLANES - s + cur.shape[0]]


def _conv_rows(cur, prev, w):
    width = w.shape[0]
    y = cur * w[width - 1:width]
    for s in range(1, width):
        y = y + _shift_rows(cur, prev, s) * w[width - 1 - s:width - s]
    return y


def _dn_in_prompt_kernel(x_ref, mod_ref, ng_ref, w_ref, cw_ref, alog_ref, dtb_ref, scw_ref,
                         q_ref, k_ref, v_ref, gb_ref, sza_ref, yb_ref, qkvt_ref, sct_ref,
                         prev_qkv, prev_sc, *, n_heads, sc_w):
    i = pl.program_id(1)
    w = n_heads * HEAD_DIM
    o_z, o_ab, o_sc = 3 * w, 4 * w, 4 * w + LANES

    @pl.when(i == 0)
    def _():
        prev_qkv[...] = jnp.zeros_like(prev_qkv)
        prev_sc[...] = jnp.zeros_like(prev_sc)

    h = _modulate(x_ref[0], ng_ref[...], mod_ref[0]).astype(BF16)
    raw = jnp.dot(h, w_ref[:, 0:o_z], preferred_element_type=F32)
    qkv = _conv_rows(raw, prev_qkv[...], cw_ref[...])
    ab = jnp.dot(h, w_ref[:, o_ab:o_sc], preferred_element_type=F32)
    q, k, v, gb = _dn_activations(qkv, ab, alog_ref[...], dtb_ref[...], n_heads)
    q_ref[0], k_ref[0], v_ref[0], gb_ref[0] = q, k, v, gb
    prev_qkv[...] = raw[-SUBLANES:]
    qkvt_ref[0] = raw[-SUBLANES:]

    sza_ref[0] = _silu(jnp.dot(h, w_ref[:, o_z:o_ab], preferred_element_type=F32)).astype(BF16)

    sc = jnp.dot(h, w_ref[:, o_sc:o_sc + 4 * sc_w], preferred_element_type=F32)
    cx = sc[:, sc_w:2 * sc_w] * sc[:, 2 * sc_w:3 * sc_w]
    cv = _conv_rows(cx, prev_sc[...], scw_ref[...])
    yb_ref[0] = (sc[:, 0:sc_w] * cv * _silu(sc[:, 3 * sc_w:4 * sc_w])).astype(BF16)
    prev_sc[...] = cx[-SUBLANES:]
    sct_ref[0] = cx[-SUBLANES:]


def _dn_in_prompt(x, mod, norm_g, w0, conv_w, alog, dtb, sc_conv_w, n_heads, sc_w):
    n, t, d = x.shape
    tm = PROMPT_ROW_TILE
    w = n_heads * HEAD_DIM
    wtot = w0.shape[1]
    row = lambda c: pl.BlockSpec((1, tm, c), lambda b, i: (b, i, 0))
    full = lambda a: pl.BlockSpec(a.shape, lambda b, i: (0,) * a.ndim)
    tail = lambda c: pl.BlockSpec((1, SUBLANES, c), lambda b, i: (b, 0, 0))
    outs = [jax.ShapeDtypeStruct((n, t, w), F32)] * 3 + [
        jax.ShapeDtypeStruct((n, t, LANES), F32), jax.ShapeDtypeStruct((n, t, w), BF16),
        jax.ShapeDtypeStruct((n, t, sc_w), BF16),
        jax.ShapeDtypeStruct((n, SUBLANES, 3 * w), F32), jax.ShapeDtypeStruct((n, SUBLANES, sc_w), F32)]
    return pl.pallas_call(
        functools.partial(_dn_in_prompt_kernel, n_heads=n_heads, sc_w=sc_w),
        grid=(n, t // tm),
        in_specs=[row(d), pl.BlockSpec((1, 1, 3 * d), lambda b, i: (b, 0, 0)), full(norm_g),
                  pl.BlockSpec((d, wtot), lambda b, i: (0, 0)), full(conv_w), full(alog), full(dtb),
                  full(sc_conv_w)],
        out_specs=[row(w), row(w), row(w), row(LANES), row(w), row(sc_w), tail(3 * w), tail(sc_w)],
        out_shape=outs,
        scratch_shapes=[pltpu.VMEM((SUBLANES, 3 * w), F32), pltpu.VMEM((SUBLANES, sc_w), F32)],
        compiler_params=_cparams(("parallel", "arbitrary")),
        name="dn_in_prompt",
    )(x, mod, norm_g, w0, conv_w, alog, dtb, sc_conv_w)


def _dn_in_sample_kernel(x_ref, mod_ref, ng_ref, w_ref, sq_ref, ssc_ref, cw_ref, alog_ref, dtb_ref,
                         scw_ref, q_ref, k_ref, v_ref, gb_ref, sza_ref, yb_ref, nq_ref, nsc_ref,
                         *, n_heads, sc_w):
    t_len = x_ref.shape[0]
    w = n_heads * HEAD_DIM
    o_z, o_ab, o_sc = 3 * w, 4 * w, 4 * w + LANES
    cw, scw = cw_ref[...], scw_ref[...]
    nq, nsc = cw.shape[0] - 1, scw.shape[0] - 1
    xq = [sq_ref[j] for j in range(nq)]
    xsc = [ssc_ref[j] for j in range(nsc)]
    hs, sc_all = [], []
    for t in range(t_len):
        h = _modulate(x_ref[t], ng_ref[...], mod_ref[...]).astype(BF16)
        hs.append(h)
        xq.append(jnp.dot(h, w_ref[:, 0:o_z], preferred_element_type=F32))
        sc = jnp.dot(h, w_ref[:, o_sc:o_sc + 4 * sc_w], preferred_element_type=F32)
        sc_all.append(sc)
        xsc.append(sc[:, sc_w:2 * sc_w] * sc[:, 2 * sc_w:3 * sc_w])
    for t in range(t_len):
        qkv = sum(xq[t + j] * cw[j:j + 1] for j in range(nq + 1))
        ab = jnp.dot(hs[t], w_ref[:, o_ab:o_sc], preferred_element_type=F32)
        q, k, v, gb = _dn_activations(qkv, ab, alog_ref[...], dtb_ref[...], n_heads)
        q_ref[t], k_ref[t], v_ref[t], gb_ref[t] = q, k, v, gb
        sza_ref[t] = _silu(jnp.dot(hs[t], w_ref[:, o_z:o_ab], preferred_element_type=F32)).astype(BF16)
        cv = sum(xsc[t + j] * scw[j:j + 1] for j in range(nsc + 1))
        sc = sc_all[t]
        yb_ref[t] = (sc[:, 0:sc_w] * cv * _silu(sc[:, 3 * sc_w:4 * sc_w])).astype(BF16)
    for j in range(nq):
        nq_ref[j] = xq[t_len + j]
    for j in range(nsc):
        nsc_ref[j] = xsc[t_len + j]


def _dn_in_sample(x_tm, mod, norm_g, w0, s_qkv_tm, s_sc_tm, conv_w, alog, dtb, sc_conv_w, n_heads, sc_w):
    t, n, d = x_tm.shape
    w = n_heads * HEAD_DIM
    gs = min(n, 128)
    wtot = w0.shape[1]
    tm3 = lambda r, c: pl.BlockSpec((r, gs, c), lambda g: (0, g, 0))
    full = lambda a: pl.BlockSpec(a.shape, lambda g: (0,) * a.ndim)
    nq, nsc = conv_w.shape[0] - 1, sc_conv_w.shape[0] - 1
    outs = [jax.ShapeDtypeStruct((t, n, w), F32)] * 3 + [
        jax.ShapeDtypeStruct((t, n, LANES), F32), jax.ShapeDtypeStruct((t, n, w), BF16),
        jax.ShapeDtypeStruct((t, n, sc_w), BF16),
        jax.ShapeDtypeStruct((nq, n, 3 * w), F32), jax.ShapeDtypeStruct((nsc, n, sc_w), F32)]
    return pl.pallas_call(
        functools.partial(_dn_in_sample_kernel, n_heads=n_heads, sc_w=sc_w),
        grid=(n // gs,),
        in_specs=[tm3(t, d), pl.BlockSpec((gs, 3 * d), lambda g: (g, 0)), full(norm_g),
                  pl.BlockSpec((d, wtot), lambda g: (0, 0)), tm3(nq, 3 * w), tm3(nsc, sc_w),
                  full(conv_w), full(alog), full(dtb), full(sc_conv_w)],
        out_specs=[tm3(t, w), tm3(t, w), tm3(t, w), tm3(t, LANES), tm3(t, w), tm3(t, sc_w),
                   tm3(nq, 3 * w), tm3(nsc, sc_w)],
        out_shape=outs,
        compiler_params=_cparams(("parallel",)),
        name="dn_in_sample",
    )(x_tm, mod, norm_g, w0, s_qkv_tm, s_sc_tm, conv_w, alog, dtb, sc_conv_w)


def _bf16_parts(x, n):
    parts, r = [], x
    for i in range(n):
        p = r.astype(BF16)
        parts.append(p)
        if i + 1 < n:
            r = r - p.astype(F32)
    return parts


def _mm(dot, a, b, passes):
    if passes == 1:
        return dot(a.astype(BF16), b.astype(BF16))
    a_hi, a_lo = _bf16_parts(a, 2)
    b_hi, b_lo = _bf16_parts(b, 2)
    return dot(a_hi, b_hi) + (dot(a_hi, b_lo) + dot(a_lo, b_hi))


def _mm_exact01(dot, sel, x):
    hi, mid, lo = (p.astype(sel.dtype) for p in _bf16_parts(x, 3))
    return dot(sel, hi) + (dot(sel, mid) + dot(sel, lo))


def _bdot(a, b):
    return jnp.einsum("bij,bjk->bik", a, b, preferred_element_type=F32)


def _bdot_nt(a, b):
    return jnp.einsum("bik,bjk->bij", a, b, preferred_element_type=F32)


def _dot(a, b):
    return jnp.dot(a, b, preferred_element_type=F32)


def _neumann_inverse(m, order, eye):
    p = eye - m
    mk, k = m, 1
    while 2 * k < order:
        mk = _mm(_bdot, mk, mk, DN_SOLVE_PASSES)
        yield
        p = p + _mm(_bdot, p, mk, DN_SOLVE_PASSES)
        yield
        k *= 2
    return p


def _unit_lower_inverse(m, c):
    row = lax.broadcasted_iota(jnp.int32, (c, c), 0)
    col = lax.broadcasted_iota(jnp.int32, (c, c), 1)
    eye = (row == col).astype(F32)
    if c <= DN_SOLVE_BLOCK:
        return (yield from _neumann_inverse(m, c, eye))
    on_diag = (row // DN_SOLVE_BLOCK) == (col // DN_SOLVE_BLOCK)
    m_diag = jnp.where(on_diag, m, 0.0)
    d_inv = yield from _neumann_inverse(m_diag, DN_SOLVE_BLOCK, eye)
    n_off = _mm(_bdot, d_inv, m - m_diag, DN_SOLVE_PASSES)
    yield
    n_inv = yield from _neumann_inverse(n_off, c // DN_SOLVE_BLOCK, eye)
    t_inv = _mm(_bdot, n_inv, d_inv, DN_SOLVE_PASSES)
    yield
    return t_inv


def _delta_prep_kernel(*refs, c):
    for _ in _delta_prep_stages(*refs, c=c):
        pass


def _delta_prep_stages(q_ref, k_ref, v_ref, gb_ref, u_ref, w_ref, qg_ref, kd_ref, el_ref, a_ref, *, c):
    g_dim, r_dim, wq = q_ref.shape
    n_heads = wq // HEAD_DIM
    rows = g_dim * r_dim
    nb = rows // c
    row = lax.broadcasted_iota(jnp.int32, (c, c), 0)
    col = lax.broadcasted_iota(jnp.int32, (c, c), 1)
    gb2 = gb_ref[...].reshape(rows, LANES)
    tril = jnp.broadcast_to((row >= col).astype(BF16), (nb, c, c))
    gc3 = _mm_exact01(_bdot, tril, gb2.reshape(nb, c, LANES))
    gc2 = gc3.reshape(rows, LANES)
    gt2 = _mm_exact01(_bdot, jnp.ones((nb, c, c), BF16), gb2.reshape(nb, c, LANES)).reshape(rows, LANES)
    yield
    hs = lambda h: slice(h * HEAD_DIM, (h + 1) * HEAD_DIM)
    stack = lambda f: jnp.concatenate([f(h) for h in range(n_heads)], axis=0)
    gch = stack(lambda h: jnp.broadcast_to(gc2[:, h:h + 1], (rows, LANES)).reshape(nb, c, LANES))
    beta = stack(lambda h: jnp.broadcast_to(gb2[:, n_heads + h:n_heads + h + 1], (rows, LANES)).reshape(nb, c, LANES))
    diag_gc = jnp.where(row == col, gch[:, :, :c], 0.0)
    gc_cols = _mm_exact01(_bdot, jnp.ones(diag_gc.shape, BF16), diag_gc)
    qh = stack(lambda h: q_ref[:, :, hs(h)].reshape(nb, c, HEAD_DIM))
    kh = stack(lambda h: k_ref[:, :, hs(h)].reshape(nb, c, HEAD_DIM))
    vh = stack(lambda h: v_ref[:, :, hs(h)].reshape(nb, c, HEAD_DIM))
    kb = kh * beta
    kk = _mm(_bdot_nt, kb, kh, DN_GRAM_PASSES)
    qk = _mm(_bdot_nt, qh, kh, DN_GRAM_PASSES)
    yield
    gc_last = stack(lambda h: jnp.broadcast_to(gt2[:, h:h + 1], (rows, LANES)).reshape(nb, c, LANES))
    decay = jnp.where(row >= col, jnp.exp(gch[:, :, :c] - gc_cols), 0.0)
    m = jnp.where(row > col, kk * decay, 0.0)
    a = qk * decay
    t_inv = yield from _unit_lower_inverse(m, c)
    egc = jnp.exp(gch)
    sol = _mm(_bdot, t_inv, jnp.concatenate([vh * beta, kb * egc], axis=-1), DN_SOLVE_PASSES)
    yield
    qg, kd, el = qh * egc, kh * jnp.exp(gc_last - gch), jnp.exp(gc_last)
    for h in range(n_heads):
        of_head = lambda x: x[h * nb:(h + 1) * nb].reshape(g_dim, r_dim, x.shape[-1])
        u_ref[:, :, hs(h)] = of_head(sol[:, :, :HEAD_DIM])
        el_ref[:, :, hs(h)] = of_head(el)
        w_ref[:, :, hs(h)] = of_head(sol[:, :, HEAD_DIM:]).astype(w_ref.dtype)
        qg_ref[:, :, hs(h)] = of_head(qg).astype(qg_ref.dtype)
        kd_ref[:, :, hs(h)] = of_head(kd).astype(kd_ref.dtype)
        a_ref[:, :, h * c:(h + 1) * c] = of_head(a).astype(a_ref.dtype)


def _delta_scan_kernel(u_ref, w_ref, qg_ref, kd_ref, el_ref, a_ref, s0_ref, o_ref, s_ref, *, c):
    @pl.when(pl.program_id(1) == 0)
    def _():
        s_ref[...] = s0_ref[...]

    for j in range(u_ref.shape[1] // c):
        _delta_scan_step(u_ref, w_ref, qg_ref, kd_ref, el_ref, a_ref, o_ref, s_ref, c=c, row0=j * c)


def _delta_one_chunk_kernel(q_ref, k_ref, v_ref, gb_ref, s0_ref, o_ref, s_ref, u, w, qg, kd, el, a, *, c):
    _delta_prep_kernel(q_ref, k_ref, v_ref, gb_ref, u, w, qg, kd, el, a, c=c)
    s_ref[...] = s0_ref[...]
    _delta_scan_step(u, w, qg, kd, el, a, o_ref, s_ref, c=c)


def _delta_scan_step(u_ref, w_ref, qg_ref, kd_ref, el_ref, a_ref, o_ref, s_ref, *, c, row0=0):
    g_dim, _, wq = u_ref.shape
    rs = slice(row0, row0 + c)
    chains = [(g, h, slice(h * HEAD_DIM, (h + 1) * HEAD_DIM)) for g in range(g_dim) for h in range(wq // HEAD_DIM)]
    states = [s_ref[g, h] for g, h, _ in chains]
    v_new = [u_ref[g, rs, sl] - _mm(_dot, w_ref[g, rs, sl], s, DN_SCAN_PASSES)
             for (g, _, sl), s in zip(chains, states)]
    o_inter = [_mm(_dot, qg_ref[g, rs, sl], s, DN_SCAN_PASSES) for (g, _, sl), s in zip(chains, states)]
    for (g, h, sl), s, vn, oi in zip(chains, states, v_new, o_inter):
        o_ref[g, rs, sl] = oi + _mm(_dot, a_ref[g, rs, h * c:(h + 1) * c], vn, DN_SCAN_PASSES)
    for (g, h, sl), s, vn in zip(chains, states, v_new):
        s_ref[g, h] = s * el_ref[g, row0:row0 + 1, sl] + _mm(_dot_tn, kd_ref[g, rs, sl], vn, DN_SCAN_PASSES)


def _delta_scan(u, w, qg, kd, el, a, s0, c, g_dim):
    n, t, wq = u.shape
    n_heads = wq // HEAD_DIM
    rows = c * DN_SCAN_CHUNKS_PER_STEP
    assert t % rows == 0
    blk = lambda cc: pl.BlockSpec((g_dim, rows, cc), lambda b, i: (b, i, 0))
    st = pl.BlockSpec((g_dim, n_heads, HEAD_DIM, HEAD_DIM), lambda b, i: (b, 0, 0, 0))
    return pl.pallas_call(
        functools.partial(_delta_scan_kernel, c=c),
        grid=(n // g_dim, t // rows),
        in_specs=[blk(wq)] * 5 + [blk(n_heads * c), st],
        out_specs=[blk(wq), st],
        out_shape=[jax.ShapeDtypeStruct((n, t, wq), F32),
                   jax.ShapeDtypeStruct((n, n_heads, HEAD_DIM, HEAD_DIM), F32)],
        compiler_params=_cparams(("parallel", "arbitrary")),
        name="delta_scan",
    )(u, w, qg, kd, el, a, s0)


def _delta_one_chunk(q, k, v, gb, s0, g_dim):
    n, c, wq = q.shape
    n_heads = wq // HEAD_DIM
    blk = lambda cc: pl.BlockSpec((g_dim, c, cc), lambda b: (b, 0, 0))
    st = pl.BlockSpec((g_dim, n_heads, HEAD_DIM, HEAD_DIM), lambda b: (b, 0, 0, 0))
    operand = BF16 if DN_SCAN_PASSES == 1 else F32
    tmp = lambda cc, dt: pltpu.VMEM((g_dim, c, cc), dt)
    return pl.pallas_call(
        functools.partial(_delta_one_chunk_kernel, c=c),
        grid=(n // g_dim,),
        in_specs=[blk(wq), blk(wq), blk(wq), blk(LANES), st],
        out_specs=[blk(wq), st],
        out_shape=[jax.ShapeDtypeStruct((n, c, wq), F32),
                   jax.ShapeDtypeStruct((n, n_heads, HEAD_DIM, HEAD_DIM), F32)],
        scratch_shapes=[tmp(wq, F32), tmp(wq, operand), tmp(wq, operand), tmp(wq, operand), tmp(wq, F32),
                        tmp(n_heads * c, operand)],
        compiler_params=_cparams(("parallel",)),
        name="delta_one_chunk",
    )(q, k, v, gb, s0)


def _dn_out_kernel(o_ref, sza_ref, yb_ref, x_ref, gate_ref, ng_ref, w_ref, y_ref):
    o = o_ref[0]
    wa = o.shape[-1]
    ya = jnp.concatenate([_rms(o[:, h * HEAD_DIM:(h + 1) * HEAD_DIM], ng_ref[...])
                          for h in range(wa // HEAD_DIM)], axis=-1) * sza_ref[0]
    out = (jnp.dot(ya.astype(BF16), w_ref[0:wa, :], preferred_element_type=F32)
           + jnp.dot(yb_ref[0].astype(BF16), w_ref[wa:, :], preferred_element_type=F32))
    y_ref[0] = x_ref[0] + gate_ref[0] * out


def _dn_out(o, sza, yb, x, gate, dn_norm_g, w_out, tm):
    n, t, d = x.shape
    wa, wb = o.shape[-1], yb.shape[-1]
    row = lambda c: pl.BlockSpec((1, tm, c), lambda b, i: (b, i, 0))
    gate_rows = gate.shape[1]
    gspec = (pl.BlockSpec((1, 1, d), lambda b, i: (b, 0, 0)) if gate_rows == 1
             else pl.BlockSpec((1, tm, d), lambda b, i: (b, i, 0)))
    return pl.pallas_call(
        _dn_out_kernel,
        grid=(n, t // tm),
        in_specs=[row(wa), row(wa), row(wb), row(d), gspec,
                  pl.BlockSpec(dn_norm_g.shape, lambda b, i: (0, 0)),
                  pl.BlockSpec(w_out.shape, lambda b, i: (0, 0))],
        out_specs=row(d),
        out_shape=jax.ShapeDtypeStruct((n, t, d), F32),
        compiler_params=_cparams(("parallel", "parallel")),
        name="dn_out",
    )(o, sza, yb, x, gate, dn_norm_g, w_out)


def _attn_in_kernel(x_ref, mod_ref, ng_ref, w_ref, qn_ref, kn_ref, q_ref, k_ref, v_ref, sz_ref, *prompt_refs):
    d = x_ref.shape[-1]
    h = _modulate(x_ref[0], ng_ref[...], mod_ref[0]).astype(BF16)
    heads = range(d // HEAD_DIM)
    qr = jnp.dot(h, w_ref[:, 0:d], preferred_element_type=F32)
    q_ref[0] = jnp.concatenate([_rms(qr[:, i * HEAD_DIM:(i + 1) * HEAD_DIM], qn_ref[...]) for i in heads], axis=-1)
    kr = jnp.dot(h, w_ref[:, d:2 * d], preferred_element_type=F32)
    k = jnp.concatenate([_rms(kr[:, i * HEAD_DIM:(i + 1) * HEAD_DIM], kn_ref[...]) for i in heads], axis=-1)
    k_ref[0] = k
    v = jnp.dot(h, w_ref[:, 2 * d:3 * d], preferred_element_type=F32)
    v_ref[0] = v
    sz_ref[0] = _silu(jnp.dot(h, w_ref[:, 3 * d:4 * d], preferred_element_type=F32)).astype(BF16)
    if prompt_refs:
        kb_ref, vt_ref, kmean_ref = prompt_refs
        kb_ref[0] = k.astype(BF16)
        vt_ref[0] = v.T.astype(BF16)
        kmean_ref[0, 0] = jnp.mean(k, axis=0, keepdims=True)


def _attn_in(x, mod, norm_g, w_in, qn_g, kn_g, tm, for_prompt):
    n, t, d = x.shape
    row = lambda: pl.BlockSpec((1, tm, d), lambda b, i: (b, i, 0))
    mod_rows = mod.shape[1]
    mspec = (pl.BlockSpec((1, 1, 3 * d), lambda b, i: (b, 0, 0)) if mod_rows == 1
             else pl.BlockSpec((1, tm, 3 * d), lambda b, i: (b, i, 0)))
    full = lambda a: pl.BlockSpec(a.shape, lambda b, i: (0,) * a.ndim)
    out_specs = [row()] * 4
    outs = [jax.ShapeDtypeStruct((n, t, d), F32)] * 3 + [jax.ShapeDtypeStruct((n, t, d), BF16)]
    if for_prompt:
        assert tm == MOBA_BLOCK
        out_specs += [row(), pl.BlockSpec((1, d, tm), lambda b, i: (b, 0, i)),
                      pl.BlockSpec((1, 1, 1, d), lambda b, i: (b, i, 0, 0))]
        outs += [jax.ShapeDtypeStruct((n, t, d), BF16), jax.ShapeDtypeStruct((n, d, t), BF16),
                 jax.ShapeDtypeStruct((n, t // tm, 1, d), F32)]
    return pl.pallas_call(
        _attn_in_kernel,
        grid=(n, t // tm),
        in_specs=[row(), mspec, full(norm_g), full(w_in), full(qn_g), full(kn_g)],
        out_specs=out_specs,
        out_shape=outs,
        compiler_params=_cparams(("parallel", "parallel")),
        name="attn_in",
    )(x, mod, norm_g, w_in, qn_g, kn_g)


def _topk_select(gate, cand, n_blocks, axis):
    blk = lax.broadcasted_iota(jnp.int32, gate.shape, axis)
    gm = jnp.where(cand, gate, -jnp.inf)
    rank = jnp.zeros(gate.shape, jnp.int32)
    for b in range(n_blocks):
        other = gm[b:b + 1, :] if axis == 0 else gm[:, b:b + 1]
        rank = rank + ((other > gm) | ((other == gm) & (b < blk))).astype(jnp.int32)
    return cand & (rank < MOBA_TOPK)


def _moba_prompt_kernel(q_ref, k_ref, vt_ref, kmean_ref, sz_ref, x_ref, gate_ref, w_ref, y_ref, acc_scr, sel_scr,
                        *, group):
    i = pl.program_id(1)
    blk = q_ref.shape[1]
    n_blocks = kmean_ref.shape[1]
    heads = range(q_ref.shape[2] // HEAD_DIM)
    hs = lambda h: slice(h * HEAD_DIM, (h + 1) * HEAD_DIM)
    start = pl.multiple_of(i * blk, blk)
    causal = (lax.broadcasted_iota(jnp.int32, (blk, blk), 0) <= lax.broadcasted_iota(jnp.int32, (blk, blk), 1))

    def values_and_ones(h, st, n_keys):
        return jnp.concatenate([vt_ref[0, hs(h), pl.ds(st, n_keys)], jnp.ones((SUBLANES_BF16, n_keys), BF16)], axis=0)

    qfs = [q_ref[0, :, hs(h)] for h in heads]
    qbs = [(qf * (HEAD_DIM ** -0.5)).astype(BF16) for qf in qfs]
    s_own = [_dot_nt(k_ref[0, pl.ds(start, blk), hs(h)], qbs[h]).astype(BF16) for h in heads]
    gates = [_dot_nt(kmean_ref[0, :, hs(h)].astype(BF16), qfs[h].astype(BF16)) for h in heads]
    m0, p_own = [], []
    for h in heads:
        s = jnp.where(causal, s_own[h], NEG)
        m = jnp.max(s, axis=0, keepdims=True)
        m0.append(m.astype(F32))
        p_own.append(jnp.exp(s - m))
    for h in heads:
        acc_scr[h] = jnp.dot(values_and_ones(h, start, blk), p_own[h], preferred_element_type=F32)
        cand = lax.broadcasted_iota(jnp.int32, gates[h].shape, 0) < i
        sel_scr[h] = _topk_select(gates[h], cand, n_blocks, 0).astype(F32)

    def past_group(jg, ms):
        st = pl.multiple_of(jg * (group * blk), group * blk)
        sgs = [_dot_nt(k_ref[0, pl.ds(st, group * blk), hs(h)], qbs[h]).astype(BF16) for h in heads]
        new_m, pjs = [], []
        for h in heads:
            sj = [jnp.where(sel_scr[h, pl.ds(jg * group + g, 1), :] > 0.5, sgs[h][g * blk:(g + 1) * blk], NEG)
                  for g in range(group)]
            m_blk = functools.reduce(jnp.maximum, [jnp.max(x, axis=0, keepdims=True) for x in sj])
            m_new = jnp.maximum(ms[h], m_blk.astype(F32))
            m_b16 = m_new.astype(BF16)
            pjs.append(jnp.concatenate([jnp.exp(x - m_b16) for x in sj], axis=0))
            new_m.append(m_new)
        for h in heads:
            acc_scr[h] = (jnp.exp(ms[h] - new_m[h]) * acc_scr[h]
                          + jnp.dot(values_and_ones(h, st, group * blk), pjs[h], preferred_element_type=F32))
        return tuple(new_m)

    lax.fori_loop(0, lax.div(i + (group - 1), group), past_group, tuple(m0))
    o = jnp.concatenate([(acc_scr[h][:HEAD_DIM] / acc_scr[h][HEAD_DIM:HEAD_DIM + 1]).T for h in heads], axis=1)
    out = jnp.dot((o * sz_ref[0]).astype(BF16), w_ref[...], preferred_element_type=F32)
    y_ref[0] = x_ref[0] + gate_ref[0] * out


def _moba_prompt(q, kb, vt, kmean, sz, x, gate, w_out):
    n, t, d = q.shape
    n_blocks = t // MOBA_BLOCK
    n_heads = d // HEAD_DIM
    group = max(g for g in range(1, MOBA_KV_GROUP + 1) if n_blocks % g == 0)
    row = pl.BlockSpec((1, MOBA_BLOCK, d), lambda b, i: (b, i, 0))
    once = dict(pipeline_mode=pl.Buffered(1))
    return pl.pallas_call(
        functools.partial(_moba_prompt_kernel, group=group),
        grid=(n, n_blocks),
        in_specs=[row, pl.BlockSpec((1, t, d), lambda b, i: (b, 0, 0), **once),
                  pl.BlockSpec((1, d, t), lambda b, i: (b, 0, 0), **once),
                  pl.BlockSpec((1, n_blocks, d), lambda b, i: (b, 0, 0)),
                  row, row, pl.BlockSpec((1, 1, d), lambda b, i: (b, 0, 0)),
                  pl.BlockSpec((d, d), lambda b, i: (0, 0), **once)],
        out_specs=row,
        out_shape=jax.ShapeDtypeStruct((n, t, d), F32),
        scratch_shapes=[pltpu.VMEM((n_heads, HEAD_DIM + SUBLANES_BF16, MOBA_BLOCK), F32),
                        pltpu.VMEM((n_heads, n_blocks, MOBA_BLOCK), F32)],
        compiler_params=_cparams(("parallel", "arbitrary"), MOBA_VMEM_LIMIT),
        name="moba_prompt",
    )(q, kb, vt, kmean, sz, x, gate, w_out)


def _moba_decode_kernel(pt_ref, q_ref, kn_ref, vn_ref, *refs, n_pages, pages_per_block, prep_chunk):
    del pt_ref
    k_refs, v_refs = refs[:n_pages], refs[n_pages:2 * n_pages]
    prep_in, o_ref, prep_out = refs[2 * n_pages:2 * n_pages + 4], refs[2 * n_pages + 4], refs[2 * n_pages + 5:]
    nb = n_pages // pages_per_block
    t_len, n_heads = q_ref.shape[1], q_ref.shape[2]
    rows = t_len * n_heads
    page = k_refs[0].shape[2]
    cols = page * n_heads
    qall = q_ref[0].reshape(rows, HEAD_DIM)
    qs = qall * (HEAD_DIM ** -0.5)
    same_head = (lax.broadcasted_iota(jnp.int32, (rows, cols), 1) % n_heads
                 == lax.broadcasted_iota(jnp.int32, (rows, cols), 0) % n_heads)
    rep = lambda col: jnp.broadcast_to(col, (rows, LANES))
    per_row = lambda a: jnp.broadcast_to(a[None], (t_len, n_heads, HEAD_DIM)).reshape(rows, HEAD_DIM)

    prep = _delta_prep_stages(*prep_in, *prep_out, c=prep_chunk)
    scores = []
    for j, r in enumerate(k_refs):
        scores.append(_dot_nt(qs, r[0, 0].reshape(cols, HEAD_DIM)))
        if j % pages_per_block == pages_per_block - 1:
            next(prep, None)
    m_blk, l_blk, gate, probs = [], [], [], []
    for b in range(nb):
        pages = range(b * pages_per_block, (b + 1) * pages_per_block)
        s = [jnp.where(same_head, scores[j], NEG) for j in pages]
        m_b = functools.reduce(jnp.maximum, [jnp.max(x, axis=1, keepdims=True) for x in s])
        p = [jnp.exp(x - m_b) for x in s]
        probs.append(p)
        m_blk.append(rep(m_b))
        l_blk.append(rep(sum(jnp.sum(x, axis=1, keepdims=True) for x in p)))
        kmean = sum(jnp.sum(k_refs[j][0, 0], axis=0) for j in pages) / (page * pages_per_block)
        gate.append(rep(jnp.sum(qall * per_row(kmean), axis=1, keepdims=True)))
    acc_blk = []
    for b in range(nb):
        acc_blk.append(sum(jnp.dot(x, v_refs[b * pages_per_block + j][0, 0].reshape(cols, HEAD_DIM),
                                   preferred_element_type=F32) for j, x in enumerate(probs[b])))
        next(prep, None)
    for _ in prep:
        pass


    sel = []
    for b in range(nb):
        beats = [(gate[o] >= gate[b]) if o < b else (gate[o] > gate[b]) for o in range(nb) if o != b]
        sel.append(sum(x.astype(jnp.int32) for x in beats) < MOBA_TOPK)
    tok = lax.broadcasted_iota(jnp.int32, (rows, LANES), 0) // n_heads
    s_own = [rep(jnp.sum(qs * per_row(kn_ref[0, j]), axis=1, keepdims=True)) for j in range(t_len)]
    m_all = functools.reduce(jnp.maximum, [jnp.where(tok >= j, s_own[j], NEG) for j in range(t_len)]
                             + [jnp.where(sel[b], m_blk[b], NEG) for b in range(nb)])
    den = jnp.zeros((rows, LANES), F32)
    num = jnp.zeros((rows, HEAD_DIM), F32)
    for j in range(t_len):
        pj = jnp.where(tok >= j, jnp.exp(s_own[j] - m_all), 0.0)
        den = den + pj
        num = num + pj * per_row(vn_ref[0, j])
    for b in range(nb):
        wb = jnp.where(sel[b], jnp.exp(m_blk[b] - m_all), 0.0)
        den = den + wb * l_blk[b]
        num = num + wb * acc_blk[b]
    o_ref[0] = (num / den).reshape(t_len, n_heads, HEAD_DIM)


def _moba_decode_with_prep(q, k_new, v_new, cache_k, cache_v, page_table, layer, prep_args, prep_chunk):
    n, t_len, n_heads, _ = q.shape
    n_pages = page_table.shape[1]
    page = cache_k.shape[2]
    assert MOBA_BLOCK % page == 0 and (n_pages * page) % MOBA_BLOCK == 0 and t_len <= MOBA_BLOCK
    page_bytes = page * n_heads * HEAD_DIM * cache_k.dtype.itemsize
    assert 2 * 2 * n_pages * page_bytes <= DECODE_VMEM_LIMIT - DECODE_VMEM_RESERVE
    tok = pl.BlockSpec((1, t_len, n_heads, HEAD_DIM), lambda s, pt: (s, 0, 0, 0))

    def page_spec(j):
        return pl.BlockSpec((1, 1, page, n_heads, HEAD_DIM), lambda s, pt: (layer, pt[s, j], 0, 0, 0))

    pn, pt_len, wq = prep_args[0].shape
    share = (pn * pt_len) // n
    assert share * n == pn * pt_len and share % prep_chunk == 0 and pt_len % share == 0
    dn_heads = wq // HEAD_DIM
    flat = lambda a: a.reshape(1, pn * pt_len, a.shape[-1])
    rows = lambda c: pl.BlockSpec((1, share, c), lambda s, pt: (0, s, 0))
    operand = BF16 if DN_SCAN_PASSES == 1 else F32
    widths = [(wq, F32), (wq, operand), (wq, operand), (wq, operand), (wq, F32), (dn_heads * prep_chunk, operand)]
    grid_spec = pltpu.PrefetchScalarGridSpec(
        num_scalar_prefetch=1,
        grid=(n,),
        in_specs=[tok, tok, tok] + [page_spec(j) for j in range(n_pages)] * 2 + [rows(wq)] * 3 + [rows(LANES)],
        out_specs=[tok] + [rows(c) for c, _ in widths])
    out = pl.pallas_call(
        functools.partial(_moba_decode_kernel, n_pages=n_pages, pages_per_block=MOBA_BLOCK // page,
                          prep_chunk=prep_chunk),
        grid_spec=grid_spec,
        out_shape=[jax.ShapeDtypeStruct((n, t_len, n_heads, HEAD_DIM), F32)]
                  + [jax.ShapeDtypeStruct((1, pn * pt_len, c), dt) for c, dt in widths],
        compiler_params=_cparams(("parallel",), DECODE_VMEM_LIMIT),
        name="moba_decode",
    )(page_table, q, k_new, v_new, *([cache_k] * n_pages), *([cache_v] * n_pages), *[flat(a) for a in prep_args])
    return out[0], [a.reshape(pn, pt_len, a.shape[-1]) for a in out[1:]]


def _attn_out_kernel(o_ref, sz_ref, x_ref, gate_ref, w_ref, y_ref):
    out = jnp.dot((o_ref[0] * sz_ref[0]).astype(BF16), w_ref[...], preferred_element_type=F32)
    y_ref[0] = x_ref[0] + gate_ref[0] * out


def _attn_out(o, sz, x, gate, w_out, tm):
    n, t, d = x.shape
    row = lambda: pl.BlockSpec((1, tm, d), lambda b, i: (b, i, 0))
    gspec = (pl.BlockSpec((1, 1, d), lambda b, i: (b, 0, 0)) if gate.shape[1] == 1 else row())
    return pl.pallas_call(
        _attn_out_kernel,
        grid=(n, t // tm),
        in_specs=[row(), row(), row(), gspec, pl.BlockSpec(w_out.shape, lambda b, i: (0, 0))],
        out_specs=row(),
        out_shape=jax.ShapeDtypeStruct((n, t, d), F32),
        compiler_params=_cparams(("parallel", "parallel")),
        name="attn_out",
    )(o, sz, x, gate, w_out)


def _pad_lanes(a):
    return jnp.pad(a, ((0, 0), (0, LANES - a.shape[-1])))


def kernel(x_prompt, x_sample, state_delta, state_qkv_conv, state_short_conv, cache_k, cache_v, page_table, c_prompt, c_sample, norm_g, ada_w, ada_b, dn_w_in, dn_conv_w, dn_a_log, dn_dt_bias, dn_norm_g, sc_conv_w, dn_w_out, att_w_in, att_qn_g, att_kn_g, att_w_out):
    bp, seq, d = x_prompt.shape
    bs, t_s, _ = x_sample.shape
    n_heads = dn_a_log.shape[-1]
    w = n_heads * HEAD_DIM
    sc_w = sc_conv_w.shape[-1]
    assert dn_w_in.shape[-1] == 4 * w + 2 * n_heads + 4 * sc_w and state_delta.shape[-2:] == (HEAD_DIM, HEAD_DIM)

    mod = _adaln(jnp.concatenate([c_prompt, c_sample], axis=0), ada_w, ada_b)
    mod_p, mod_s = mod[:, :bp], mod[:, bp:]

    wi = dn_w_in[0]
    w0 = jnp.concatenate([wi[:, :4 * w], _pad_lanes(wi[:, 4 * w:4 * w + 2 * n_heads]),
                          wi[:, 4 * w + 2 * n_heads:]], axis=1).astype(BF16)
    alog, dtb = _pad_lanes(dn_a_log[0][None]), _pad_lanes(dn_dt_bias[0][None])
    ng0 = norm_g[0][None]
    dng = dn_norm_g[0][None]
    w_out0 = dn_w_out[0].astype(BF16)
    nq, nsc = dn_conv_w.shape[1] - 1, sc_conv_w.shape[1] - 1

    q, k, v, gb, sza, yb, qkv_tail, sc_tail = _dn_in_prompt(
        x_prompt, mod_p[0][:, None], ng0, w0, dn_conv_w[0], alog, dtb, sc_conv_w[0], n_heads, sc_w)
    p_qkv, p_sc = qkv_tail[:, SUBLANES - nq:], sc_tail[:, SUBLANES - nsc:]

    tm_of = lambda a: jnp.swapaxes(a, 0, 1)
    qs, ks, vs, gbs, szas, ybs, s_qkv_tm, s_sc_tm = _dn_in_sample(
        tm_of(x_sample), mod_s[0], ng0, w0, tm_of(state_qkv_conv[0]), tm_of(state_short_conv[0]),
        dn_conv_w[0], alog, dtb, sc_conv_w[0], n_heads, sc_w)
    cs = DN_CHUNK_SAMPLE
    assert t_s <= cs
    pad_t = lambda a: jnp.pad(tm_of(a), ((0, 0), (0, cs - t_s), (0, 0)))
    o_s, s_delta = _delta_one_chunk(pad_t(qs), pad_t(ks), pad_t(vs), pad_t(gbs), state_delta[0], 8)
    rows_s = bs * t_s
    flat = lambda a: a.reshape(1, rows_s, a.shape[-1])
    gate_rows = lambda m: flat(jnp.repeat(m, t_s, axis=0))
    xs1 = _dn_out(flat(o_s[:, :t_s]), flat(tm_of(szas)), flat(tm_of(ybs)), flat(x_sample),
                  gate_rows(mod_s[0][:, 2 * d:]), dng, w_out0, rows_s)
    s_qkv, s_sc = tm_of(s_qkv_tm), tm_of(s_sc_tm)

    wa = att_w_in[0].astype(BF16)
    ng1, qn, kn = norm_g[1][None], att_qn_g[0][None], att_kn_g[0][None]
    w_out1 = att_w_out[0].astype(BF16)
    n_att_heads = d // HEAD_DIM

    q1s, k1s, v1s, sz1s = _attn_in(xs1, gate_rows(mod_s[1]), ng1, wa, qn, kn, rows_s, False)
    seqs = lambda a: a.reshape(bs, t_s, n_att_heads, HEAD_DIM)
    o1s, prep = _moba_decode_with_prep(seqs(q1s), seqs(k1s), seqs(v1s), cache_k, cache_v, page_table, 0,
                                       (q, k, v, gb), DN_CHUNK_PROMPT)
    y_sample = _attn_out(flat(o1s.reshape(bs, t_s, d)), sz1s, xs1, gate_rows(mod_s[1][:, 2 * d:]), w_out1, rows_s)

    o_p, p_delta = _delta_scan(*prep, jnp.zeros((bp, n_heads, HEAD_DIM, HEAD_DIM), F32), DN_CHUNK_PROMPT, bp)
    xp1 = _dn_out(o_p, sza, yb, x_prompt, mod_p[0][:, None, 2 * d:], dng, w_out0, PROMPT_ROW_TILE)
    q1, k1, v1, sz1, kb1, vt1, kmean = _attn_in(xp1, mod_p[1][:, None], ng1, wa, qn, kn, MOBA_BLOCK, True)
    y_prompt = _moba_prompt(q1, kb1, vt1, kmean.reshape(bp, seq // MOBA_BLOCK, d), sz1, xp1,
                            mod_p[1][:, None, 2 * d:], w_out1)

    heads = lambda a, n: a.reshape(1, n, -1, n_att_heads, HEAD_DIM)
    return (y_prompt, y_sample.reshape(bs, t_s, d),
            p_delta[None], p_qkv[None], p_sc[None], heads(k1, bp), heads(v1, bp),
            s_delta[None], s_qkv[None], s_sc[None], heads(k1s, bs), heads(v1s, bs))
```

```python
import functools

import jax
import jax.numpy as jnp
from jax import lax
from jax.experimental import pallas as pl
from jax.experimental.pallas import tpu as pltpu

F32 = jnp.float32
BF16 = jnp.bfloat16

EPS = 1e-6
LANES = 128
SUBLANES = 8
SUBLANES_BF16 = 16
HEAD_DIM = 128
DN_CHUNK_PROMPT = 64
DN_CHUNK_SAMPLE = 16
DN_SOLVE_BLOCK = 16
MOBA_BLOCK = 256
MOBA_TOPK = 3
DECODE_VMEM_LIMIT = 48 * 1024 * 1024
DECODE_VMEM_RESERVE = 12 * 1024 * 1024
MOBA_KV_GROUP = 4
PROMPT_ROW_TILE = 512
DN_OUT_ROW_TILE = 1024
DN_ONE_CHUNK_SEQS = 16
DN_SCAN_CHUNKS_PER_STEP = 4
NEG = -1e30
DN_GRAM_PASSES = 1
DN_SOLVE_PASSES = 1
DN_SCAN_PASSES = 1
VMEM_LIMIT = 48 * 1024 * 1024
MOBA_VMEM_LIMIT = 58 * 1024 * 1024


def _cparams(sem, vmem_limit=VMEM_LIMIT):
    return pltpu.CompilerParams(dimension_semantics=sem, vmem_limit_bytes=vmem_limit)


def _silu(x):
    return x * jax.nn.sigmoid(x)


def _softplus(x):
    return jnp.maximum(x, 0.0) + jnp.log1p(jnp.exp(-jnp.abs(x)))


def _rms(x, g):
    return x * lax.rsqrt(jnp.mean(x * x, axis=-1, keepdims=True) + EPS) * g


def _modulate(x, norm_g, mod):
    d = x.shape[-1]
    return _rms(x, norm_g) * (1.0 + mod[:, d:2 * d]) + mod[:, :d]


def _dot_nt(a, b, precision=None):
    return lax.dot_general(a, b, (((1,), (1,)), ((), ())), precision=precision,
                           preferred_element_type=F32)


def _dot_tn(a, b, precision=None):
    return lax.dot_general(a, b, (((0,), (0,)), ((), ())), precision=precision,
                           preferred_element_type=F32)


def _adaln_kernel(c_ref, w_ref, b_ref, o_ref):
    o_ref[0] = jnp.dot(c_ref[...], w_ref[0], preferred_element_type=F32) + b_ref[0]


def _adaln(c_all, ada_w, ada_b):
    n_layers, d, d3 = ada_w.shape
    n = c_all.shape[0]
    tn = d
    return pl.pallas_call(
        _adaln_kernel,
        grid=(n_layers, d3 // tn),
        in_specs=[pl.BlockSpec((n, d), lambda l, j: (0, 0)),
                  pl.BlockSpec((1, d, tn), lambda l, j: (l, 0, j)),
                  pl.BlockSpec((1, 1, tn), lambda l, j: (l, 0, j))],
        out_specs=pl.BlockSpec((1, n, tn), lambda l, j: (l, 0, j)),
        out_shape=jax.ShapeDtypeStruct((n_layers, n, d3), F32),
        compiler_params=_cparams(("parallel", "parallel")),
        name="adaln_mod",
    )(c_all, ada_w, ada_b.reshape(n_layers, 1, d3))


def _dn_activations(qkv, ab, alog, dtb, n_heads):
    w = n_heads * HEAD_DIM
    act = _silu(qkv)
    qs, ks = [], []
    for h in range(n_heads):
        qh = act[:, h * HEAD_DIM:(h + 1) * HEAD_DIM]
        kh = act[:, w + h * HEAD_DIM:w + (h + 1) * HEAD_DIM]
        qs.append(qh * lax.rsqrt(jnp.sum(qh * qh, axis=-1, keepdims=True) + EPS) * (HEAD_DIM ** -0.5))
        ks.append(kh * lax.rsqrt(jnp.sum(kh * kh, axis=-1, keepdims=True) + EPS))
    q = jnp.concatenate(qs, axis=-1)
    k = jnp.concatenate(ks, axis=-1)
    v = act[:, 2 * w:3 * w]
    lane = lax.broadcasted_iota(jnp.int32, ab.shape, 1)
    g = -jnp.exp(alog) * _softplus(ab + dtb)
    beta = jax.nn.sigmoid(ab)
    gb = jnp.where(lane < n_heads, g, jnp.where(lane < 2 * n_heads, beta, 0.0))
    return q, k, v, gb


def _shift_rows(cur, prev, s):
    ext = jnp.concatenate([prev, cur], axis=0)
    return ext[SUBLANES - s:SUBLANES - s + cur.shape[0]]


def _conv_rows(cur, prev, w):
    width = w.shape[0]
    y = cur * w[width - 1:width]
    for s in range(1, width):
        y = y + _shift_rows(cur, prev, s) * w[width - 1 - s:width - s]
    return y


def _dn_in_prompt_kernel(x_ref, mod_ref, ng_ref, w_ref, cw_ref, alog_ref, dtb_ref, scw_ref,
                         q_ref, k_ref, v_ref, gb_ref, sza_ref, yb_ref, qkvt_ref, sct_ref,
                         prev_qkv, prev_sc, *, n_heads, sc_w):
    i = pl.program_id(1)
    w = n_heads * HEAD_DIM
    o_z, o_ab, o_sc = 3 * w, 4 * w, 4 * w + LANES

    @pl.when(i == 0)
    def _():
        prev_qkv[...] = jnp.zeros_like(prev_qkv)
        prev_sc[...] = jnp.zeros_like(prev_sc)

    h = _modulate(x_ref[0], ng_ref[...], mod_ref[0]).astype(BF16)
    raw = jnp.dot(h, w_ref[:, 0:o_z], preferred_element_type=F32)
    qkv = _conv_rows(raw, prev_qkv[...], cw_ref[...])
    ab = jnp.dot(h, w_ref[:, o_ab:o_sc], preferred_element_type=F32)
    q, k, v, gb = _dn_activations(qkv, ab, alog_ref[...], dtb_ref[...], n_heads)
    q_ref[0], k_ref[0], v_ref[0], gb_ref[0] = q, k, v, gb
    prev_qkv[...] = raw[-SUBLANES:]
    qkvt_ref[0] = raw[-SUBLANES:]

    sza_ref[0] = _silu(jnp.dot(h, w_ref[:, o_z:o_ab], preferred_element_type=F32)).astype(BF16)

    sc = jnp.dot(h, w_ref[:, o_sc:o_sc + 4 * sc_w], preferred_element_type=F32)
    cx = sc[:, sc_w:2 * sc_w] * sc[:, 2 * sc_w:3 * sc_w]
    cv = _conv_rows(cx, prev_sc[...], scw_ref[...])
    yb_ref[0] = (sc[:, 0:sc_w] * cv * _silu(sc[:, 3 * sc_w:4 * sc_w])).astype(BF16)
    prev_sc[...] = cx[-SUBLANES:]
    sct_ref[0] = cx[-SUBLANES:]


def _dn_in_prompt(x, mod, norm_g, w0, conv_w, alog, dtb, sc_conv_w, n_heads, sc_w):
    n, t, d = x.shape
    tm = PROMPT_ROW_TILE
    w = n_heads * HEAD_DIM
    wtot = w0.shape[1]
    row = lambda c: pl.BlockSpec((1, tm, c), lambda b, i: (b, i, 0))
    full = lambda a: pl.BlockSpec(a.shape, lambda b, i: (0,) * a.ndim)
    tail = lambda c: pl.BlockSpec((1, SUBLANES, c), lambda b, i: (b, 0, 0))
    outs = [jax.ShapeDtypeStruct((n, t, w), F32)] * 3 + [
        jax.ShapeDtypeStruct((n, t, LANES), F32), jax.ShapeDtypeStruct((n, t, w), BF16),
        jax.ShapeDtypeStruct((n, t, sc_w), BF16),
        jax.ShapeDtypeStruct((n, SUBLANES, 3 * w), F32), jax.ShapeDtypeStruct((n, SUBLANES, sc_w), F32)]
    return pl.pallas_call(
        functools.partial(_dn_in_prompt_kernel, n_heads=n_heads, sc_w=sc_w),
        grid=(n, t // tm),
        in_specs=[row(d), pl.BlockSpec((1, 1, 3 * d), lambda b, i: (b, 0, 0)), full(norm_g),
                  pl.BlockSpec((d, wtot), lambda b, i: (0, 0)), full(conv_w), full(alog), full(dtb),
                  full(sc_conv_w)],
        out_specs=[row(w), row(w), row(w), row(LANES), row(w), row(sc_w), tail(3 * w), tail(sc_w)],
        out_shape=outs,
        scratch_shapes=[pltpu.VMEM((SUBLANES, 3 * w), F32), pltpu.VMEM((SUBLANES, sc_w), F32)],
        compiler_params=_cparams(("parallel", "arbitrary")),
        name="dn_in_prompt",
    )(x, mod, norm_g, w0, conv_w, alog, dtb, sc_conv_w)


def _dn_in_sample_kernel(x_ref, mod_ref, ng_ref, w_ref, sq_ref, ssc_ref, cw_ref, alog_ref, dtb_ref,
                         scw_ref, q_ref, k_ref, v_ref, gb_ref, sza_ref, yb_ref, nq_ref, nsc_ref,
                         *, n_heads, sc_w):
    t_len = x_ref.shape[0]
    w = n_heads * HEAD_DIM
    o_z, o_ab, o_sc = 3 * w, 4 * w, 4 * w + LANES
    cw, scw = cw_ref[...], scw_ref[...]
    nq, nsc = cw.shape[0] - 1, scw.shape[0] - 1
    xq = [sq_ref[j] for j in range(nq)]
    xsc = [ssc_ref[j] for j in range(nsc)]
    hs, sc_all = [], []
    for t in range(t_len):
        h = _modulate(x_ref[t], ng_ref[...], mod_ref[...]).astype(BF16)
        hs.append(h)
        xq.append(jnp.dot(h, w_ref[:, 0:o_z], preferred_element_type=F32))
        sc = jnp.dot(h, w_ref[:, o_sc:o_sc + 4 * sc_w], preferred_element_type=F32)
        sc_all.append(sc)
        xsc.append(sc[:, sc_w:2 * sc_w] * sc[:, 2 * sc_w:3 * sc_w])
    for t in range(t_len):
        qkv = sum(xq[t + j] * cw[j:j + 1] for j in range(nq + 1))
        ab = jnp.dot(hs[t], w_ref[:, o_ab:o_sc], preferred_element_type=F32)
        q, k, v, gb = _dn_activations(qkv, ab, alog_ref[...], dtb_ref[...], n_heads)
        q_ref[t], k_ref[t], v_ref[t], gb_ref[t] = q, k, v, gb
        sza_ref[t] = _silu(jnp.dot(hs[t], w_ref[:, o_z:o_ab], preferred_element_type=F32)).astype(BF16)
        cv = sum(xsc[t + j] * scw[j:j + 1] for j in range(nsc + 1))
        sc = sc_all[t]
        yb_ref[t] = (sc[:, 0:sc_w] * cv * _silu(sc[:, 3 * sc_w:4 * sc_w])).astype(BF16)
    for j in range(nq):
        nq_ref[j] = xq[t_len + j]
    for j in range(nsc):
        nsc_ref[j] = xsc[t_len + j]


def _dn_in_sample(x_tm, mod, norm_g, w0, s_qkv_tm, s_sc_tm, conv_w, alog, dtb, sc_conv_w, n_heads, sc_w):
    t, n, d = x_tm.shape
    w = n_heads * HEAD_DIM
    gs = min(n, 128)
    wtot = w0.shape[1]
    tm3 = lambda r, c: pl.BlockSpec((r, gs, c), lambda g: (0, g, 0))
    full = lambda a: pl.BlockSpec(a.shape, lambda g: (0,) * a.ndim)
    nq, nsc = conv_w.shape[0] - 1, sc_conv_w.shape[0] - 1
    outs = [jax.ShapeDtypeStruct((t, n, w), F32)] * 3 + [
        jax.ShapeDtypeStruct((t, n, LANES), F32), jax.ShapeDtypeStruct((t, n, w), BF16),
        jax.ShapeDtypeStruct((t, n, sc_w), BF16),
        jax.ShapeDtypeStruct((nq, n, 3 * w), F32), jax.ShapeDtypeStruct((nsc, n, sc_w), F32)]
    return pl.pallas_call(
        functools.partial(_dn_in_sample_kernel, n_heads=n_heads, sc_w=sc_w),
        grid=(n // gs,),
        in_specs=[tm3(t, d), pl.BlockSpec((gs, 3 * d), lambda g: (g, 0)), full(norm_g),
                  pl.BlockSpec((d, wtot), lambda g: (0, 0)), tm3(nq, 3 * w), tm3(nsc, sc_w),
                  full(conv_w), full(alog), full(dtb), full(sc_conv_w)],
        out_specs=[tm3(t, w), tm3(t, w), tm3(t, w), tm3(t, LANES), tm3(t, w), tm3(t, sc_w),
                   tm3(nq, 3 * w), tm3(nsc, sc_w)],
        out_shape=outs,
        compiler_params=_cparams(("parallel",)),
        name="dn_in_sample",
    )(x_tm, mod, norm_g, w0, s_qkv_tm, s_sc_tm, conv_w, alog, dtb, sc_conv_w)


def _bf16_parts(x, n):
    parts, r = [], x
    for i in range(n):
        p = r.astype(BF16)
        parts.append(p)
        if i + 1 < n:
            r = r - p.astype(F32)
    return parts


def _mm(dot, a, b, passes):
    if passes == 1:
        return dot(a.astype(BF16), b.astype(BF16))
    a_hi, a_lo = _bf16_parts(a, 2)
    b_hi, b_lo = _bf16_parts(b, 2)
    return dot(a_hi, b_hi) + (dot(a_hi, b_lo) + dot(a_lo, b_hi))


def _mm_exact01(dot, sel, x):
    hi, mid, lo = (p.astype(sel.dtype) for p in _bf16_parts(x, 3))
    return dot(sel, hi) + (dot(sel, mid) + dot(sel, lo))


def _bdot(a, b):
    return jnp.einsum("bij,bjk->bik", a, b, preferred_element_type=F32)


def _bdot_nt(a, b):
    return jnp.einsum("bik,bjk->bij", a, b, preferred_element_type=F32)


def _dot(a, b):
    return jnp.dot(a, b, preferred_element_type=F32)


def _neumann_inverse(m, order, eye):
    p = eye - m
    mk, k = m, 1
    while 2 * k < order:
        mk = _mm(_bdot, mk, mk, DN_SOLVE_PASSES)
        yield
        p = p + _mm(_bdot, p, mk, DN_SOLVE_PASSES)
        yield
        k *= 2
    return p


def _unit_lower_inverse(m, c):
    row = lax.broadcasted_iota(jnp.int32, (c, c), 0)
    col = lax.broadcasted_iota(jnp.int32, (c, c), 1)
    eye = (row == col).astype(F32)
    if c <= DN_SOLVE_BLOCK:
        return (yield from _neumann_inverse(m, c, eye))
    on_diag = (row // DN_SOLVE_BLOCK) == (col // DN_SOLVE_BLOCK)
    m_diag = jnp.where(on_diag, m, 0.0)
    d_inv = yield from _neumann_inverse(m_diag, DN_SOLVE_BLOCK, eye)
    n_off = _mm(_bdot, d_inv, m - m_diag, DN_SOLVE_PASSES)
    yield
    n_inv = yield from _neumann_inverse(n_off, c // DN_SOLVE_BLOCK, eye)
    t_inv = _mm(_bdot, n_inv, d_inv, DN_SOLVE_PASSES)
    yield
    return t_inv


def _delta_prep_kernel(*refs, c):
    for _ in _delta_prep_stages(*refs, c=c):
        pass


def _delta_prep_stages(q_ref, k_ref, v_ref, gb_ref, u_ref, w_ref, qg_ref, kd_ref, el_ref, a_ref, *, c):
    g_dim, r_dim, wq = q_ref.shape
    n_heads = wq // HEAD_DIM
    rows = g_dim * r_dim
    nb = rows // c
    row = lax.broadcasted_iota(jnp.int32, (c, c), 0)
    col = lax.broadcasted_iota(jnp.int32, (c, c), 1)
    gb2 = gb_ref[...].reshape(rows, LANES)
    tril = jnp.broadcast_to((row >= col).astype(BF16), (nb, c, c))
    gc3 = _mm_exact01(_bdot, tril, gb2.reshape(nb, c, LANES))
    gc2 = gc3.reshape(rows, LANES)
    gt2 = _mm_exact01(_bdot, jnp.ones((nb, c, c), BF16), gb2.reshape(nb, c, LANES)).reshape(rows, LANES)
    yield
    hs = lambda h: slice(h * HEAD_DIM, (h + 1) * HEAD_DIM)
    stack = lambda f: jnp.concatenate([f(h) for h in range(n_heads)], axis=0)
    gch = stack(lambda h: jnp.broadcast_to(gc2[:, h:h + 1], (rows, LANES)).reshape(nb, c, LANES))
    beta = stack(lambda h: jnp.broadcast_to(gb2[:, n_heads + h:n_heads + h + 1], (rows, LANES)).reshape(nb, c, LANES))
    diag_gc = jnp.where(row == col, gch[:, :, :c], 0.0)
    gc_cols = _mm_exact01(_bdot, jnp.ones(diag_gc.shape, BF16), diag_gc)
    qh = stack(lambda h: q_ref[:, :, hs(h)].reshape(nb, c, HEAD_DIM))
    kh = stack(lambda h: k_ref[:, :, hs(h)].reshape(nb, c, HEAD_DIM))
    vh = stack(lambda h: v_ref[:, :, hs(h)].reshape(nb, c, HEAD_DIM))
    kb = kh * beta
    kk = _mm(_bdot_nt, kb, kh, DN_GRAM_PASSES)
    qk = _mm(_bdot_nt, qh, kh, DN_GRAM_PASSES)
    yield
    gc_last = stack(lambda h: jnp.broadcast_to(gt2[:, h:h + 1], (rows, LANES)).reshape(nb, c, LANES))
    decay = jnp.where(row >= col, jnp.exp(gch[:, :, :c] - gc_cols), 0.0)
    m = jnp.where(row > col, kk * decay, 0.0)
    a = qk * decay
    t_inv = yield from _unit_lower_inverse(m, c)
    egc = jnp.exp(gch)
    sol = _mm(_bdot, t_inv, jnp.concatenate([vh * beta, kb * egc], axis=-1), DN_SOLVE_PASSES)
    yield
    qg, kd, el = qh * egc, kh * jnp.exp(gc_last - gch), jnp.exp(gc_last)
    for h in range(n_heads):
        of_head = lambda x: x[h * nb:(h + 1) * nb].reshape(g_dim, r_dim, x.shape[-1])
        u_ref[:, :, hs(h)] = of_head(sol[:, :, :HEAD_DIM])
        el_ref[:, :, hs(h)] = of_head(el)
        w_ref[:, :, hs(h)] = of_head(sol[:, :, HEAD_DIM:]).astype(w_ref.dtype)
        qg_ref[:, :, hs(h)] = of_head(qg).astype(qg_ref.dtype)
        kd_ref[:, :, hs(h)] = of_head(kd).astype(kd_ref.dtype)
        a_ref[:, :, h * c:(h + 1) * c] = of_head(a).astype(a_ref.dtype)


def _delta_scan_kernel(u_ref, w_ref, qg_ref, kd_ref, el_ref, a_ref, s0_ref, o_ref, s_ref, *, c):
    @pl.when(pl.program_id(1) == 0)
    def _():
        s_ref[...] = s0_ref[...]

    for j in range(u_ref.shape[1] // c):
        _delta_scan_step(u_ref, w_ref, qg_ref, kd_ref, el_ref, a_ref, o_ref, s_ref, c=c, row0=j * c)


def _delta_one_chunk_kernel(q_ref, k_ref, v_ref, gb_ref, s0_ref, o_ref, s_ref, u, w, qg, kd, el, a, *, c):
    _delta_prep_kernel(q_ref, k_ref, v_ref, gb_ref, u, w, qg, kd, el, a, c=c)
    s_ref[...] = s0_ref[...]
    _delta_scan_step(u, w, qg, kd, el, a, o_ref, s_ref, c=c)


def _delta_scan_step(u_ref, w_ref, qg_ref, kd_ref, el_ref, a_ref, o_ref, s_ref, *, c, row0=0):
    g_dim, _, wq = u_ref.shape
    rs = slice(row0, row0 + c)
    chains = [(g, h, slice(h * HEAD_DIM, (h + 1) * HEAD_DIM)) for g in range(g_dim) for h in range(wq // HEAD_DIM)]
    states = [s_ref[g, h] for g, h, _ in chains]
    v_new = [u_ref[g, rs, sl] - _mm(_dot, w_ref[g, rs, sl], s, DN_SCAN_PASSES)
             for (g, _, sl), s in zip(chains, states)]
    o_inter = [_mm(_dot, qg_ref[g, rs, sl], s, DN_SCAN_PASSES) for (g, _, sl), s in zip(chains, states)]
    for (g, h, sl), s, vn, oi in zip(chains, states, v_new, o_inter):
        o_ref[g, rs, sl] = oi + _mm(_dot, a_ref[g, rs, h * c:(h + 1) * c], vn, DN_SCAN_PASSES)
    for (g, h, sl), s, vn in zip(chains, states, v_new):
        s_ref[g, h] = s * el_ref[g, row0:row0 + 1, sl] + _mm(_dot_tn, kd_ref[g, rs, sl], vn, DN_SCAN_PASSES)


def _delta_scan(u, w, qg, kd, el, a, s0, c, g_dim):
    n, t, wq = u.shape
    n_heads = wq // HEAD_DIM
    rows = c * DN_SCAN_CHUNKS_PER_STEP
    assert t % rows == 0
    blk = lambda cc: pl.BlockSpec((g_dim, rows, cc), lambda b, i: (b, i, 0))
    st = pl.BlockSpec((g_dim, n_heads, HEAD_DIM, HEAD_DIM), lambda b, i: (b, 0, 0, 0))
    return pl.pallas_call(
        functools.partial(_delta_scan_kernel, c=c),
        grid=(n // g_dim, t // rows),
        in_specs=[blk(wq)] * 5 + [blk(n_heads * c), st],
        out_specs=[blk(wq), st],
        out_shape=[jax.ShapeDtypeStruct((n, t, wq), F32),
                   jax.ShapeDtypeStruct((n, n_heads, HEAD_DIM, HEAD_DIM), F32)],
        compiler_params=_cparams(("parallel", "arbitrary")),
        name="delta_scan",
    )(u, w, qg, kd, el, a, s0)


def _delta_one_chunk(q, k, v, gb, s0, g_dim):
    n, c, wq = q.shape
    n_heads = wq // HEAD_DIM
    blk = lambda cc: pl.BlockSpec((g_dim, c, cc), lambda b: (b, 0, 0))
    st = pl.BlockSpec((g_dim, n_heads, HEAD_DIM, HEAD_DIM), lambda b: (b, 0, 0, 0))
    operand = BF16 if DN_SCAN_PASSES == 1 else F32
    tmp = lambda cc, dt: pltpu.VMEM((g_dim, c, cc), dt)
    return pl.pallas_call(
        functools.partial(_delta_one_chunk_kernel, c=c),
        grid=(n // g_dim,),
        in_specs=[blk(wq), blk(wq), blk(wq), blk(LANES), st],
        out_specs=[blk(wq), st],
        out_shape=[jax.ShapeDtypeStruct((n, c, wq), F32),
                   jax.ShapeDtypeStruct((n, n_heads, HEAD_DIM, HEAD_DIM), F32)],
        scratch_shapes=[tmp(wq, F32), tmp(wq, operand), tmp(wq, operand), tmp(wq, operand), tmp(wq, F32),
                        tmp(n_heads * c, operand)],
        compiler_params=_cparams(("parallel",)),
        name="delta_one_chunk",
    )(q, k, v, gb, s0)


def _dn_out_kernel(o_ref, sza_ref, yb_ref, x_ref, gate_ref, ng_ref, w_ref, y_ref):
    o = o_ref[0]
    wa = o.shape[-1]
    ya = jnp.concatenate([_rms(o[:, h * HEAD_DIM:(h + 1) * HEAD_DIM], ng_ref[...])
                          for h in range(wa // HEAD_DIM)], axis=-1) * sza_ref[0]
    out = (jnp.dot(ya.astype(BF16), w_ref[0:wa, :], preferred_element_type=F32)
           + jnp.dot(yb_ref[0].astype(BF16), w_ref[wa:, :], preferred_element_type=F32))
    y_ref[0] = x_ref[0] + gate_ref[0] * out


def _dn_out(o, sza, yb, x, gate, dn_norm_g, w_out, tm):
    n, t, d = x.shape
    wa, wb = o.shape[-1], yb.shape[-1]
    row = lambda c: pl.BlockSpec((1, tm, c), lambda b, i: (b, i, 0))
    gate_rows = gate.shape[1]
    gspec = (pl.BlockSpec((1, 1, d), lambda b, i: (b, 0, 0)) if gate_rows == 1
             else pl.BlockSpec((1, tm, d), lambda b, i: (b, i, 0)))
    return pl.pallas_call(
        _dn_out_kernel,
        grid=(n, t // tm),
        in_specs=[row(wa), row(wa), row(wb), row(d), gspec,
                  pl.BlockSpec(dn_norm_g.shape, lambda b, i: (0, 0)),
                  pl.BlockSpec(w_out.shape, lambda b, i: (0, 0))],
        out_specs=row(d),
        out_shape=jax.ShapeDtypeStruct((n, t, d), F32),
        compiler_params=_cparams(("parallel", "parallel")),
        name="dn_out",
    )(o, sza, yb, x, gate, dn_norm_g, w_out)


def _attn_in_kernel(x_ref, mod_ref, ng_ref, w_ref, qn_ref, kn_ref, q_ref, k_ref, v_ref, sz_ref, *prompt_refs):
    d = x_ref.shape[-1]
    h = _modulate(x_ref[0], ng_ref[...], mod_ref[0]).astype(BF16)
    heads = range(d // HEAD_DIM)
    qr = jnp.dot(h, w_ref[:, 0:d], preferred_element_type=F32)
    q_ref[0] = jnp.concatenate([_rms(qr[:, i * HEAD_DIM:(i + 1) * HEAD_DIM], qn_ref[...]) for i in heads], axis=-1)
    kr = jnp.dot(h, w_ref[:, d:2 * d], preferred_element_type=F32)
    k = jnp.concatenate([_rms(kr[:, i * HEAD_DIM:(i + 1) * HEAD_DIM], kn_ref[...]) for i in heads], axis=-1)
    k_ref[0] = k
    v = jnp.dot(h, w_ref[:, 2 * d:3 * d], preferred_element_type=F32)
    v_ref[0] = v
    sz_ref[0] = _silu(jnp.dot(h, w_ref[:, 3 * d:4 * d], preferred_element_type=F32)).astype(BF16)
    if prompt_refs:
        kb_ref, vt_ref, kmean_ref = prompt_refs
        kb_ref[0] = k.astype(BF16)
        vt_ref[0] = v.T.astype(BF16)
        for j in range(kmean_ref.shape[1]):
            kmean_ref[0, j] = jnp.mean(k[j * MOBA_BLOCK:(j + 1) * MOBA_BLOCK], axis=0, keepdims=True)


def _attn_in(x, mod, norm_g, w_in, qn_g, kn_g, tm, for_prompt):
    n, t, d = x.shape
    row = lambda: pl.BlockSpec((1, tm, d), lambda b, i: (b, i, 0))
    mod_rows = mod.shape[1]
    mspec = (pl.BlockSpec((1, 1, 3 * d), lambda b, i: (b, 0, 0)) if mod_rows == 1
             else pl.BlockSpec((1, tm, 3 * d), lambda b, i: (b, i, 0)))
    full = lambda a: pl.BlockSpec(a.shape, lambda b, i: (0,) * a.ndim)
    out_specs = [row()] * 4
    outs = [jax.ShapeDtypeStruct((n, t, d), F32)] * 3 + [jax.ShapeDtypeStruct((n, t, d), BF16)]
    if for_prompt:
        assert tm % MOBA_BLOCK == 0
        out_specs += [row(), pl.BlockSpec((1, d, tm), lambda b, i: (b, 0, i)),
                      pl.BlockSpec((1, tm // MOBA_BLOCK, 1, d), lambda b, i: (b, i, 0, 0))]
        outs += [jax.ShapeDtypeStruct((n, t, d), BF16), jax.ShapeDtypeStruct((n, d, t), BF16),
                 jax.ShapeDtypeStruct((n, t // MOBA_BLOCK, 1, d), F32)]
    return pl.pallas_call(
        _attn_in_kernel,
        grid=(n, t // tm),
        in_specs=[row(), mspec, full(norm_g), full(w_in), full(qn_g), full(kn_g)],
        out_specs=out_specs,
        out_shape=outs,
        compiler_params=_cparams(("parallel", "parallel")),
        name="attn_in",
    )(x, mod, norm_g, w_in, qn_g, kn_g)


def _topk_select(gate, cand, n_blocks, axis):
    blk = lax.broadcasted_iota(jnp.int32, gate.shape, axis)
    gm = jnp.where(cand, gate, -jnp.inf)
    rank = jnp.zeros(gate.shape, jnp.int32)
    for b in range(n_blocks):
        other = gm[b:b + 1, :] if axis == 0 else gm[:, b:b + 1]
        rank = rank + ((other > gm) | ((other == gm) & (b < blk))).astype(jnp.int32)
    return cand & (rank < MOBA_TOPK)


def _moba_prompt_kernel(q_ref, k_ref, vt_ref, kmean_ref, sz_ref, x_ref, gate_ref, w_ref, y_ref, acc_scr, sel_scr,
                        *, group):
    i = pl.program_id(1)
    blk = q_ref.shape[1]
    n_blocks = kmean_ref.shape[1]
    heads = range(q_ref.shape[2] // HEAD_DIM)
    hs = lambda h: slice(h * HEAD_DIM, (h + 1) * HEAD_DIM)
    start = pl.multiple_of(i * blk, blk)
    causal = (lax.broadcasted_iota(jnp.int32, (blk, blk), 0) <= lax.broadcasted_iota(jnp.int32, (blk, blk), 1))

    def values_and_ones(h, st, n_keys):
        return jnp.concatenate([vt_ref[0, hs(h), pl.ds(st, n_keys)], jnp.ones((SUBLANES_BF16, n_keys), BF16)], axis=0)

    qfs = [q_ref[0, :, hs(h)] for h in heads]
    qbs = [(qf * (HEAD_DIM ** -0.5)).astype(BF16) for qf in qfs]
    s_own = [_dot_nt(k_ref[0, pl.ds(start, blk), hs(h)], qbs[h]).astype(BF16) for h in heads]
    gates = [_dot_nt(kmean_ref[0, :, hs(h)].astype(BF16), qfs[h].astype(BF16)) for h in heads]
    m0, p_own = [], []
    for h in heads:
        s = jnp.where(causal, s_own[h], NEG)
        m = jnp.max(s, axis=0, keepdims=True)
        m0.append(m.astype(F32))
        p_own.append(jnp.exp(s - m))
    for h in heads:
        acc_scr[h] = jnp.dot(values_and_ones(h, start, blk), p_own[h], preferred_element_type=F32)
        cand = lax.broadcasted_iota(jnp.int32, gates[h].shape, 0) < i
        sel_scr[h] = _topk_select(gates[h], cand, n_blocks, 0).astype(F32)

    def past_group(jg, ms):
        st = pl.multiple_of(jg * (group * blk), group * blk)
        sgs = [_dot_nt(k_ref[0, pl.ds(st, group * blk), hs(h)], qbs[h]).astype(BF16) for h in heads]
        new_m, pjs = [], []
        for h in heads:
            sj = [jnp.where(sel_scr[h, pl.ds(jg * group + g, 1), :] > 0.5, sgs[h][g * blk:(g + 1) * blk], NEG)
                  for g in range(group)]
            m_blk = functools.reduce(jnp.maximum, [jnp.max(x, axis=0, keepdims=True) for x in sj])
            m_new = jnp.maximum(ms[h], m_blk.astype(F32))
            m_b16 = m_new.astype(BF16)
            pjs.append(jnp.concatenate([jnp.exp(x - m_b16) for x in sj], axis=0))
            new_m.append(m_new)
        for h in heads:
            acc_scr[h] = (jnp.exp(ms[h] - new_m[h]) * acc_scr[h]
                          + jnp.dot(values_and_ones(h, st, group * blk), pjs[h], preferred_element_type=F32))
        return tuple(new_m)

    lax.fori_loop(0, lax.div(i + (group - 1), group), past_group, tuple(m0))
    o = jnp.concatenate([(acc_scr[h][:HEAD_DIM] / acc_scr[h][HEAD_DIM:HEAD_DIM + 1]).T for h in heads], axis=1)
    out = jnp.dot((o * sz_ref[0]).astype(BF16), w_ref[...], preferred_element_type=F32)
    y_ref[0] = x_ref[0] + gate_ref[0] * out


def _moba_prompt(q, kb, vt, kmean, sz, x, gate, w_out):
    n, t, d = q.shape
    n_blocks = t // MOBA_BLOCK
    n_heads = d // HEAD_DIM
    group = max(g for g in range(1, MOBA_KV_GROUP + 1) if n_blocks % g == 0)
    row = pl.BlockSpec((1, MOBA_BLOCK, d), lambda b, i: (b, i, 0))
    once = dict(pipeline_mode=pl.Buffered(1))
    return pl.pallas_call(
        functools.partial(_moba_prompt_kernel, group=group),
        grid=(n, n_blocks),
        in_specs=[row, pl.BlockSpec((1, t, d), lambda b, i: (b, 0, 0), **once),
                  pl.BlockSpec((1, d, t), lambda b, i: (b, 0, 0), **once),
                  pl.BlockSpec((1, n_blocks, d), lambda b, i: (b, 0, 0)),
                  row, row, pl.BlockSpec((1, 1, d), lambda b, i: (b, 0, 0)),
                  pl.BlockSpec((d, d), lambda b, i: (0, 0), **once)],
        out_specs=row,
        out_shape=jax.ShapeDtypeStruct((n, t, d), F32),
        scratch_shapes=[pltpu.VMEM((n_heads, HEAD_DIM + SUBLANES_BF16, MOBA_BLOCK), F32),
                        pltpu.VMEM((n_heads, n_blocks, MOBA_BLOCK), F32)],
        compiler_params=_cparams(("parallel", "arbitrary"), MOBA_VMEM_LIMIT),
        name="moba_prompt",
    )(q, kb, vt, kmean, sz, x, gate, w_out)


def _moba_decode_kernel(pt_ref, q_ref, kn_ref, vn_ref, *refs, n_pages, pages_per_block, prep_chunk):
    del pt_ref
    k_refs, v_refs = refs[:n_pages], refs[n_pages:2 * n_pages]
    prep_in, o_ref, prep_out = refs[2 * n_pages:2 * n_pages + 4], refs[2 * n_pages + 4], refs[2 * n_pages + 5:]
    nb = n_pages // pages_per_block
    t_len, n_heads = q_ref.shape[1], q_ref.shape[2]
    rows = t_len * n_heads
    page = k_refs[0].shape[2]
    cols = page * n_heads
    qall = q_ref[0].reshape(rows, HEAD_DIM)
    qs = qall * (HEAD_DIM ** -0.5)
    same_head = (lax.broadcasted_iota(jnp.int32, (rows, cols), 1) % n_heads
                 == lax.broadcasted_iota(jnp.int32, (rows, cols), 0) % n_heads)
    rep = lambda col: jnp.broadcast_to(col, (rows, LANES))
    per_row = lambda a: jnp.broadcast_to(a[None], (t_len, n_heads, HEAD_DIM)).reshape(rows, HEAD_DIM)

    prep = _delta_prep_stages(*prep_in, *prep_out, c=prep_chunk)
    scores = []
    for j, r in enumerate(k_refs):
        scores.append(_dot_nt(qs, r[0, 0].reshape(cols, HEAD_DIM)))
        if j % pages_per_block == pages_per_block - 1:
            next(prep, None)
    m_blk, l_blk, gate, probs = [], [], [], []
    for b in range(nb):
        pages = range(b * pages_per_block, (b + 1) * pages_per_block)
        s = [jnp.where(same_head, scores[j], NEG) for j in pages]
        m_b = functools.reduce(jnp.maximum, [jnp.max(x, axis=1, keepdims=True) for x in s])
        p = [jnp.exp(x - m_b) for x in s]
        probs.append(p)
        m_blk.append(rep(m_b))
        l_blk.append(rep(sum(jnp.sum(x, axis=1, keepdims=True) for x in p)))
        kmean = sum(jnp.sum(k_refs[j][0, 0], axis=0) for j in pages) / (page * pages_per_block)
        gate.append(rep(jnp.sum(qall * per_row(kmean), axis=1, keepdims=True)))
    acc_blk = []
    for b in range(nb):
        acc_blk.append(sum(jnp.dot(x, v_refs[b * pages_per_block + j][0, 0].reshape(cols, HEAD_DIM),
                                   preferred_element_type=F32) for j, x in enumerate(probs[b])))
        next(prep, None)
    for _ in prep:
        pass


    sel = []
    for b in range(nb):
        beats = [(gate[o] >= gate[b]) if o < b else (gate[o] > gate[b]) for o in range(nb) if o != b]
        sel.append(sum(x.astype(jnp.int32) for x in beats) < MOBA_TOPK)
    tok = lax.broadcasted_iota(jnp.int32, (rows, LANES), 0) // n_heads
    s_own = [rep(jnp.sum(qs * per_row(kn_ref[0, j]), axis=1, keepdims=True)) for j in range(t_len)]
    m_all = functools.reduce(jnp.maximum, [jnp.where(tok >= j, s_own[j], NEG) for j in range(t_len)]
                             + [jnp.where(sel[b], m_blk[b], NEG) for b in range(nb)])
    den = jnp.zeros((rows, LANES), F32)
    num = jnp.zeros((rows, HEAD_DIM), F32)
    for j in range(t_len):
        pj = jnp.where(tok >= j, jnp.exp(s_own[j] - m_all), 0.0)
        den = den + pj
        num = num + pj * per_row(vn_ref[0, j])
    for b in range(nb):
        wb = jnp.where(sel[b], jnp.exp(m_blk[b] - m_all), 0.0)
        den = den + wb * l_blk[b]
        num = num + wb * acc_blk[b]
    o_ref[0] = (num / den).reshape(t_len, n_heads, HEAD_DIM)


def _moba_decode_with_prep(q, k_new, v_new, cache_k, cache_v, page_table, layer, prep_args, prep_chunk):
    n, t_len, n_heads, _ = q.shape
    n_pages = page_table.shape[1]
    page = cache_k.shape[2]
    assert MOBA_BLOCK % page == 0 and (n_pages * page) % MOBA_BLOCK == 0 and t_len <= MOBA_BLOCK
    page_bytes = page * n_heads * HEAD_DIM * cache_k.dtype.itemsize
    assert 2 * 2 * n_pages * page_bytes <= DECODE_VMEM_LIMIT - DECODE_VMEM_RESERVE
    tok = pl.BlockSpec((1, t_len, n_heads, HEAD_DIM), lambda s, pt: (s, 0, 0, 0))

    def page_spec(j):
        return pl.BlockSpec((1, 1, page, n_heads, HEAD_DIM), lambda s, pt: (layer, pt[s, j], 0, 0, 0))

    pn, pt_len, wq = prep_args[0].shape
    share = (pn * pt_len) // n
    assert share * n == pn * pt_len and share % prep_chunk == 0 and pt_len % share == 0
    dn_heads = wq // HEAD_DIM
    flat = lambda a: a.reshape(1, pn * pt_len, a.shape[-1])
    rows = lambda c: pl.BlockSpec((1, share, c), lambda s, pt: (0, s, 0))
    operand = BF16 if DN_SCAN_PASSES == 1 else F32
    widths = [(wq, F32), (wq, operand), (wq, operand), (wq, operand), (wq, F32), (dn_heads * prep_chunk, operand)]
    grid_spec = pltpu.PrefetchScalarGridSpec(
        num_scalar_prefetch=1,
        grid=(n,),
        in_specs=[tok, tok, tok] + [page_spec(j) for j in range(n_pages)] * 2 + [rows(wq)] * 3 + [rows(LANES)],
        out_specs=[tok] + [rows(c) for c, _ in widths])
    out = pl.pallas_call(
        functools.partial(_moba_decode_kernel, n_pages=n_pages, pages_per_block=MOBA_BLOCK // page,
                          prep_chunk=prep_chunk),
        grid_spec=grid_spec,
        out_shape=[jax.ShapeDtypeStruct((n, t_len, n_heads, HEAD_DIM), F32)]
                  + [jax.ShapeDtypeStruct((1, pn * pt_len, c), dt) for c, dt in widths],
        compiler_params=_cparams(("parallel",), DECODE_VMEM_LIMIT),
        name="moba_decode",
    )(page_table, q, k_new, v_new, *([cache_k] * n_pages), *([cache_v] * n_pages), *[flat(a) for a in prep_args])
    return out[0], [a.reshape(pn, pt_len, a.shape[-1]) for a in out[1:]]


def _attn_out_kernel(o_ref, sz_ref, x_ref, gate_ref, w_ref, y_ref):
    out = jnp.dot((o_ref[0] * sz_ref[0]).astype(BF16), w_ref[...], preferred_element_type=F32)
    y_ref[0] = x_ref[0] + gate_ref[0] * out


def _attn_out(o, sz, x, gate, w_out, tm):
    n, t, d = x.shape
    row = lambda: pl.BlockSpec((1, tm, d), lambda b, i: (b, i, 0))
    gspec = (pl.BlockSpec((1, 1, d), lambda b, i: (b, 0, 0)) if gate.shape[1] == 1 else row())
    return pl.pallas_call(
        _attn_out_kernel,
        grid=(n, t // tm),
        in_specs=[row(), row(), row(), gspec, pl.BlockSpec(w_out.shape, lambda b, i: (0, 0))],
        out_specs=row(),
        out_shape=jax.ShapeDtypeStruct((n, t, d), F32),
        compiler_params=_cparams(("parallel", "parallel")),
        name="attn_out",
    )(o, sz, x, gate, w_out)


def _pad_lanes(a):
    return jnp.pad(a, ((0, 0), (0, LANES - a.shape[-1])))


def kernel(x_prompt, x_sample, state_delta, state_qkv_conv, state_short_conv, cache_k, cache_v, page_table, c_prompt, c_sample, norm_g, ada_w, ada_b, dn_w_in, dn_conv_w, dn_a_log, dn_dt_bias, dn_norm_g, sc_conv_w, dn_w_out, att_w_in, att_qn_g, att_kn_g, att_w_out):
    bp, seq, d = x_prompt.shape
    bs, t_s, _ = x_sample.shape
    n_heads = dn_a_log.shape[-1]
    w = n_heads * HEAD_DIM
    sc_w = sc_conv_w.shape[-1]
    assert dn_w_in.shape[-1] == 4 * w + 2 * n_heads + 4 * sc_w and state_delta.shape[-2:] == (HEAD_DIM, HEAD_DIM)

    mod = _adaln(jnp.concatenate([c_prompt, c_sample], axis=0), ada_w, ada_b)
    mod_p, mod_s = mod[:, :bp], mod[:, bp:]

    wi = dn_w_in[0]
    w0 = jnp.concatenate([wi[:, :4 * w], _pad_lanes(wi[:, 4 * w:4 * w + 2 * n_heads]),
                          wi[:, 4 * w + 2 * n_heads:]], axis=1).astype(BF16)
    alog, dtb = _pad_lanes(dn_a_log[0][None]), _pad_lanes(dn_dt_bias[0][None])
    ng0 = norm_g[0][None]
    dng = dn_norm_g[0][None]
    w_out0 = dn_w_out[0].astype(BF16)
    nq, nsc = dn_conv_w.shape[1] - 1, sc_conv_w.shape[1] - 1

    q, k, v, gb, sza, yb, qkv_tail, sc_tail = _dn_in_prompt(
        x_prompt, mod_p[0][:, None], ng0, w0, dn_conv_w[0], alog, dtb, sc_conv_w[0], n_heads, sc_w)
    p_qkv, p_sc = qkv_tail[:, SUBLANES - nq:], sc_tail[:, SUBLANES - nsc:]

    tm_of = lambda a: jnp.swapaxes(a, 0, 1)
    qs, ks, vs, gbs, szas, ybs, s_qkv_tm, s_sc_tm = _dn_in_sample(
        tm_of(x_sample), mod_s[0], ng0, w0, tm_of(state_qkv_conv[0]), tm_of(state_short_conv[0]),
        dn_conv_w[0], alog, dtb, sc_conv_w[0], n_heads, sc_w)
    cs = DN_CHUNK_SAMPLE
    assert t_s <= cs
    pad_t = lambda a: jnp.pad(tm_of(a), ((0, 0), (0, cs - t_s), (0, 0)))
    o_s, s_delta = _delta_one_chunk(pad_t(qs), pad_t(ks), pad_t(vs), pad_t(gbs), state_delta[0],
                                    DN_ONE_CHUNK_SEQS)
    rows_s = bs * t_s
    flat = lambda a: a.reshape(1, rows_s, a.shape[-1])
    gate_rows = lambda m: flat(jnp.repeat(m, t_s, axis=0))
    xs1 = _dn_out(flat(o_s[:, :t_s]), flat(tm_of(szas)), flat(tm_of(ybs)), flat(x_sample),
                  gate_rows(mod_s[0][:, 2 * d:]), dng, w_out0, rows_s)
    s_qkv, s_sc = tm_of(s_qkv_tm), tm_of(s_sc_tm)

    wa = att_w_in[0].astype(BF16)
    ng1, qn, kn = norm_g[1][None], att_qn_g[0][None], att_kn_g[0][None]
    w_out1 = att_w_out[0].astype(BF16)
    n_att_heads = d // HEAD_DIM

    q1s, k1s, v1s, sz1s = _attn_in(xs1, gate_rows(mod_s[1]), ng1, wa, qn, kn, rows_s, False)
    seqs = lambda a: a.reshape(bs, t_s, n_att_heads, HEAD_DIM)
    o1s, prep = _moba_decode_with_prep(seqs(q1s), seqs(k1s), seqs(v1s), cache_k, cache_v, page_table, 0,
                                       (q, k, v, gb), DN_CHUNK_PROMPT)
    y_sample = _attn_out(flat(o1s.reshape(bs, t_s, d)), sz1s, xs1, gate_rows(mod_s[1][:, 2 * d:]), w_out1, rows_s)

    o_p, p_delta = _delta_scan(*prep, jnp.zeros((bp, n_heads, HEAD_DIM, HEAD_DIM), F32), DN_CHUNK_PROMPT, bp)
    xp1 = _dn_out(o_p, sza, yb, x_prompt, mod_p[0][:, None, 2 * d:], dng, w_out0, DN_OUT_ROW_TILE)
    q1, k1, v1, sz1, kb1, vt1, kmean = _attn_in(xp1, mod_p[1][:, None], ng1, wa, qn, kn, PROMPT_ROW_TILE, True)
    y_prompt = _moba_prompt(q1, kb1, vt1, kmean.reshape(bp, seq // MOBA_BLOCK, d), sz1, xp1,
                            mod_p[1][:, None, 2 * d:], w_out1)

    heads = lambda a, n: a.reshape(1, n, -1, n_att_heads, HEAD_DIM)
    return (y_prompt, y_sample.reshape(bs, t_s, d),
            p_delta[None], p_qkv[None], p_sc[None], heads(k1, bp), heads(v1, bp),
            s_delta[None], s_qkv[None], s_sc[None], heads(k1s, bs), heads(v1s, bs))
```

```python
import functools

import jax
import jax.numpy as jnp
from jax import lax
from jax.experimental import pallas as pl
from jax.experimental.pallas import tpu as pltpu

F32 = jnp.float32
BF16 = jnp.bfloat16

EPS = 1e-6
LANES = 128
SUBLANES = 8
SUBLANES_BF16 = 16
HEAD_DIM = 128
DN_CHUNK_PROMPT = 64
DN_CHUNK_SAMPLE = 16
DN_SOLVE_BLOCK = 16
MOBA_BLOCK = 256
MOBA_TOPK = 3
DECODE_VMEM_LIMIT = 48 * 1024 * 1024
DECODE_VMEM_RESERVE = 12 * 1024 * 1024
MOBA_KV_GROUP = 4
PROMPT_ROW_TILE = 512
DN_OUT_ROW_TILE = 1024
DN_ONE_CHUNK_SEQS = 16
DN_SCAN_CHUNKS_PER_STEP = 4
NEG = -1e30
DN_GRAM_PASSES = 1
DN_SOLVE_PASSES = 1
DN_SCAN_PASSES = 1
VMEM_LIMIT = 48 * 1024 * 1024
MOBA_VMEM_LIMIT = 58 * 1024 * 1024


def _cparams(sem, vmem_limit=VMEM_LIMIT):
    return pltpu.CompilerParams(dimension_semantics=sem, vmem_limit_bytes=vmem_limit)


def _silu(x):
    return x * jax.nn.sigmoid(x)


def _softplus(x):
    return jnp.maximum(x, 0.0) + jnp.log1p(jnp.exp(-jnp.abs(x)))


def _rms(x, g):
    return x * lax.rsqrt(jnp.mean(x * x, axis=-1, keepdims=True) + EPS) * g


def _modulate(x, norm_g, mod):
    d = x.shape[-1]
    return _rms(x, norm_g) * (1.0 + mod[:, d:2 * d]) + mod[:, :d]


def _dot_nt(a, b, precision=None):
    return lax.dot_general(a, b, (((1,), (1,)), ((), ())), precision=precision,
                           preferred_element_type=F32)


def _dot_tn(a, b, precision=None):
    return lax.dot_general(a, b, (((0,), (0,)), ((), ())), precision=precision,
                           preferred_element_type=F32)


def _adaln_kernel(c_ref, w_ref, b_ref, o_ref):
    o_ref[0] = jnp.dot(c_ref[...], w_ref[0], preferred_element_type=F32) + b_ref[0]


def _adaln(c_all, ada_w, ada_b):
    n_layers, d, d3 = ada_w.shape
    n = c_all.shape[0]
    tn = d
    return pl.pallas_call(
        _adaln_kernel,
        grid=(n_layers, d3 // tn),
        in_specs=[pl.BlockSpec((n, d), lambda l, j: (0, 0)),
                  pl.BlockSpec((1, d, tn), lambda l, j: (l, 0, j)),
                  pl.BlockSpec((1, 1, tn), lambda l, j: (l, 0, j))],
        out_specs=pl.BlockSpec((1, n, tn), lambda l, j: (l, 0, j)),
        out_shape=jax.ShapeDtypeStruct((n_layers, n, d3), F32),
        compiler_params=_cparams(("parallel", "parallel")),
        name="adaln_mod",
    )(c_all, ada_w, ada_b.reshape(n_layers, 1, d3))


def _dn_activations(qkv, ab, alog, dtb, n_heads):
    w = n_heads * HEAD_DIM
    act = _silu(qkv)
    qs, ks = [], []
    for h in range(n_heads):
        qh = act[:, h * HEAD_DIM:(h + 1) * HEAD_DIM]
        kh = act[:, w + h * HEAD_DIM:w + (h + 1) * HEAD_DIM]
        qs.append(qh * lax.rsqrt(jnp.sum(qh * qh, axis=-1, keepdims=True) + EPS) * (HEAD_DIM ** -0.5))
        ks.append(kh * lax.rsqrt(jnp.sum(kh * kh, axis=-1, keepdims=True) + EPS))
    q = jnp.concatenate(qs, axis=-1)
    k = jnp.concatenate(ks, axis=-1)
    v = act[:, 2 * w:3 * w]
    lane = lax.broadcasted_iota(jnp.int32, ab.shape, 1)
    g = -jnp.exp(alog) * _softplus(ab + dtb)
    beta = jax.nn.sigmoid(ab)
    gb = jnp.where(lane < n_heads, g, jnp.where(lane < 2 * n_heads, beta, 0.0))
    return q, k, v, gb


def _shift_rows(cur, prev, s):
    ext = jnp.concatenate([prev, cur], axis=0)
    return ext[SUBLANES - s:SUBLANES - s + cur.shape[0]]


def _conv_rows(cur, prev, w):
    width = w.shape[0]
    y = cur * w[width - 1:width]
    for s in range(1, width):
        y = y + _shift_rows(cur, prev, s) * w[width - 1 - s:width - s]
    return y


def _dn_in_prompt_kernel(x_ref, mod_ref, ng_ref, w_ref, cw_ref, alog_ref, dtb_ref, scw_ref,
                         q_ref, k_ref, v_ref, gb_ref, sza_ref, yb_ref, qkvt_ref, sct_ref,
                         prev_qkv, prev_sc, *, n_heads, sc_w):
    i = pl.program_id(1)
    w = n_heads * HEAD_DIM
    o_z, o_ab, o_sc = 3 * w, 4 * w, 4 * w + LANES

    @pl.when(i == 0)
    def _():
        prev_qkv[...] = jnp.zeros_like(prev_qkv)
        prev_sc[...] = jnp.zeros_like(prev_sc)

    h = _modulate(x_ref[0], ng_ref[...], mod_ref[0]).astype(BF16)
    raw = jnp.dot(h, w_ref[:, 0:o_z], preferred_element_type=F32)
    qkv = _conv_rows(raw, prev_qkv[...], cw_ref[...])
    ab = jnp.dot(h, w_ref[:, o_ab:o_sc], preferred_element_type=F32)
    q, k, v, gb = _dn_activations(qkv, ab, alog_ref[...], dtb_ref[...], n_heads)
    q_ref[0], k_ref[0], v_ref[0], gb_ref[0] = q, k, v, gb
    prev_qkv[...] = raw[-SUBLANES:]
    qkvt_ref[0] = raw[-SUBLANES:]

    sza_ref[0] = _silu(jnp.dot(h, w_ref[:, o_z:o_ab], preferred_element_type=F32)).astype(BF16)

    sc = jnp.dot(h, w_ref[:, o_sc:o_sc + 4 * sc_w], preferred_element_type=F32)
    cx = sc[:, sc_w:2 * sc_w] * sc[:, 2 * sc_w:3 * sc_w]
    cv = _conv_rows(cx, prev_sc[...], scw_ref[...])
    yb_ref[0] = (sc[:, 0:sc_w] * cv * _silu(sc[:, 3 * sc_w:4 * sc_w])).astype(BF16)
    prev_sc[...] = cx[-SUBLANES:]
    sct_ref[0] = cx[-SUBLANES:]


def _dn_in_prompt(x, mod, norm_g, w0, conv_w, alog, dtb, sc_conv_w, n_heads, sc_w):
    n, t, d = x.shape
    tm = PROMPT_ROW_TILE
    w = n_heads * HEAD_DIM
    wtot = w0.shape[1]
    row = lambda c: pl.BlockSpec((1, tm, c), lambda b, i: (b, i, 0))
    full = lambda a: pl.BlockSpec(a.shape, lambda b, i: (0,) * a.ndim)
    tail = lambda c: pl.BlockSpec((1, SUBLANES, c), lambda b, i: (b, 0, 0))
    outs = [jax.ShapeDtypeStruct((n, t, w), F32)] * 3 + [
        jax.ShapeDtypeStruct((n, t, LANES), F32), jax.ShapeDtypeStruct((n, t, w), BF16),
        jax.ShapeDtypeStruct((n, t, sc_w), BF16),
        jax.ShapeDtypeStruct((n, SUBLANES, 3 * w), F32), jax.ShapeDtypeStruct((n, SUBLANES, sc_w), F32)]
    return pl.pallas_call(
        functools.partial(_dn_in_prompt_kernel, n_heads=n_heads, sc_w=sc_w),
        grid=(n, t // tm),
        in_specs=[row(d), pl.BlockSpec((1, 1, 3 * d), lambda b, i: (b, 0, 0)), full(norm_g),
                  pl.BlockSpec((d, wtot), lambda b, i: (0, 0)), full(conv_w), full(alog), full(dtb),
                  full(sc_conv_w)],
        out_specs=[row(w), row(w), row(w), row(LANES), row(w), row(sc_w), tail(3 * w), tail(sc_w)],
        out_shape=outs,
        scratch_shapes=[pltpu.VMEM((SUBLANES, 3 * w), F32), pltpu.VMEM((SUBLANES, sc_w), F32)],
        compiler_params=_cparams(("parallel", "arbitrary")),
        name="dn_in_prompt",
    )(x, mod, norm_g, w0, conv_w, alog, dtb, sc_conv_w)


def _dn_in_sample_kernel(x_ref, mod_ref, ng_ref, w_ref, sq_ref, ssc_ref, cw_ref, alog_ref, dtb_ref,
                         scw_ref, q_ref, k_ref, v_ref, gb_ref, sza_ref, yb_ref, nq_ref, nsc_ref,
                         *, n_heads, sc_w):
    t_len = x_ref.shape[0]
    w = n_heads * HEAD_DIM
    o_z, o_ab, o_sc = 3 * w, 4 * w, 4 * w + LANES
    cw, scw = cw_ref[...], scw_ref[...]
    nq, nsc = cw.shape[0] - 1, scw.shape[0] - 1
    xq = [sq_ref[j] for j in range(nq)]
    xsc = [ssc_ref[j] for j in range(nsc)]
    hs, sc_all = [], []
    for t in range(t_len):
        h = _modulate(x_ref[t], ng_ref[...], mod_ref[...]).astype(BF16)
        hs.append(h)
        xq.append(jnp.dot(h, w_ref[:, 0:o_z], preferred_element_type=F32))
        sc = jnp.dot(h, w_ref[:, o_sc:o_sc + 4 * sc_w], preferred_element_type=F32)
        sc_all.append(sc)
        xsc.append(sc[:, sc_w:2 * sc_w] * sc[:, 2 * sc_w:3 * sc_w])
    for t in range(t_len):
        qkv = sum(xq[t + j] * cw[j:j + 1] for j in range(nq + 1))
        ab = jnp.dot(hs[t], w_ref[:, o_ab:o_sc], preferred_element_type=F32)
        q, k, v, gb = _dn_activations(qkv, ab, alog_ref[...], dtb_ref[...], n_heads)
        q_ref[t], k_ref[t], v_ref[t], gb_ref[t] = q, k, v, gb
        sza_ref[t] = _silu(jnp.dot(hs[t], w_ref[:, o_z:o_ab], preferred_element_type=F32)).astype(BF16)
        cv = sum(xsc[t + j] * scw[j:j + 1] for j in range(nsc + 1))
        sc = sc_all[t]
        yb_ref[t] = (sc[:, 0:sc_w] * cv * _silu(sc[:, 3 * sc_w:4 * sc_w])).astype(BF16)
    for j in range(nq):
        nq_ref[j] = xq[t_len + j]
    for j in range(nsc):
        nsc_ref[j] = xsc[t_len + j]


def _dn_in_sample(x_tm, mod, norm_g, w0, s_qkv_tm, s_sc_tm, conv_w, alog, dtb, sc_conv_w, n_heads, sc_w):
    t, n, d = x_tm.shape
    w = n_heads * HEAD_DIM
    gs = min(n, 128)
    wtot = w0.shape[1]
    tm3 = lambda r, c: pl.BlockSpec((r, gs, c), lambda g: (0, g, 0))
    full = lambda a: pl.BlockSpec(a.shape, lambda g: (0,) * a.ndim)
    nq, nsc = conv_w.shape[0] - 1, sc_conv_w.shape[0] - 1
    outs = [jax.ShapeDtypeStruct((t, n, w), F32)] * 3 + [
        jax.ShapeDtypeStruct((t, n, LANES), F32), jax.ShapeDtypeStruct((t, n, w), BF16),
        jax.ShapeDtypeStruct((t, n, sc_w), BF16),
        jax.ShapeDtypeStruct((nq, n, 3 * w), F32), jax.ShapeDtypeStruct((nsc, n, sc_w), F32)]
    return pl.pallas_call(
        functools.partial(_dn_in_sample_kernel, n_heads=n_heads, sc_w=sc_w),
        grid=(n // gs,),
        in_specs=[tm3(t, d), pl.BlockSpec((gs, 3 * d), lambda g: (g, 0)), full(norm_g),
                  pl.BlockSpec((d, wtot), lambda g: (0, 0)), tm3(nq, 3 * w), tm3(nsc, sc_w),
                  full(conv_w), full(alog), full(dtb), full(sc_conv_w)],
        out_specs=[tm3(t, w), tm3(t, w), tm3(t, w), tm3(t, LANES), tm3(t, w), tm3(t, sc_w),
                   tm3(nq, 3 * w), tm3(nsc, sc_w)],
        out_shape=outs,
        compiler_params=_cparams(("parallel",)),
        name="dn_in_sample",
    )(x_tm, mod, norm_g, w0, s_qkv_tm, s_sc_tm, conv_w, alog, dtb, sc_conv_w)


def _bf16_parts(x, n):
    parts, r = [], x
    for i in range(n):
        p = r.astype(BF16)
        parts.append(p)
        if i + 1 < n:
            r = r - p.astype(F32)
    return parts


def _mm(dot, a, b, passes):
    if passes == 1:
        return dot(a.astype(BF16), b.astype(BF16))
    a_hi, a_lo = _bf16_parts(a, 2)
    b_hi, b_lo = _bf16_parts(b, 2)
    return dot(a_hi, b_hi) + (dot(a_hi, b_lo) + dot(a_lo, b_hi))


def _mm_exact01(dot, sel, x):
    hi, mid, lo = (p.astype(sel.dtype) for p in _bf16_parts(x, 3))
    return dot(sel, hi) + (dot(sel, mid) + dot(sel, lo))


def _bdot(a, b):
    return jnp.einsum("bij,bjk->bik", a, b, preferred_element_type=F32)


def _bdot_nt(a, b):
    return jnp.einsum("bik,bjk->bij", a, b, preferred_element_type=F32)


def _dot(a, b):
    return jnp.dot(a, b, preferred_element_type=F32)


def _neumann_inverse(m, order, eye):
    p = eye - m
    mk, k = m, 1
    while 2 * k < order:
        mk = _mm(_bdot, mk, mk, DN_SOLVE_PASSES)
        yield
        p = p + _mm(_bdot, p, mk, DN_SOLVE_PASSES)
        yield
        k *= 2
    return p


def _unit_lower_inverse(m, c):
    row = lax.broadcasted_iota(jnp.int32, (c, c), 0)
    col = lax.broadcasted_iota(jnp.int32, (c, c), 1)
    eye = (row == col).astype(F32)
    if c <= DN_SOLVE_BLOCK:
        return (yield from _neumann_inverse(m, c, eye))
    on_diag = (row // DN_SOLVE_BLOCK) == (col // DN_SOLVE_BLOCK)
    m_diag = jnp.where(on_diag, m, 0.0)
    d_inv = yield from _neumann_inverse(m_diag, DN_SOLVE_BLOCK, eye)
    n_off = _mm(_bdot, d_inv, m - m_diag, DN_SOLVE_PASSES)
    yield
    n_inv = yield from _neumann_inverse(n_off, c // DN_SOLVE_BLOCK, eye)
    t_inv = _mm(_bdot, n_inv, d_inv, DN_SOLVE_PASSES)
    yield
    return t_inv


def _delta_prep_kernel(*refs, c):
    for _ in _delta_prep_stages(*refs, c=c):
        pass


def _delta_prep_stages(q_ref, k_ref, v_ref, gb_ref, u_ref, w_ref, qg_ref, kd_ref, el_ref, a_ref, *, c):
    g_dim, r_dim, wq = q_ref.shape
    n_heads = wq // HEAD_DIM
    rows = g_dim * r_dim
    nb = rows // c
    row = lax.broadcasted_iota(jnp.int32, (c, c), 0)
    col = lax.broadcasted_iota(jnp.int32, (c, c), 1)
    gb2 = gb_ref[...].reshape(rows, LANES)
    tril = jnp.broadcast_to((row >= col).astype(BF16), (nb, c, c))
    gc3 = _mm_exact01(_bdot, tril, gb2.reshape(nb, c, LANES))
    gc2 = gc3.reshape(rows, LANES)
    gt2 = _mm_exact01(_bdot, jnp.ones((nb, c, c), BF16), gb2.reshape(nb, c, LANES)).reshape(rows, LANES)
    yield
    hs = lambda h: slice(h * HEAD_DIM, (h + 1) * HEAD_DIM)
    stack = lambda f: jnp.concatenate([f(h) for h in range(n_heads)], axis=0)
    gch = stack(lambda h: jnp.broadcast_to(gc2[:, h:h + 1], (rows, LANES)).reshape(nb, c, LANES))
    beta = stack(lambda h: jnp.broadcast_to(gb2[:, n_heads + h:n_heads + h + 1], (rows, LANES)).reshape(nb, c, LANES))
    diag_gc = jnp.where(row == col, gch[:, :, :c], 0.0)
    gc_cols = _mm_exact01(_bdot, jnp.ones(diag_gc.shape, BF16), diag_gc)
    qh = stack(lambda h: q_ref[:, :, hs(h)].reshape(nb, c, HEAD_DIM))
    kh = stack(lambda h: k_ref[:, :, hs(h)].reshape(nb, c, HEAD_DIM))
    vh = stack(lambda h: v_ref[:, :, hs(h)].reshape(nb, c, HEAD_DIM))
    kb = kh * beta
    kk = _mm(_bdot_nt, kb, kh, DN_GRAM_PASSES)
    qk = _mm(_bdot_nt, qh, kh, DN_GRAM_PASSES)
    yield
    gc_last = stack(lambda h: jnp.broadcast_to(gt2[:, h:h + 1], (rows, LANES)).reshape(nb, c, LANES))
    decay = jnp.where(row >= col, jnp.exp(gch[:, :, :c] - gc_cols), 0.0)
    m = jnp.where(row > col, kk * decay, 0.0)
    a = qk * decay
    t_inv = yield from _unit_lower_inverse(m, c)
    egc = jnp.exp(gch)
    sol = _mm(_bdot, t_inv, jnp.concatenate([vh * beta, kb * egc], axis=-1), DN_SOLVE_PASSES)
    yield
    qg, kd, el = qh * egc, kh * jnp.exp(gc_last - gch), jnp.exp(gc_last)
    for h in range(n_heads):
        of_head = lambda x: x[h * nb:(h + 1) * nb].reshape(g_dim, r_dim, x.shape[-1])
        u_ref[:, :, hs(h)] = of_head(sol[:, :, :HEAD_DIM])
        el_ref[:, :, hs(h)] = of_head(el)
        w_ref[:, :, hs(h)] = of_head(sol[:, :, HEAD_DIM:]).astype(w_ref.dtype)
        qg_ref[:, :, hs(h)] = of_head(qg).astype(qg_ref.dtype)
        kd_ref[:, :, hs(h)] = of_head(kd).astype(kd_ref.dtype)
        a_ref[:, :, h * c:(h + 1) * c] = of_head(a).astype(a_ref.dtype)


def _delta_scan_kernel(u_ref, w_ref, qg_ref, kd_ref, el_ref, a_ref, s0_ref, o_ref, s_ref, *, c):
    @pl.when(pl.program_id(1) == 0)
    def _():
        s_ref[...] = s0_ref[...]

    for j in range(u_ref.shape[1] // c):
        _delta_scan_step(u_ref, w_ref, qg_ref, kd_ref, el_ref, a_ref, o_ref, s_ref, c=c, row0=j * c)


def _delta_one_chunk_kernel(q_ref, k_ref, v_ref, gb_ref, s0_ref, o_ref, s_ref, u, w, qg, kd, el, a, *, c):
    _delta_prep_kernel(q_ref, k_ref, v_ref, gb_ref, u, w, qg, kd, el, a, c=c)
    s_ref[...] = s0_ref[...]
    _delta_scan_step(u, w, qg, kd, el, a, o_ref, s_ref, c=c)


def _delta_scan_step(u_ref, w_ref, qg_ref, kd_ref, el_ref, a_ref, o_ref, s_ref, *, c, row0=0):
    g_dim, _, wq = u_ref.shape
    rs = slice(row0, row0 + c)
    chains = [(g, h, slice(h * HEAD_DIM, (h + 1) * HEAD_DIM)) for g in range(g_dim) for h in range(wq // HEAD_DIM)]
    states = [s_ref[g, h] for g, h, _ in chains]
    v_new = [u_ref[g, rs, sl] - _mm(_dot, w_ref[g, rs, sl], s, DN_SCAN_PASSES)
             for (g, _, sl), s in zip(chains, states)]
    o_inter = [_mm(_dot, qg_ref[g, rs, sl], s, DN_SCAN_PASSES) for (g, _, sl), s in zip(chains, states)]
    for (g, h, sl), s, vn, oi in zip(chains, states, v_new, o_inter):
        o_ref[g, rs, sl] = oi + _mm(_dot, a_ref[g, rs, h * c:(h + 1) * c], vn, DN_SCAN_PASSES)
    for (g, h, sl), s, vn in zip(chains, states, v_new):
        s_ref[g, h] = s * el_ref[g, row0:row0 + 1, sl] + _mm(_dot_tn, kd_ref[g, rs, sl], vn, DN_SCAN_PASSES)


def _delta_scan(u, w, qg, kd, el, a, s0, c, g_dim):
    n, t, wq = u.shape
    n_heads = wq // HEAD_DIM
    rows = c * DN_SCAN_CHUNKS_PER_STEP
    assert t % rows == 0
    blk = lambda cc: pl.BlockSpec((g_dim, rows, cc), lambda b, i: (b, i, 0))
    st = pl.BlockSpec((g_dim, n_heads, HEAD_DIM, HEAD_DIM), lambda b, i: (b, 0, 0, 0))
    return pl.pallas_call(
        functools.partial(_delta_scan_kernel, c=c),
        grid=(n // g_dim, t // rows),
        in_specs=[blk(wq)] * 5 + [blk(n_heads * c), st],
        out_specs=[blk(wq), st],
        out_shape=[jax.ShapeDtypeStruct((n, t, wq), F32),
                   jax.ShapeDtypeStruct((n, n_heads, HEAD_DIM, HEAD_DIM), F32)],
        compiler_params=_cparams(("parallel", "arbitrary")),
        name="delta_scan",
    )(u, w, qg, kd, el, a, s0)


def _delta_one_chunk(q, k, v, gb, s0, g_dim):
    n, c, wq = q.shape
    n_heads = wq // HEAD_DIM
    blk = lambda cc: pl.BlockSpec((g_dim, c, cc), lambda b: (b, 0, 0))
    st = pl.BlockSpec((g_dim, n_heads, HEAD_DIM, HEAD_DIM), lambda b: (b, 0, 0, 0))
    operand = BF16 if DN_SCAN_PASSES == 1 else F32
    tmp = lambda cc, dt: pltpu.VMEM((g_dim, c, cc), dt)
    return pl.pallas_call(
        functools.partial(_delta_one_chunk_kernel, c=c),
        grid=(n // g_dim,),
        in_specs=[blk(wq), blk(wq), blk(wq), blk(LANES), st],
        out_specs=[blk(wq), st],
        out_shape=[jax.ShapeDtypeStruct((n, c, wq), F32),
                   jax.ShapeDtypeStruct((n, n_heads, HEAD_DIM, HEAD_DIM), F32)],
        scratch_shapes=[tmp(wq, F32), tmp(wq, operand), tmp(wq, operand), tmp(wq, operand), tmp(wq, F32),
                        tmp(n_heads * c, operand)],
        compiler_params=_cparams(("parallel",)),
        name="delta_one_chunk",
    )(q, k, v, gb, s0)


def _dn_out_kernel(o_ref, sza_ref, yb_ref, x_ref, gate_ref, ng_ref, w_ref, y_ref):
    o = o_ref[0]
    wa = o.shape[-1]
    ya = jnp.concatenate([_rms(o[:, h * HEAD_DIM:(h + 1) * HEAD_DIM], ng_ref[...])
                          for h in range(wa // HEAD_DIM)], axis=-1) * sza_ref[0]
    out = (jnp.dot(ya.astype(BF16), w_ref[0:wa, :], preferred_element_type=F32)
           + jnp.dot(yb_ref[0].astype(BF16), w_ref[wa:, :], preferred_element_type=F32))
    y_ref[0] = x_ref[0] + gate_ref[0] * out


def _dn_out(o, sza, yb, x, gate, dn_norm_g, w_out, tm):
    n, t, d = x.shape
    wa, wb = o.shape[-1], yb.shape[-1]
    row = lambda c: pl.BlockSpec((1, tm, c), lambda b, i: (b, i, 0))
    gate_rows = gate.shape[1]
    gspec = (pl.BlockSpec((1, 1, d), lambda b, i: (b, 0, 0)) if gate_rows == 1
             else pl.BlockSpec((1, tm, d), lambda b, i: (b, i, 0)))
    return pl.pallas_call(
        _dn_out_kernel,
        grid=(n, t // tm),
        in_specs=[row(wa), row(wa), row(wb), row(d), gspec,
                  pl.BlockSpec(dn_norm_g.shape, lambda b, i: (0, 0)),
                  pl.BlockSpec(w_out.shape, lambda b, i: (0, 0))],
        out_specs=row(d),
        out_shape=jax.ShapeDtypeStruct((n, t, d), F32),
        compiler_params=_cparams(("parallel", "parallel")),
        name="dn_out",
    )(o, sza, yb, x, gate, dn_norm_g, w_out)


def _attn_in_kernel(x_ref, mod_ref, ng_ref, w_ref, qn_ref, kn_ref, q_ref, k_ref, v_ref, sz_ref, *prompt_refs):
    d = x_ref.shape[-1]
    h = _modulate(x_ref[0], ng_ref[...], mod_ref[0]).astype(BF16)
    heads = range(d // HEAD_DIM)
    qr = jnp.dot(h, w_ref[:, 0:d], preferred_element_type=F32)
    q_ref[0] = jnp.concatenate([_rms(qr[:, i * HEAD_DIM:(i + 1) * HEAD_DIM], qn_ref[...]) for i in heads], axis=-1)
    kr = jnp.dot(h, w_ref[:, d:2 * d], preferred_element_type=F32)
    k = jnp.concatenate([_rms(kr[:, i * HEAD_DIM:(i + 1) * HEAD_DIM], kn_ref[...]) for i in heads], axis=-1)
    k_ref[0] = k
    v = jnp.dot(h, w_ref[:, 2 * d:3 * d], preferred_element_type=F32)
    v_ref[0] = v
    sz_ref[0] = _silu(jnp.dot(h, w_ref[:, 3 * d:4 * d], preferred_element_type=F32)).astype(BF16)
    if prompt_refs:
        kb_ref, vt_ref, kmean_ref = prompt_refs
        kb_ref[0] = k.astype(BF16)
        vt_ref[0] = v.T.astype(BF16)
        for j in range(kmean_ref.shape[1]):
            kmean_ref[0, j] = jnp.mean(k[j * MOBA_BLOCK:(j + 1) * MOBA_BLOCK], axis=0, keepdims=True)


def _attn_in(x, mod, norm_g, w_in, qn_g, kn_g, tm, for_prompt):
    n, t, d = x.shape
    row = lambda: pl.BlockSpec((1, tm, d), lambda b, i: (b, i, 0))
    mod_rows = mod.shape[1]
    mspec = (pl.BlockSpec((1, 1, 3 * d), lambda b, i: (b, 0, 0)) if mod_rows == 1
             else pl.BlockSpec((1, tm, 3 * d), lambda b, i: (b, i, 0)))
    full = lambda a: pl.BlockSpec(a.shape, lambda b, i: (0,) * a.ndim)
    out_specs = [row()] * 4
    outs = [jax.ShapeDtypeStruct((n, t, d), F32)] * 3 + [jax.ShapeDtypeStruct((n, t, d), BF16)]
    if for_prompt:
        assert tm % MOBA_BLOCK == 0
        out_specs += [row(), pl.BlockSpec((1, d, tm), lambda b, i: (b, 0, i)),
                      pl.BlockSpec((1, tm // MOBA_BLOCK, 1, d), lambda b, i: (b, i, 0, 0))]
        outs += [jax.ShapeDtypeStruct((n, t, d), BF16), jax.ShapeDtypeStruct((n, d, t), BF16),
                 jax.ShapeDtypeStruct((n, t // MOBA_BLOCK, 1, d), F32)]
    return pl.pallas_call(
        _attn_in_kernel,
        grid=(n, t // tm),
        in_specs=[row(), mspec, full(norm_g), full(w_in), full(qn_g), full(kn_g)],
        out_specs=out_specs,
        out_shape=outs,
        compiler_params=_cparams(("parallel", "parallel")),
        name="attn_in",
    )(x, mod, norm_g, w_in, qn_g, kn_g)


def _topk_select(gate, cand, n_blocks, axis):
    blk = lax.broadcasted_iota(jnp.int32, gate.shape, axis)
    gm = jnp.where(cand, gate, -jnp.inf)
    rank = jnp.zeros(gate.shape, jnp.int32)
    for b in range(n_blocks):
        other = gm[b:b + 1, :] if axis == 0 else gm[:, b:b + 1]
        rank = rank + ((other > gm) | ((other == gm) & (b < blk))).astype(jnp.int32)
    return cand & (rank < MOBA_TOPK)


def _moba_prompt_kernel(q_ref, k_ref, vt_ref, kmean_ref, sz_ref, x_ref, gate_ref, w_ref, y_ref, acc_scr, sel_scr,
                        *, group):
    i = pl.program_id(1)
    blk = q_ref.shape[1]
    n_blocks = kmean_ref.shape[1]
    heads = range(q_ref.shape[2] // HEAD_DIM)
    hs = lambda h: slice(h * HEAD_DIM, (h + 1) * HEAD_DIM)
    start = pl.multiple_of(i * blk, blk)
    causal = (lax.broadcasted_iota(jnp.int32, (blk, blk), 0) <= lax.broadcasted_iota(jnp.int32, (blk, blk), 1))

    def values_and_ones(h, st, n_keys):
        return jnp.concatenate([vt_ref[0, hs(h), pl.ds(st, n_keys)], jnp.ones((SUBLANES_BF16, n_keys), BF16)], axis=0)

    qfs = [q_ref[0, :, hs(h)] for h in heads]
    qbs = [(qf * (HEAD_DIM ** -0.5)).astype(BF16) for qf in qfs]
    s_own = [_dot_nt(k_ref[0, pl.ds(start, blk), hs(h)], qbs[h]).astype(BF16) for h in heads]
    gates = [_dot_nt(kmean_ref[0, :, hs(h)].astype(BF16), qfs[h].astype(BF16)) for h in heads]
    m0, p_own = [], []
    for h in heads:
        s = jnp.where(causal, s_own[h], NEG)
        m = jnp.max(s, axis=0, keepdims=True)
        m0.append(m.astype(F32))
        p_own.append(jnp.exp(s - m))
    for h in heads:
        acc_scr[h] = jnp.dot(values_and_ones(h, start, blk), p_own[h], preferred_element_type=F32)
        cand = lax.broadcasted_iota(jnp.int32, gates[h].shape, 0) < i
        sel_scr[h] = _topk_select(gates[h], cand, n_blocks, 0).astype(F32)

    def past_blocks(grp, first_block):
        def body(jg, ms):
            b0 = first_block + jg * grp
            st = pl.multiple_of(b0 * blk, grp * blk)
            sgs = [_dot_nt(k_ref[0, pl.ds(st, grp * blk), hs(h)], qbs[h]).astype(BF16) for h in heads]
            new_m, pjs = [], []
            for h in heads:
                sj = [jnp.where(sel_scr[h, pl.ds(b0 + g, 1), :] > 0.5, sgs[h][g * blk:(g + 1) * blk], NEG)
                      for g in range(grp)]
                m_blk = functools.reduce(jnp.maximum, [jnp.max(x, axis=0, keepdims=True) for x in sj])
                m_new = jnp.maximum(ms[h], m_blk.astype(F32))
                m_b16 = m_new.astype(BF16)
                pjs.append(jnp.concatenate([jnp.exp(x - m_b16) for x in sj], axis=0))
                new_m.append(m_new)
            for h in heads:
                acc_scr[h] = (jnp.exp(ms[h] - new_m[h]) * acc_scr[h]
                              + jnp.dot(values_and_ones(h, st, grp * blk), pjs[h], preferred_element_type=F32))
            return tuple(new_m)
        return body

    whole = lax.div(i, group)
    ms = lax.fori_loop(0, whole, past_blocks(group, 0), tuple(m0))
    if group > 1:
        half = group // 2
        rest = i - whole * group
        lax.fori_loop(0, lax.div(rest + (half - 1), half), past_blocks(half, whole * group), ms)
    o = jnp.concatenate([(acc_scr[h][:HEAD_DIM] / acc_scr[h][HEAD_DIM:HEAD_DIM + 1]).T for h in heads], axis=1)
    out = jnp.dot((o * sz_ref[0]).astype(BF16), w_ref[...], preferred_element_type=F32)
    y_ref[0] = x_ref[0] + gate_ref[0] * out


def _moba_prompt(q, kb, vt, kmean, sz, x, gate, w_out):
    n, t, d = q.shape
    n_blocks = t // MOBA_BLOCK
    n_heads = d // HEAD_DIM
    group = max(g for g in range(1, MOBA_KV_GROUP + 1) if n_blocks % g == 0)
    row = pl.BlockSpec((1, MOBA_BLOCK, d), lambda b, i: (b, i, 0))
    once = dict(pipeline_mode=pl.Buffered(1))
    return pl.pallas_call(
        functools.partial(_moba_prompt_kernel, group=group),
        grid=(n, n_blocks),
        in_specs=[row, pl.BlockSpec((1, t, d), lambda b, i: (b, 0, 0), **once),
                  pl.BlockSpec((1, d, t), lambda b, i: (b, 0, 0), **once),
                  pl.BlockSpec((1, n_blocks, d), lambda b, i: (b, 0, 0)),
                  row, row, pl.BlockSpec((1, 1, d), lambda b, i: (b, 0, 0)),
                  pl.BlockSpec((d, d), lambda b, i: (0, 0), **once)],
        out_specs=row,
        out_shape=jax.ShapeDtypeStruct((n, t, d), F32),
        scratch_shapes=[pltpu.VMEM((n_heads, HEAD_DIM + SUBLANES_BF16, MOBA_BLOCK), F32),
                        pltpu.VMEM((n_heads, n_blocks, MOBA_BLOCK), F32)],
        compiler_params=_cparams(("parallel", "arbitrary"), MOBA_VMEM_LIMIT),
        name="moba_prompt",
    )(q, kb, vt, kmean, sz, x, gate, w_out)


def _moba_decode_kernel(pt_ref, q_ref, kn_ref, vn_ref, *refs, n_pages, pages_per_block, prep_chunk):
    del pt_ref
    k_refs, v_refs = refs[:n_pages], refs[n_pages:2 * n_pages]
    prep_in, o_ref, prep_out = refs[2 * n_pages:2 * n_pages + 4], refs[2 * n_pages + 4], refs[2 * n_pages + 5:]
    nb = n_pages // pages_per_block
    t_len, n_heads = q_ref.shape[1], q_ref.shape[2]
    rows = t_len * n_heads
    page = k_refs[0].shape[2]
    cols = page * n_heads
    qall = q_ref[0].reshape(rows, HEAD_DIM)
    qs = qall * (HEAD_DIM ** -0.5)
    same_head = (lax.broadcasted_iota(jnp.int32, (rows, cols), 1) % n_heads
                 == lax.broadcasted_iota(jnp.int32, (rows, cols), 0) % n_heads)
    rep = lambda col: jnp.broadcast_to(col, (rows, LANES))
    per_row = lambda a: jnp.broadcast_to(a[None], (t_len, n_heads, HEAD_DIM)).reshape(rows, HEAD_DIM)

    prep = _delta_prep_stages(*prep_in, *prep_out, c=prep_chunk)
    scores = []
    for j, r in enumerate(k_refs):
        scores.append(_dot_nt(qs, r[0, 0].reshape(cols, HEAD_DIM)))
        if j % pages_per_block == pages_per_block - 1:
            next(prep, None)
    m_blk, l_blk, gate, probs = [], [], [], []
    for b in range(nb):
        pages = range(b * pages_per_block, (b + 1) * pages_per_block)
        s = [jnp.where(same_head, scores[j], NEG) for j in pages]
        m_b = functools.reduce(jnp.maximum, [jnp.max(x, axis=1, keepdims=True) for x in s])
        p = [jnp.exp(x - m_b) for x in s]
        probs.append(p)
        m_blk.append(rep(m_b))
        l_blk.append(rep(sum(jnp.sum(x, axis=1, keepdims=True) for x in p)))
        kmean = sum(jnp.sum(k_refs[j][0, 0], axis=0) for j in pages) / (page * pages_per_block)
        gate.append(rep(jnp.sum(qall * per_row(kmean), axis=1, keepdims=True)))
    acc_blk = []
    for b in range(nb):
        acc_blk.append(sum(jnp.dot(x, v_refs[b * pages_per_block + j][0, 0].reshape(cols, HEAD_DIM),
                                   preferred_element_type=F32) for j, x in enumerate(probs[b])))
        next(prep, None)
    for _ in prep:
        pass


    sel = []
    for b in range(nb):
        beats = [(gate[o] >= gate[b]) if o < b else (gate[o] > gate[b]) for o in range(nb) if o != b]
        sel.append(sum(x.astype(jnp.int32) for x in beats) < MOBA_TOPK)
    tok = lax.broadcasted_iota(jnp.int32, (rows, LANES), 0) // n_heads
    s_own = [rep(jnp.sum(qs * per_row(kn_ref[0, j]), axis=1, keepdims=True)) for j in range(t_len)]
    m_all = functools.reduce(jnp.maximum, [jnp.where(tok >= j, s_own[j], NEG) for j in range(t_len)]
                             + [jnp.where(sel[b], m_blk[b], NEG) for b in range(nb)])
    den = jnp.zeros((rows, LANES), F32)
    num = jnp.zeros((rows, HEAD_DIM), F32)
    for j in range(t_len):
        pj = jnp.where(tok >= j, jnp.exp(s_own[j] - m_all), 0.0)
        den = den + pj
        num = num + pj * per_row(vn_ref[0, j])
    for b in range(nb):
        wb = jnp.where(sel[b], jnp.exp(m_blk[b] - m_all), 0.0)
        den = den + wb * l_blk[b]
        num = num + wb * acc_blk[b]
    o_ref[0] = (num / den).reshape(t_len, n_heads, HEAD_DIM)


def _moba_decode_with_prep(q, k_new, v_new, cache_k, cache_v, page_table, layer, prep_args, prep_chunk):
    n, t_len, n_heads, _ = q.shape
    n_pages = page_table.shape[1]
    page = cache_k.shape[2]
    assert MOBA_BLOCK % page == 0 and (n_pages * page) % MOBA_BLOCK == 0 and t_len <= MOBA_BLOCK
    page_bytes = page * n_heads * HEAD_DIM * cache_k.dtype.itemsize
    assert 2 * 2 * n_pages * page_bytes <= DECODE_VMEM_LIMIT - DECODE_VMEM_RESERVE
    tok = pl.BlockSpec((1, t_len, n_heads, HEAD_DIM), lambda s, pt: (s, 0, 0, 0))

    def page_spec(j):
        return pl.BlockSpec((1, 1, page, n_heads, HEAD_DIM), lambda s, pt: (layer, pt[s, j], 0, 0, 0))

    pn, pt_len, wq = prep_args[0].shape
    share = (pn * pt_len) // n
    assert share * n == pn * pt_len and share % prep_chunk == 0 and pt_len % share == 0
    dn_heads = wq // HEAD_DIM
    flat = lambda a: a.reshape(1, pn * pt_len, a.shape[-1])
    rows = lambda c: pl.BlockSpec((1, share, c), lambda s, pt: (0, s, 0))
    operand = BF16 if DN_SCAN_PASSES == 1 else F32
    widths = [(wq, F32), (wq, operand), (wq, operand), (wq, operand), (wq, F32), (dn_heads * prep_chunk, operand)]
    grid_spec = pltpu.PrefetchScalarGridSpec(
        num_scalar_prefetch=1,
        grid=(n,),
        in_specs=[tok, tok, tok] + [page_spec(j) for j in range(n_pages)] * 2 + [rows(wq)] * 3 + [rows(LANES)],
        out_specs=[tok] + [rows(c) for c, _ in widths])
    out = pl.pallas_call(
        functools.partial(_moba_decode_kernel, n_pages=n_pages, pages_per_block=MOBA_BLOCK // page,
                          prep_chunk=prep_chunk),
        grid_spec=grid_spec,
        out_shape=[jax.ShapeDtypeStruct((n, t_len, n_heads, HEAD_DIM), F32)]
                  + [jax.ShapeDtypeStruct((1, pn * pt_len, c), dt) for c, dt in widths],
        compiler_params=_cparams(("parallel",), DECODE_VMEM_LIMIT),
        name="moba_decode",
    )(page_table, q, k_new, v_new, *([cache_k] * n_pages), *([cache_v] * n_pages), *[flat(a) for a in prep_args])
    return out[0], [a.reshape(pn, pt_len, a.shape[-1]) for a in out[1:]]


def _attn_out_kernel(o_ref, sz_ref, x_ref, gate_ref, w_ref, y_ref):
    out = jnp.dot((o_ref[0] * sz_ref[0]).astype(BF16), w_ref[...], preferred_element_type=F32)
    y_ref[0] = x_ref[0] + gate_ref[0] * out


def _attn_out(o, sz, x, gate, w_out, tm):
    n, t, d = x.shape
    row = lambda: pl.BlockSpec((1, tm, d), lambda b, i: (b, i, 0))
    gspec = (pl.BlockSpec((1, 1, d), lambda b, i: (b, 0, 0)) if gate.shape[1] == 1 else row())
    return pl.pallas_call(
        _attn_out_kernel,
        grid=(n, t // tm),
        in_specs=[row(), row(), row(), gspec, pl.BlockSpec(w_out.shape, lambda b, i: (0, 0))],
        out_specs=row(),
        out_shape=jax.ShapeDtypeStruct((n, t, d), F32),
        compiler_params=_cparams(("parallel", "parallel")),
        name="attn_out",
    )(o, sz, x, gate, w_out)


def _pad_lanes(a):
    return jnp.pad(a, ((0, 0), (0, LANES - a.shape[-1])))


def kernel(x_prompt, x_sample, state_delta, state_qkv_conv, state_short_conv, cache_k, cache_v, page_table, c_prompt, c_sample, norm_g, ada_w, ada_b, dn_w_in, dn_conv_w, dn_a_log, dn_dt_bias, dn_norm_g, sc_conv_w, dn_w_out, att_w_in, att_qn_g, att_kn_g, att_w_out):
    bp, seq, d = x_prompt.shape
    bs, t_s, _ = x_sample.shape
    n_heads = dn_a_log.shape[-1]
    w = n_heads * HEAD_DIM
    sc_w = sc_conv_w.shape[-1]
    assert dn_w_in.shape[-1] == 4 * w + 2 * n_heads + 4 * sc_w and state_delta.shape[-2:] == (HEAD_DIM, HEAD_DIM)

    mod = _adaln(jnp.concatenate([c_prompt, c_sample], axis=0), ada_w, ada_b)
    mod_p, mod_s = mod[:, :bp], mod[:, bp:]

    wi = dn_w_in[0]
    w0 = jnp.concatenate([wi[:, :4 * w], _pad_lanes(wi[:, 4 * w:4 * w + 2 * n_heads]),
                          wi[:, 4 * w + 2 * n_heads:]], axis=1).astype(BF16)
    alog, dtb = _pad_lanes(dn_a_log[0][None]), _pad_lanes(dn_dt_bias[0][None])
    ng0 = norm_g[0][None]
    dng = dn_norm_g[0][None]
    w_out0 = dn_w_out[0].astype(BF16)
    nq, nsc = dn_conv_w.shape[1] - 1, sc_conv_w.shape[1] - 1

    q, k, v, gb, sza, yb, qkv_tail, sc_tail = _dn_in_prompt(
        x_prompt, mod_p[0][:, None], ng0, w0, dn_conv_w[0], alog, dtb, sc_conv_w[0], n_heads, sc_w)
    p_qkv, p_sc = qkv_tail[:, SUBLANES - nq:], sc_tail[:, SUBLANES - nsc:]

    tm_of = lambda a: jnp.swapaxes(a, 0, 1)
    qs, ks, vs, gbs, szas, ybs, s_qkv_tm, s_sc_tm = _dn_in_sample(
        tm_of(x_sample), mod_s[0], ng0, w0, tm_of(state_qkv_conv[0]), tm_of(state_short_conv[0]),
        dn_conv_w[0], alog, dtb, sc_conv_w[0], n_heads, sc_w)
    cs = DN_CHUNK_SAMPLE
    assert t_s <= cs
    pad_t = lambda a: jnp.pad(tm_of(a), ((0, 0), (0, cs - t_s), (0, 0)))
    o_s, s_delta = _delta_one_chunk(pad_t(qs), pad_t(ks), pad_t(vs), pad_t(gbs), state_delta[0],
                                    DN_ONE_CHUNK_SEQS)
    rows_s = bs * t_s
    flat = lambda a: a.reshape(1, rows_s, a.shape[-1])
    gate_rows = lambda m: flat(jnp.repeat(m, t_s, axis=0))
    xs1 = _dn_out(flat(o_s[:, :t_s]), flat(tm_of(szas)), flat(tm_of(ybs)), flat(x_sample),
                  gate_rows(mod_s[0][:, 2 * d:]), dng, w_out0, rows_s)
    s_qkv, s_sc = tm_of(s_qkv_tm), tm_of(s_sc_tm)

    wa = att_w_in[0].astype(BF16)
    ng1, qn, kn = norm_g[1][None], att_qn_g[0][None], att_kn_g[0][None]
    w_out1 = att_w_out[0].astype(BF16)
    n_att_heads = d // HEAD_DIM

    q1s, k1s, v1s, sz1s = _attn_in(xs1, gate_rows(mod_s[1]), ng1, wa, qn, kn, rows_s, False)
    seqs = lambda a: a.reshape(bs, t_s, n_att_heads, HEAD_DIM)
    o1s, prep = _moba_decode_with_prep(seqs(q1s), seqs(k1s), seqs(v1s), cache_k, cache_v, page_table, 0,
                                       (q, k, v, gb), DN_CHUNK_PROMPT)
    y_sample = _attn_out(flat(o1s.reshape(bs, t_s, d)), sz1s, xs1, gate_rows(mod_s[1][:, 2 * d:]), w_out1, rows_s)

    o_p, p_delta = _delta_scan(*prep, jnp.zeros((bp, n_heads, HEAD_DIM, HEAD_DIM), F32), DN_CHUNK_PROMPT, bp)
    xp1 = _dn_out(o_p, sza, yb, x_prompt, mod_p[0][:, None, 2 * d:], dng, w_out0, DN_OUT_ROW_TILE)
    q1, k1, v1, sz1, kb1, vt1, kmean = _attn_in(xp1, mod_p[1][:, None], ng1, wa, qn, kn, PROMPT_ROW_TILE, True)
    y_prompt = _moba_prompt(q1, kb1, vt1, kmean.reshape(bp, seq // MOBA_BLOCK, d), sz1, xp1,
                            mod_p[1][:, None, 2 * d:], w_out1)

    heads = lambda a, n: a.reshape(1, n, -1, n_att_heads, HEAD_DIM)
    return (y_prompt, y_sample.reshape(bs, t_s, d),
            p_delta[None], p_qkv[None], p_sc[None], heads(k1, bp), heads(v1, bp),
            s_delta[None], s_qkv[None], s_sc[None], heads(k1s, bs), heads(v1s, bs))
```

```python
import functools

import jax
import jax.numpy as jnp
from jax import lax
from jax.experimental import pallas as pl
from jax.experimental.pallas import tpu as pltpu

F32 = jnp.float32
BF16 = jnp.bfloat16

EPS = 1e-6
LANES = 128
SUBLANES = 8
SUBLANES_BF16 = 16
HEAD_DIM = 128
DN_CHUNK_PROMPT = 64
DN_CHUNK_SAMPLE = 16
DN_SOLVE_BLOCK = 16
MOBA_BLOCK = 256
MOBA_TOPK = 3
DECODE_VMEM_LIMIT = 48 * 1024 * 1024
DECODE_VMEM_RESERVE = 12 * 1024 * 1024
MOBA_KV_GROUP = 4
PROMPT_ROW_TILE = 512
DN_OUT_ROW_TILE = 1024
DN_ONE_CHUNK_SEQS = 16
DN_SCAN_CHUNKS_PER_STEP = 4
NEG = -1e30
DN_GRAM_PASSES = 1
DN_SOLVE_PASSES = 1
DN_SCAN_PASSES = 1
VMEM_LIMIT = 48 * 1024 * 1024
MOBA_VMEM_LIMIT = 58 * 1024 * 1024


def _cparams(sem, vmem_limit=VMEM_LIMIT):
    return pltpu.CompilerParams(dimension_semantics=sem, vmem_limit_bytes=vmem_limit)


def _silu(x):
    return x * jax.nn.sigmoid(x)


def _softplus(x):
    return jnp.maximum(x, 0.0) + jnp.log1p(jnp.exp(-jnp.abs(x)))


def _rms(x, g):
    return x * lax.rsqrt(jnp.mean(x * x, axis=-1, keepdims=True) + EPS) * g


def _modulate(x, norm_g, mod):
    d = x.shape[-1]
    return _rms(x, norm_g) * (1.0 + mod[:, d:2 * d]) + mod[:, :d]


def _dot_nt(a, b, precision=None):
    return lax.dot_general(a, b, (((1,), (1,)), ((), ())), precision=precision,
                           preferred_element_type=F32)


def _dot_tn(a, b, precision=None):
    return lax.dot_general(a, b, (((0,), (0,)), ((), ())), precision=precision,
                           preferred_element_type=F32)


def _adaln_kernel(c_ref, w_ref, b_ref, o_ref):
    o_ref[0] = jnp.dot(c_ref[...], w_ref[0], preferred_element_type=F32) + b_ref[0]


def _adaln(c_all, ada_w, ada_b):
    n_layers, d, d3 = ada_w.shape
    n = c_all.shape[0]
    tn = d
    return pl.pallas_call(
        _adaln_kernel,
        grid=(n_layers, d3 // tn),
        in_specs=[pl.BlockSpec((n, d), lambda l, j: (0, 0)),
                  pl.BlockSpec((1, d, tn), lambda l, j: (l, 0, j)),
                  pl.BlockSpec((1, 1, tn), lambda l, j: (l, 0, j))],
        out_specs=pl.BlockSpec((1, n, tn), lambda l, j: (l, 0, j)),
        out_shape=jax.ShapeDtypeStruct((n_layers, n, d3), F32),
        compiler_params=_cparams(("parallel", "parallel")),
        name="adaln_mod",
    )(c_all, ada_w, ada_b.reshape(n_layers, 1, d3))


def _dn_activations(qkv, ab, alog, dtb, n_heads):
    w = n_heads * HEAD_DIM
    act = _silu(qkv)
    qs, ks = [], []
    for h in range(n_heads):
        qh = act[:, h * HEAD_DIM:(h + 1) * HEAD_DIM]
        kh = act[:, w + h * HEAD_DIM:w + (h + 1) * HEAD_DIM]
        qs.append(qh * lax.rsqrt(jnp.sum(qh * qh, axis=-1, keepdims=True) + EPS) * (HEAD_DIM ** -0.5))
        ks.append(kh * lax.rsqrt(jnp.sum(kh * kh, axis=-1, keepdims=True) + EPS))
    q = jnp.concatenate(qs, axis=-1)
    k = jnp.concatenate(ks, axis=-1)
    v = act[:, 2 * w:3 * w]
    lane = lax.broadcasted_iota(jnp.int32, ab.shape, 1)
    g = -jnp.exp(alog) * _softplus(ab + dtb)
    beta = jax.nn.sigmoid(ab)
    gb = jnp.where(lane < n_heads, g, jnp.where(lane < 2 * n_heads, beta, 0.0))
    return q, k, v, gb


def _shift_rows(cur, prev, s):
    ext = jnp.concatenate([prev, cur], axis=0)
    return ext[SUBLANES - s:SUBLANES - s + cur.shape[0]]


def _conv_rows(cur, prev, w):
    width = w.shape[0]
    y = cur * w[width - 1:width]
    for s in range(1, width):
        y = y + _shift_rows(cur, prev, s) * w[width - 1 - s:width - s]
    return y


def _dn_in_prompt_kernel(x_ref, mod_ref, ng_ref, w_ref, cw_ref, alog_ref, dtb_ref, scw_ref,
                         q_ref, k_ref, v_ref, gb_ref, sza_ref, yb_ref, qkvt_ref, sct_ref,
                         prev_qkv, prev_sc, *, n_heads, sc_w):
    i = pl.program_id(1)
    w = n_heads * HEAD_DIM
    o_z, o_ab, o_sc = 3 * w, 4 * w, 4 * w + LANES

    @pl.when(i == 0)
    def _():
        prev_qkv[...] = jnp.zeros_like(prev_qkv)
        prev_sc[...] = jnp.zeros_like(prev_sc)

    h = _modulate(x_ref[0], ng_ref[...], mod_ref[0]).astype(BF16)
    raw = jnp.dot(h, w_ref[:, 0:o_z], preferred_element_type=F32)
    qkv = _conv_rows(raw, prev_qkv[...], cw_ref[...])
    ab = jnp.dot(h, w_ref[:, o_ab:o_sc], preferred_element_type=F32)
    q, k, v, gb = _dn_activations(qkv, ab, alog_ref[...], dtb_ref[...], n_heads)
    q_ref[0], k_ref[0], v_ref[0], gb_ref[0] = q, k, v, gb
    prev_qkv[...] = raw[-SUBLANES:]
    qkvt_ref[0] = raw[-SUBLANES:]

    sza_ref[0] = _silu(jnp.dot(h, w_ref[:, o_z:o_ab], preferred_element_type=F32)).astype(BF16)

    sc = jnp.dot(h, w_ref[:, o_sc:o_sc + 4 * sc_w], preferred_element_type=F32)
    cx = sc[:, sc_w:2 * sc_w] * sc[:, 2 * sc_w:3 * sc_w]
    cv = _conv_rows(cx, prev_sc[...], scw_ref[...])
    yb_ref[0] = (sc[:, 0:sc_w] * cv * _silu(sc[:, 3 * sc_w:4 * sc_w])).astype(BF16)
    prev_sc[...] = cx[-SUBLANES:]
    sct_ref[0] = cx[-SUBLANES:]


def _dn_in_prompt(x, mod, norm_g, w0, conv_w, alog, dtb, sc_conv_w, n_heads, sc_w):
    n, t, d = x.shape
    tm = PROMPT_ROW_TILE
    w = n_heads * HEAD_DIM
    wtot = w0.shape[1]
    row = lambda c: pl.BlockSpec((1, tm, c), lambda b, i: (b, i, 0))
    full = lambda a: pl.BlockSpec(a.shape, lambda b, i: (0,) * a.ndim)
    tail = lambda c: pl.BlockSpec((1, SUBLANES, c), lambda b, i: (b, 0, 0))
    outs = [jax.ShapeDtypeStruct((n, t, w), F32)] * 3 + [
        jax.ShapeDtypeStruct((n, t, LANES), F32), jax.ShapeDtypeStruct((n, t, w), BF16),
        jax.ShapeDtypeStruct((n, t, sc_w), BF16),
        jax.ShapeDtypeStruct((n, SUBLANES, 3 * w), F32), jax.ShapeDtypeStruct((n, SUBLANES, sc_w), F32)]
    return pl.pallas_call(
        functools.partial(_dn_in_prompt_kernel, n_heads=n_heads, sc_w=sc_w),
        grid=(n, t // tm),
        in_specs=[row(d), pl.BlockSpec((1, 1, 3 * d), lambda b, i: (b, 0, 0)), full(norm_g),
                  pl.BlockSpec((d, wtot), lambda b, i: (0, 0)), full(conv_w), full(alog), full(dtb),
                  full(sc_conv_w)],
        out_specs=[row(w), row(w), row(w), row(LANES), row(w), row(sc_w), tail(3 * w), tail(sc_w)],
        out_shape=outs,
        scratch_shapes=[pltpu.VMEM((SUBLANES, 3 * w), F32), pltpu.VMEM((SUBLANES, sc_w), F32)],
        compiler_params=_cparams(("parallel", "arbitrary")),
        name="dn_in_prompt",
    )(x, mod, norm_g, w0, conv_w, alog, dtb, sc_conv_w)


def _dn_in_sample_kernel(x_ref, mod_ref, ng_ref, w_ref, sq_ref, ssc_ref, cw_ref, alog_ref, dtb_ref,
                         scw_ref, q_ref, k_ref, v_ref, gb_ref, sza_ref, yb_ref, nq_ref, nsc_ref,
                         *, n_heads, sc_w):
    t_len = x_ref.shape[0]
    w = n_heads * HEAD_DIM
    o_z, o_ab, o_sc = 3 * w, 4 * w, 4 * w + LANES
    cw, scw = cw_ref[...], scw_ref[...]
    nq, nsc = cw.shape[0] - 1, scw.shape[0] - 1
    xq = [sq_ref[j] for j in range(nq)]
    xsc = [ssc_ref[j] for j in range(nsc)]
    hs, sc_all = [], []
    for t in range(t_len):
        h = _modulate(x_ref[t], ng_ref[...], mod_ref[...]).astype(BF16)
        hs.append(h)
        xq.append(jnp.dot(h, w_ref[:, 0:o_z], preferred_element_type=F32))
        sc = jnp.dot(h, w_ref[:, o_sc:o_sc + 4 * sc_w], preferred_element_type=F32)
        sc_all.append(sc)
        xsc.append(sc[:, sc_w:2 * sc_w] * sc[:, 2 * sc_w:3 * sc_w])
    for t in range(t_len):
        qkv = sum(xq[t + j] * cw[j:j + 1] for j in range(nq + 1))
        ab = jnp.dot(hs[t], w_ref[:, o_ab:o_sc], preferred_element_type=F32)
        q, k, v, gb = _dn_activations(qkv, ab, alog_ref[...], dtb_ref[...], n_heads)
        q_ref[t], k_ref[t], v_ref[t], gb_ref[t] = q, k, v, gb
        sza_ref[t] = _silu(jnp.dot(hs[t], w_ref[:, o_z:o_ab], preferred_element_type=F32)).astype(BF16)
        cv = sum(xsc[t + j] * scw[j:j + 1] for j in range(nsc + 1))
        sc = sc_all[t]
        yb_ref[t] = (sc[:, 0:sc_w] * cv * _silu(sc[:, 3 * sc_w:4 * sc_w])).astype(BF16)
    for j in range(nq):
        nq_ref[j] = xq[t_len + j]
    for j in range(nsc):
        nsc_ref[j] = xsc[t_len + j]


def _dn_in_sample(x_tm, mod, norm_g, w0, s_qkv_tm, s_sc_tm, conv_w, alog, dtb, sc_conv_w, n_heads, sc_w):
    t, n, d = x_tm.shape
    w = n_heads * HEAD_DIM
    gs = min(n, 128)
    wtot = w0.shape[1]
    tm3 = lambda r, c: pl.BlockSpec((r, gs, c), lambda g: (0, g, 0))
    full = lambda a: pl.BlockSpec(a.shape, lambda g: (0,) * a.ndim)
    nq, nsc = conv_w.shape[0] - 1, sc_conv_w.shape[0] - 1
    outs = [jax.ShapeDtypeStruct((t, n, w), F32)] * 3 + [
        jax.ShapeDtypeStruct((t, n, LANES), F32), jax.ShapeDtypeStruct((t, n, w), BF16),
        jax.ShapeDtypeStruct((t, n, sc_w), BF16),
        jax.ShapeDtypeStruct((nq, n, 3 * w), F32), jax.ShapeDtypeStruct((nsc, n, sc_w), F32)]
    return pl.pallas_call(
        functools.partial(_dn_in_sample_kernel, n_heads=n_heads, sc_w=sc_w),
        grid=(n // gs,),
        in_specs=[tm3(t, d), pl.BlockSpec((gs, 3 * d), lambda g: (g, 0)), full(norm_g),
                  pl.BlockSpec((d, wtot), lambda g: (0, 0)), tm3(nq, 3 * w), tm3(nsc, sc_w),
                  full(conv_w), full(alog), full(dtb), full(sc_conv_w)],
        out_specs=[tm3(t, w), tm3(t, w), tm3(t, w), tm3(t, LANES), tm3(t, w), tm3(t, sc_w),
                   tm3(nq, 3 * w), tm3(nsc, sc_w)],
        out_shape=outs,
        compiler_params=_cparams(("parallel",)),
        name="dn_in_sample",
    )(x_tm, mod, norm_g, w0, s_qkv_tm, s_sc_tm, conv_w, alog, dtb, sc_conv_w)


def _bf16_parts(x, n):
    parts, r = [], x
    for i in range(n):
        p = r.astype(BF16)
        parts.append(p)
        if i + 1 < n:
            r = r - p.astype(F32)
    return parts


def _mm(dot, a, b, passes):
    if passes == 1:
        return dot(a.astype(BF16), b.astype(BF16))
    a_hi, a_lo = _bf16_parts(a, 2)
    b_hi, b_lo = _bf16_parts(b, 2)
    return dot(a_hi, b_hi) + (dot(a_hi, b_lo) + dot(a_lo, b_hi))


def _mm_exact01(dot, sel, x):
    hi, mid, lo = (p.astype(sel.dtype) for p in _bf16_parts(x, 3))
    return dot(sel, hi) + (dot(sel, mid) + dot(sel, lo))


def _bdot(a, b):
    return jnp.einsum("bij,bjk->bik", a, b, preferred_element_type=F32)


def _bdot_nt(a, b):
    return jnp.einsum("bik,bjk->bij", a, b, preferred_element_type=F32)


def _dot(a, b):
    return jnp.dot(a, b, preferred_element_type=F32)


def _neumann_inverse(m, order, eye):
    p = eye - m
    mk, k = m, 1
    while 2 * k < order:
        mk = _mm(_bdot, mk, mk, DN_SOLVE_PASSES)
        yield
        p = p + _mm(_bdot, p, mk, DN_SOLVE_PASSES)
        yield
        k *= 2
    return p


def _unit_lower_inverse(m, c):
    row = lax.broadcasted_iota(jnp.int32, (c, c), 0)
    col = lax.broadcasted_iota(jnp.int32, (c, c), 1)
    eye = (row == col).astype(F32)
    if c <= DN_SOLVE_BLOCK:
        return (yield from _neumann_inverse(m, c, eye))
    on_diag = (row // DN_SOLVE_BLOCK) == (col // DN_SOLVE_BLOCK)
    m_diag = jnp.where(on_diag, m, 0.0)
    d_inv = yield from _neumann_inverse(m_diag, DN_SOLVE_BLOCK, eye)
    n_off = _mm(_bdot, d_inv, m - m_diag, DN_SOLVE_PASSES)
    yield
    n_inv = yield from _neumann_inverse(n_off, c // DN_SOLVE_BLOCK, eye)
    t_inv = _mm(_bdot, n_inv, d_inv, DN_SOLVE_PASSES)
    yield
    return t_inv


def _delta_prep_kernel(*refs, c):
    for _ in _delta_prep_stages(*refs, c=c):
        pass


def _delta_prep_stages(q_ref, k_ref, v_ref, gb_ref, u_ref, w_ref, qg_ref, kd_ref, el_ref, a_ref, *, c):
    g_dim, r_dim, wq = q_ref.shape
    n_heads = wq // HEAD_DIM
    rows = g_dim * r_dim
    nb = rows // c
    row = lax.broadcasted_iota(jnp.int32, (c, c), 0)
    col = lax.broadcasted_iota(jnp.int32, (c, c), 1)
    gb2 = gb_ref[...].reshape(rows, LANES)
    tril = jnp.broadcast_to((row >= col).astype(BF16), (nb, c, c))
    gc3 = _mm_exact01(_bdot, tril, gb2.reshape(nb, c, LANES))
    gc2 = gc3.reshape(rows, LANES)
    gt2 = _mm_exact01(_bdot, jnp.ones((nb, c, c), BF16), gb2.reshape(nb, c, LANES)).reshape(rows, LANES)
    yield
    hs = lambda h: slice(h * HEAD_DIM, (h + 1) * HEAD_DIM)
    stack = lambda f: jnp.concatenate([f(h) for h in range(n_heads)], axis=0)
    gch = stack(lambda h: jnp.broadcast_to(gc2[:, h:h + 1], (rows, LANES)).reshape(nb, c, LANES))
    beta = stack(lambda h: jnp.broadcast_to(gb2[:, n_heads + h:n_heads + h + 1], (rows, LANES)).reshape(nb, c, LANES))
    diag_gc = jnp.where(row == col, gch[:, :, :c], 0.0)
    gc_cols = _mm_exact01(_bdot, jnp.ones(diag_gc.shape, BF16), diag_gc)
    qh = stack(lambda h: q_ref[:, :, hs(h)].reshape(nb, c, HEAD_DIM))
    kh = stack(lambda h: k_ref[:, :, hs(h)].reshape(nb, c, HEAD_DIM))
    vh = stack(lambda h: v_ref[:, :, hs(h)].reshape(nb, c, HEAD_DIM))
    kb = kh * beta
    kk = _mm(_bdot_nt, kb, kh, DN_GRAM_PASSES)
    qk = _mm(_bdot_nt, qh, kh, DN_GRAM_PASSES)
    yield
    gc_last = stack(lambda h: jnp.broadcast_to(gt2[:, h:h + 1], (rows, LANES)).reshape(nb, c, LANES))
    decay = jnp.where(row >= col, jnp.exp(gch[:, :, :c] - gc_cols), 0.0)
    m = jnp.where(row > col, kk * decay, 0.0)
    a = qk * decay
    t_inv = yield from _unit_lower_inverse(m, c)
    egc = jnp.exp(gch)
    sol = _mm(_bdot, t_inv, jnp.concatenate([vh * beta, kb * egc], axis=-1), DN_SOLVE_PASSES)
    yield
    qg, kd, el = qh * egc, kh * jnp.exp(gc_last - gch), jnp.exp(gc_last)
    for h in range(n_heads):
        of_head = lambda x: x[h * nb:(h + 1) * nb].reshape(g_dim, r_dim, x.shape[-1])
        u_ref[:, :, hs(h)] = of_head(sol[:, :, :HEAD_DIM])
        el_ref[:, :, hs(h)] = of_head(el)
        w_ref[:, :, hs(h)] = of_head(sol[:, :, HEAD_DIM:]).astype(w_ref.dtype)
        qg_ref[:, :, hs(h)] = of_head(qg).astype(qg_ref.dtype)
        kd_ref[:, :, hs(h)] = of_head(kd).astype(kd_ref.dtype)
        a_ref[:, :, h * c:(h + 1) * c] = of_head(a).astype(a_ref.dtype)


def _delta_scan_kernel(u_ref, w_ref, qg_ref, kd_ref, el_ref, a_ref, s0_ref, o_ref, s_ref, *, c):
    @pl.when(pl.program_id(1) == 0)
    def _():
        s_ref[...] = s0_ref[...]

    for j in range(u_ref.shape[1] // c):
        _delta_scan_step(u_ref, w_ref, qg_ref, kd_ref, el_ref, a_ref, o_ref, s_ref, c=c, row0=j * c)


def _delta_one_chunk_kernel(q_ref, k_ref, v_ref, gb_ref, s0_ref, o_ref, s_ref, u, w, qg, kd, el, a, *, c):
    _delta_prep_kernel(q_ref, k_ref, v_ref, gb_ref, u, w, qg, kd, el, a, c=c)
    s_ref[...] = s0_ref[...]
    _delta_scan_step(u, w, qg, kd, el, a, o_ref, s_ref, c=c)


def _delta_scan_step(u_ref, w_ref, qg_ref, kd_ref, el_ref, a_ref, o_ref, s_ref, *, c, row0=0):
    g_dim, _, wq = u_ref.shape
    rs = slice(row0, row0 + c)
    chains = [(g, h, slice(h * HEAD_DIM, (h + 1) * HEAD_DIM)) for g in range(g_dim) for h in range(wq // HEAD_DIM)]
    states = [s_ref[g, h] for g, h, _ in chains]
    v_new = [u_ref[g, rs, sl] - _mm(_dot, w_ref[g, rs, sl], s, DN_SCAN_PASSES)
             for (g, _, sl), s in zip(chains, states)]
    o_inter = [_mm(_dot, qg_ref[g, rs, sl], s, DN_SCAN_PASSES) for (g, _, sl), s in zip(chains, states)]
    for (g, h, sl), s, vn, oi in zip(chains, states, v_new, o_inter):
        o_ref[g, rs, sl] = oi + _mm(_dot, a_ref[g, rs, h * c:(h + 1) * c], vn, DN_SCAN_PASSES)
    for (g, h, sl), s, vn in zip(chains, states, v_new):
        s_ref[g, h] = s * el_ref[g, row0:row0 + 1, sl] + _mm(_dot_tn, kd_ref[g, rs, sl], vn, DN_SCAN_PASSES)


def _delta_scan(u, w, qg, kd, el, a, s0, c, g_dim):
    n, t, wq = u.shape
    n_heads = wq // HEAD_DIM
    rows = c * DN_SCAN_CHUNKS_PER_STEP
    assert t % rows == 0
    blk = lambda cc: pl.BlockSpec((g_dim, rows, cc), lambda b, i: (b, i, 0))
    st = pl.BlockSpec((g_dim, n_heads, HEAD_DIM, HEAD_DIM), lambda b, i: (b, 0, 0, 0))
    return pl.pallas_call(
        functools.partial(_delta_scan_kernel, c=c),
        grid=(n // g_dim, t // rows),
        in_specs=[blk(wq)] * 5 + [blk(n_heads * c), st],
        out_specs=[blk(wq), st],
        out_shape=[jax.ShapeDtypeStruct((n, t, wq), F32),
                   jax.ShapeDtypeStruct((n, n_heads, HEAD_DIM, HEAD_DIM), F32)],
        compiler_params=_cparams(("parallel", "arbitrary")),
        name="delta_scan",
    )(u, w, qg, kd, el, a, s0)


def _delta_one_chunk(q, k, v, gb, s0, g_dim):
    n, c, wq = q.shape
    n_heads = wq // HEAD_DIM
    blk = lambda cc: pl.BlockSpec((g_dim, c, cc), lambda b: (b, 0, 0))
    st = pl.BlockSpec((g_dim, n_heads, HEAD_DIM, HEAD_DIM), lambda b: (b, 0, 0, 0))
    operand = BF16 if DN_SCAN_PASSES == 1 else F32
    tmp = lambda cc, dt: pltpu.VMEM((g_dim, c, cc), dt)
    return pl.pallas_call(
        functools.partial(_delta_one_chunk_kernel, c=c),
        grid=(n // g_dim,),
        in_specs=[blk(wq), blk(wq), blk(wq), blk(LANES), st],
        out_specs=[blk(wq), st],
        out_shape=[jax.ShapeDtypeStruct((n, c, wq), F32),
                   jax.ShapeDtypeStruct((n, n_heads, HEAD_DIM, HEAD_DIM), F32)],
        scratch_shapes=[tmp(wq, F32), tmp(wq, operand), tmp(wq, operand), tmp(wq, operand), tmp(wq, F32),
                        tmp(n_heads * c, operand)],
        compiler_params=_cparams(("parallel",)),
        name="delta_one_chunk",
    )(q, k, v, gb, s0)


def _dn_out_kernel(o_ref, sza_ref, yb_ref, x_ref, gate_ref, ng_ref, w_ref, y_ref):
    o = o_ref[0]
    wa = o.shape[-1]
    ya = jnp.concatenate([_rms(o[:, h * HEAD_DIM:(h + 1) * HEAD_DIM], ng_ref[...])
                          for h in range(wa // HEAD_DIM)], axis=-1) * sza_ref[0]
    out = (jnp.dot(ya.astype(BF16), w_ref[0:wa, :], preferred_element_type=F32)
           + jnp.dot(yb_ref[0].astype(BF16), w_ref[wa:, :], preferred_element_type=F32))
    y_ref[0] = x_ref[0] + gate_ref[0] * out


def _dn_out(o, sza, yb, x, gate, dn_norm_g, w_out, tm):
    n, t, d = x.shape
    wa, wb = o.shape[-1], yb.shape[-1]
    row = lambda c: pl.BlockSpec((1, tm, c), lambda b, i: (b, i, 0))
    gate_rows = gate.shape[1]
    gspec = (pl.BlockSpec((1, 1, d), lambda b, i: (b, 0, 0)) if gate_rows == 1
             else pl.BlockSpec((1, tm, d), lambda b, i: (b, i, 0)))
    return pl.pallas_call(
        _dn_out_kernel,
        grid=(n, t // tm),
        in_specs=[row(wa), row(wa), row(wb), row(d), gspec,
                  pl.BlockSpec(dn_norm_g.shape, lambda b, i: (0, 0)),
                  pl.BlockSpec(w_out.shape, lambda b, i: (0, 0))],
        out_specs=row(d),
        out_shape=jax.ShapeDtypeStruct((n, t, d), F32),
        compiler_params=_cparams(("parallel", "parallel")),
        name="dn_out",
    )(o, sza, yb, x, gate, dn_norm_g, w_out)


def _attn_in_kernel(x_ref, mod_ref, ng_ref, w_ref, qn_ref, kn_ref, q_ref, k_ref, v_ref, sz_ref, *prompt_refs):
    d = x_ref.shape[-1]
    h = _modulate(x_ref[0], ng_ref[...], mod_ref[0]).astype(BF16)
    heads = range(d // HEAD_DIM)
    qr = jnp.dot(h, w_ref[:, 0:d], preferred_element_type=F32)
    q_ref[0] = jnp.concatenate([_rms(qr[:, i * HEAD_DIM:(i + 1) * HEAD_DIM], qn_ref[...]) for i in heads], axis=-1)
    kr = jnp.dot(h, w_ref[:, d:2 * d], preferred_element_type=F32)
    k = jnp.concatenate([_rms(kr[:, i * HEAD_DIM:(i + 1) * HEAD_DIM], kn_ref[...]) for i in heads], axis=-1)
    k_ref[0] = k
    v = jnp.dot(h, w_ref[:, 2 * d:3 * d], preferred_element_type=F32)
    v_ref[0] = v
    sz_ref[0] = _silu(jnp.dot(h, w_ref[:, 3 * d:4 * d], preferred_element_type=F32)).astype(BF16)
    if prompt_refs:
        kb_ref, vt_ref, kmean_ref = prompt_refs
        kb_ref[0] = k.astype(BF16)
        vt_ref[0] = v.T.astype(BF16)
        for j in range(kmean_ref.shape[1]):
            kmean_ref[0, j] = jnp.mean(k[j * MOBA_BLOCK:(j + 1) * MOBA_BLOCK], axis=0, keepdims=True)


def _attn_in(x, mod, norm_g, w_in, qn_g, kn_g, tm, for_prompt):
    n, t, d = x.shape
    row = lambda: pl.BlockSpec((1, tm, d), lambda b, i: (b, i, 0))
    mod_rows = mod.shape[1]
    mspec = (pl.BlockSpec((1, 1, 3 * d), lambda b, i: (b, 0, 0)) if mod_rows == 1
             else pl.BlockSpec((1, tm, 3 * d), lambda b, i: (b, i, 0)))
    full = lambda a: pl.BlockSpec(a.shape, lambda b, i: (0,) * a.ndim)
    out_specs = [row()] * 4
    outs = [jax.ShapeDtypeStruct((n, t, d), F32)] * 3 + [jax.ShapeDtypeStruct((n, t, d), BF16)]
    if for_prompt:
        assert tm % MOBA_BLOCK == 0
        out_specs += [row(), pl.BlockSpec((1, d, tm), lambda b, i: (b, 0, i)),
                      pl.BlockSpec((1, tm // MOBA_BLOCK, 1, d), lambda b, i: (b, i, 0, 0))]
        outs += [jax.ShapeDtypeStruct((n, t, d), BF16), jax.ShapeDtypeStruct((n, d, t), BF16),
                 jax.ShapeDtypeStruct((n, t // MOBA_BLOCK, 1, d), F32)]
    return pl.pallas_call(
        _attn_in_kernel,
        grid=(n, t // tm),
        in_specs=[row(), mspec, full(norm_g), full(w_in), full(qn_g), full(kn_g)],
        out_specs=out_specs,
        out_shape=outs,
        compiler_params=_cparams(("parallel", "parallel")),
        name="attn_in",
    )(x, mod, norm_g, w_in, qn_g, kn_g)


def _topk_select(gate, cand, n_blocks, axis):
    blk = lax.broadcasted_iota(jnp.int32, gate.shape, axis)
    gm = jnp.where(cand, gate, -jnp.inf)
    rank = jnp.zeros(gate.shape, jnp.int32)
    for b in range(n_blocks):
        other = gm[b:b + 1, :] if axis == 0 else gm[:, b:b + 1]
        rank = rank + ((other > gm) | ((other == gm) & (b < blk))).astype(jnp.int32)
    return cand & (rank < MOBA_TOPK)


def _moba_prompt_kernel(q_ref, k_ref, vt_ref, kmean_ref, sz_ref, x_ref, gate_ref, w_ref, y_ref, acc_scr, sel_scr,
                        *, group):
    i = pl.program_id(1)
    blk = q_ref.shape[1]
    n_blocks = kmean_ref.shape[1]
    heads = range(q_ref.shape[2] // HEAD_DIM)
    hs = lambda h: slice(h * HEAD_DIM, (h + 1) * HEAD_DIM)
    start = pl.multiple_of(i * blk, blk)
    causal = (lax.broadcasted_iota(jnp.int32, (blk, blk), 0) <= lax.broadcasted_iota(jnp.int32, (blk, blk), 1))

    def values_and_ones(h, st, n_keys):
        return jnp.concatenate([vt_ref[0, hs(h), pl.ds(st, n_keys)], jnp.ones((SUBLANES_BF16, n_keys), BF16)], axis=0)

    qfs = [q_ref[0, :, hs(h)] for h in heads]
    qbs = [(qf * (HEAD_DIM ** -0.5)).astype(BF16) for qf in qfs]
    s_own = [_dot_nt(k_ref[0, pl.ds(start, blk), hs(h)], qbs[h]).astype(BF16) for h in heads]
    gates = [_dot_nt(kmean_ref[0, :, hs(h)].astype(BF16), qfs[h].astype(BF16)) for h in heads]
    m0, p_own = [], []
    for h in heads:
        s = jnp.where(causal, s_own[h], NEG)
        m = jnp.max(s, axis=0, keepdims=True)
        m0.append(m.astype(F32))
        p_own.append(jnp.exp(s - m))
    for h in heads:
        acc_scr[h] = jnp.dot(values_and_ones(h, start, blk), p_own[h], preferred_element_type=F32)
        cand = lax.broadcasted_iota(jnp.int32, gates[h].shape, 0) < i
        sel_scr[h] = _topk_select(gates[h], cand, n_blocks, 0).astype(F32)

    def past_blocks(grp, first_block):
        def body(jg, ms):
            b0 = first_block + jg * grp
            st = pl.multiple_of(b0 * blk, grp * blk)
            sgs = [_dot_nt(k_ref[0, pl.ds(st, grp * blk), hs(h)], qbs[h]).astype(BF16) for h in heads]
            new_m, pjs = [], []
            for h in heads:
                sj = [jnp.where(sel_scr[h, pl.ds(b0 + g, 1), :] > 0.5, sgs[h][g * blk:(g + 1) * blk], NEG)
                      for g in range(grp)]
                m_blk = functools.reduce(jnp.maximum, [jnp.max(x, axis=0, keepdims=True) for x in sj])
                m_new = jnp.maximum(ms[h], m_blk.astype(F32))
                m_b16 = m_new.astype(BF16)
                pjs.append(jnp.concatenate([jnp.exp(x - m_b16) for x in sj], axis=0))
                new_m.append(m_new)
            for h in heads:
                acc_scr[h] = (jnp.exp(ms[h] - new_m[h]) * acc_scr[h]
                              + jnp.dot(values_and_ones(h, st, grp * blk), pjs[h], preferred_element_type=F32))
            return tuple(new_m)
        return body

    ms, done, rest, size = tuple(m0), 0, i, group
    while size >= 1:
        count = lax.div(rest, size)
        ms = lax.fori_loop(0, count, past_blocks(size, done), ms)
        done, rest, size = done + count * size, rest - count * size, size // 2
    o = jnp.concatenate([(acc_scr[h][:HEAD_DIM] / acc_scr[h][HEAD_DIM:HEAD_DIM + 1]).T for h in heads], axis=1)
    out = jnp.dot((o * sz_ref[0]).astype(BF16), w_ref[...], preferred_element_type=F32)
    y_ref[0] = x_ref[0] + gate_ref[0] * out


def _moba_prompt(q, kb, vt, kmean, sz, x, gate, w_out):
    n, t, d = q.shape
    n_blocks = t // MOBA_BLOCK
    n_heads = d // HEAD_DIM
    group = max(g for g in range(1, MOBA_KV_GROUP + 1) if n_blocks % g == 0)
    row = pl.BlockSpec((1, MOBA_BLOCK, d), lambda b, i: (b, i, 0))
    once = dict(pipeline_mode=pl.Buffered(1))
    return pl.pallas_call(
        functools.partial(_moba_prompt_kernel, group=group),
        grid=(n, n_blocks),
        in_specs=[row, pl.BlockSpec((1, t, d), lambda b, i: (b, 0, 0), **once),
                  pl.BlockSpec((1, d, t), lambda b, i: (b, 0, 0), **once),
                  pl.BlockSpec((1, n_blocks, d), lambda b, i: (b, 0, 0)),
                  row, row, pl.BlockSpec((1, 1, d), lambda b, i: (b, 0, 0)),
                  pl.BlockSpec((d, d), lambda b, i: (0, 0), **once)],
        out_specs=row,
        out_shape=jax.ShapeDtypeStruct((n, t, d), F32),
        scratch_shapes=[pltpu.VMEM((n_heads, HEAD_DIM + SUBLANES_BF16, MOBA_BLOCK), F32),
                        pltpu.VMEM((n_heads, n_blocks, MOBA_BLOCK), F32)],
        compiler_params=_cparams(("parallel", "arbitrary"), MOBA_VMEM_LIMIT),
        name="moba_prompt",
    )(q, kb, vt, kmean, sz, x, gate, w_out)


def _moba_decode_kernel(pt_ref, q_ref, kn_ref, vn_ref, *refs, n_pages, pages_per_block, prep_chunk):
    del pt_ref
    k_refs, v_refs = refs[:n_pages], refs[n_pages:2 * n_pages]
    prep_in, o_ref, prep_out = refs[2 * n_pages:2 * n_pages + 4], refs[2 * n_pages + 4], refs[2 * n_pages + 5:]
    nb = n_pages // pages_per_block
    t_len, n_heads = q_ref.shape[1], q_ref.shape[2]
    rows = t_len * n_heads
    page = k_refs[0].shape[2]
    cols = page * n_heads
    qall = q_ref[0].reshape(rows, HEAD_DIM)
    qs = qall * (HEAD_DIM ** -0.5)
    same_head = (lax.broadcasted_iota(jnp.int32, (rows, cols), 1) % n_heads
                 == lax.broadcasted_iota(jnp.int32, (rows, cols), 0) % n_heads)
    rep = lambda col: jnp.broadcast_to(col, (rows, LANES))
    per_row = lambda a: jnp.broadcast_to(a[None], (t_len, n_heads, HEAD_DIM)).reshape(rows, HEAD_DIM)

    prep = _delta_prep_stages(*prep_in, *prep_out, c=prep_chunk)
    scores = []
    for j, r in enumerate(k_refs):
        scores.append(_dot_nt(qs, r[0, 0].reshape(cols, HEAD_DIM)))
        if j % pages_per_block == pages_per_block - 1:
            next(prep, None)
    m_blk, l_blk, gate, probs = [], [], [], []
    for b in range(nb):
        pages = range(b * pages_per_block, (b + 1) * pages_per_block)
        s = [jnp.where(same_head, scores[j], NEG) for j in pages]
        m_b = functools.reduce(jnp.maximum, [jnp.max(x, axis=1, keepdims=True) for x in s])
        p = [jnp.exp(x - m_b) for x in s]
        probs.append(p)
        m_blk.append(rep(m_b))
        l_blk.append(rep(sum(jnp.sum(x, axis=1, keepdims=True) for x in p)))
        kmean = sum(jnp.sum(k_refs[j][0, 0], axis=0) for j in pages) / (page * pages_per_block)
        gate.append(rep(jnp.sum(qall * per_row(kmean), axis=1, keepdims=True)))
    acc_blk = []
    for b in range(nb):
        acc_blk.append(sum(jnp.dot(x, v_refs[b * pages_per_block + j][0, 0].reshape(cols, HEAD_DIM),
                                   preferred_element_type=F32) for j, x in enumerate(probs[b])))
        next(prep, None)
    for _ in prep:
        pass


    sel = []
    for b in range(nb):
        beats = [(gate[o] >= gate[b]) if o < b else (gate[o] > gate[b]) for o in range(nb) if o != b]
        sel.append(sum(x.astype(jnp.int32) for x in beats) < MOBA_TOPK)
    tok = lax.broadcasted_iota(jnp.int32, (rows, LANES), 0) // n_heads
    s_own = [rep(jnp.sum(qs * per_row(kn_ref[0, j]), axis=1, keepdims=True)) for j in range(t_len)]
    m_all = functools.reduce(jnp.maximum, [jnp.where(tok >= j, s_own[j], NEG) for j in range(t_len)]
                             + [jnp.where(sel[b], m_blk[b], NEG) for b in range(nb)])
    den = jnp.zeros((rows, LANES), F32)
    num = jnp.zeros((rows, HEAD_DIM), F32)
    for j in range(t_len):
        pj = jnp.where(tok >= j, jnp.exp(s_own[j] - m_all), 0.0)
        den = den + pj
        num = num + pj * per_row(vn_ref[0, j])
    for b in range(nb):
        wb = jnp.where(sel[b], jnp.exp(m_blk[b] - m_all), 0.0)
        den = den + wb * l_blk[b]
        num = num + wb * acc_blk[b]
    o_ref[0] = (num / den).reshape(t_len, n_heads, HEAD_DIM)


def _moba_decode_with_prep(q, k_new, v_new, cache_k, cache_v, page_table, layer, prep_args, prep_chunk):
    n, t_len, n_heads, _ = q.shape
    n_pages = page_table.shape[1]
    page = cache_k.shape[2]
    assert MOBA_BLOCK % page == 0 and (n_pages * page) % MOBA_BLOCK == 0 and t_len <= MOBA_BLOCK
    page_bytes = page * n_heads * HEAD_DIM * cache_k.dtype.itemsize
    assert 2 * 2 * n_pages * page_bytes <= DECODE_VMEM_LIMIT - DECODE_VMEM_RESERVE
    tok = pl.BlockSpec((1, t_len, n_heads, HEAD_DIM), lambda s, pt: (s, 0, 0, 0))

    def page_spec(j):
        return pl.BlockSpec((1, 1, page, n_heads, HEAD_DIM), lambda s, pt: (layer, pt[s, j], 0, 0, 0))

    pn, pt_len, wq = prep_args[0].shape
    share = (pn * pt_len) // n
    assert share * n == pn * pt_len and share % prep_chunk == 0 and pt_len % share == 0
    dn_heads = wq // HEAD_DIM
    flat = lambda a: a.reshape(1, pn * pt_len, a.shape[-1])
    rows = lambda c: pl.BlockSpec((1, share, c), lambda s, pt: (0, s, 0))
    operand = BF16 if DN_SCAN_PASSES == 1 else F32
    widths = [(wq, F32), (wq, operand), (wq, operand), (wq, operand), (wq, F32), (dn_heads * prep_chunk, operand)]
    grid_spec = pltpu.PrefetchScalarGridSpec(
        num_scalar_prefetch=1,
        grid=(n,),
        in_specs=[tok, tok, tok] + [page_spec(j) for j in range(n_pages)] * 2 + [rows(wq)] * 3 + [rows(LANES)],
        out_specs=[tok] + [rows(c) for c, _ in widths])
    out = pl.pallas_call(
        functools.partial(_moba_decode_kernel, n_pages=n_pages, pages_per_block=MOBA_BLOCK // page,
                          prep_chunk=prep_chunk),
        grid_spec=grid_spec,
        out_shape=[jax.ShapeDtypeStruct((n, t_len, n_heads, HEAD_DIM), F32)]
                  + [jax.ShapeDtypeStruct((1, pn * pt_len, c), dt) for c, dt in widths],
        compiler_params=_cparams(("parallel",), DECODE_VMEM_LIMIT),
        name="moba_decode",
    )(page_table, q, k_new, v_new, *([cache_k] * n_pages), *([cache_v] * n_pages), *[flat(a) for a in prep_args])
    return out[0], [a.reshape(pn, pt_len, a.shape[-1]) for a in out[1:]]


def _attn_out_kernel(o_ref, sz_ref, x_ref, gate_ref, w_ref, y_ref):
    out = jnp.dot((o_ref[0] * sz_ref[0]).astype(BF16), w_ref[...], preferred_element_type=F32)
    y_ref[0] = x_ref[0] + gate_ref[0] * out


def _attn_out(o, sz, x, gate, w_out, tm):
    n, t, d = x.shape
    row = lambda: pl.BlockSpec((1, tm, d), lambda b, i: (b, i, 0))
    gspec = (pl.BlockSpec((1, 1, d), lambda b, i: (b, 0, 0)) if gate.shape[1] == 1 else row())
    return pl.pallas_call(
        _attn_out_kernel,
        grid=(n, t // tm),
        in_specs=[row(), row(), row(), gspec, pl.BlockSpec(w_out.shape, lambda b, i: (0, 0))],
        out_specs=row(),
        out_shape=jax.ShapeDtypeStruct((n, t, d), F32),
        compiler_params=_cparams(("parallel", "parallel")),
        name="attn_out",
    )(o, sz, x, gate, w_out)


def _pad_lanes(a):
    return jnp.pad(a, ((0, 0), (0, LANES - a.shape[-1])))


def kernel(x_prompt, x_sample, state_delta, state_qkv_conv, state_short_conv, cache_k, cache_v, page_table, c_prompt, c_sample, norm_g, ada_w, ada_b, dn_w_in, dn_conv_w, dn_a_log, dn_dt_bias, dn_norm_g, sc_conv_w, dn_w_out, att_w_in, att_qn_g, att_kn_g, att_w_out):
    bp, seq, d = x_prompt.shape
    bs, t_s, _ = x_sample.shape
    n_heads = dn_a_log.shape[-1]
    w = n_heads * HEAD_DIM
    sc_w = sc_conv_w.shape[-1]
    assert dn_w_in.shape[-1] == 4 * w + 2 * n_heads + 4 * sc_w and state_delta.shape[-2:] == (HEAD_DIM, HEAD_DIM)

    mod = _adaln(jnp.concatenate([c_prompt, c_sample], axis=0), ada_w, ada_b)
    mod_p, mod_s = mod[:, :bp], mod[:, bp:]

    wi = dn_w_in[0]
    w0 = jnp.concatenate([wi[:, :4 * w], _pad_lanes(wi[:, 4 * w:4 * w + 2 * n_heads]),
                          wi[:, 4 * w + 2 * n_heads:]], axis=1).astype(BF16)
    alog, dtb = _pad_lanes(dn_a_log[0][None]), _pad_lanes(dn_dt_bias[0][None])
    ng0 = norm_g[0][None]
    dng = dn_norm_g[0][None]
    w_out0 = dn_w_out[0].astype(BF16)
    nq, nsc = dn_conv_w.shape[1] - 1, sc_conv_w.shape[1] - 1

    q, k, v, gb, sza, yb, qkv_tail, sc_tail = _dn_in_prompt(
        x_prompt, mod_p[0][:, None], ng0, w0, dn_conv_w[0], alog, dtb, sc_conv_w[0], n_heads, sc_w)
    p_qkv, p_sc = qkv_tail[:, SUBLANES - nq:], sc_tail[:, SUBLANES - nsc:]

    tm_of = lambda a: jnp.swapaxes(a, 0, 1)
    qs, ks, vs, gbs, szas, ybs, s_qkv_tm, s_sc_tm = _dn_in_sample(
        tm_of(x_sample), mod_s[0], ng0, w0, tm_of(state_qkv_conv[0]), tm_of(state_short_conv[0]),
        dn_conv_w[0], alog, dtb, sc_conv_w[0], n_heads, sc_w)
    cs = DN_CHUNK_SAMPLE
    assert t_s <= cs
    pad_t = lambda a: jnp.pad(tm_of(a), ((0, 0), (0, cs - t_s), (0, 0)))
    o_s, s_delta = _delta_one_chunk(pad_t(qs), pad_t(ks), pad_t(vs), pad_t(gbs), state_delta[0],
                                    DN_ONE_CHUNK_SEQS)
    rows_s = bs * t_s
    flat = lambda a: a.reshape(1, rows_s, a.shape[-1])
    gate_rows = lambda m: flat(jnp.repeat(m, t_s, axis=0))
    xs1 = _dn_out(flat(o_s[:, :t_s]), flat(tm_of(szas)), flat(tm_of(ybs)), flat(x_sample),
                  gate_rows(mod_s[0][:, 2 * d:]), dng, w_out0, rows_s)
    s_qkv, s_sc = tm_of(s_qkv_tm), tm_of(s_sc_tm)

    wa = att_w_in[0].astype(BF16)
    ng1, qn, kn = norm_g[1][None], att_qn_g[0][None], att_kn_g[0][None]
    w_out1 = att_w_out[0].astype(BF16)
    n_att_heads = d // HEAD_DIM

    q1s, k1s, v1s, sz1s = _attn_in(xs1, gate_rows(mod_s[1]), ng1, wa, qn, kn, rows_s, False)
    seqs = lambda a: a.reshape(bs, t_s, n_att_heads, HEAD_DIM)
    o1s, prep = _moba_decode_with_prep(seqs(q1s), seqs(k1s), seqs(v1s), cache_k, cache_v, page_table, 0,
                                       (q, k, v, gb), DN_CHUNK_PROMPT)
    y_sample = _attn_out(flat(o1s.reshape(bs, t_s, d)), sz1s, xs1, gate_rows(mod_s[1][:, 2 * d:]), w_out1, rows_s)

    o_p, p_delta = _delta_scan(*prep, jnp.zeros((bp, n_heads, HEAD_DIM, HEAD_DIM), F32), DN_CHUNK_PROMPT, bp)
    xp1 = _dn_out(o_p, sza, yb, x_prompt, mod_p[0][:, None, 2 * d:], dng, w_out0, DN_OUT_ROW_TILE)
    q1, k1, v1, sz1, kb1, vt1, kmean = _attn_in(xp1, mod_p[1][:, None], ng1, wa, qn, kn, PROMPT_ROW_TILE, True)
    y_prompt = _moba_prompt(q1, kb1, vt1, kmean.reshape(bp, seq // MOBA_BLOCK, d), sz1, xp1,
                            mod_p[1][:, None, 2 * d:], w_out1)

    heads = lambda a, n: a.reshape(1, n, -1, n_att_heads, HEAD_DIM)
    return (y_prompt, y_sample.reshape(bs, t_s, d),
            p_delta[None], p_qkv[None], p_sc[None], heads(k1, bp), heads(v1, bp),
            s_delta[None], s_qkv[None], s_sc[None], heads(k1s, bs), heads(v1s, bs))
```
